```python
import jax, jax.numpy as jnp
from jax import lax
import numpy as np

D_MODEL = 4096
BATCH = 4
SEQ = 4096
DEPTH = 2

N_META = 16
HEAD_DIM = 64
NORM_EPS = 1e-5
NEG_INF = -1e30

RWKV_WIDTH = 3 * D_MODEL // 8
RWKV_HEADS = RWKV_WIDTH // HEAD_DIM
RWKV_DECAY_RANK = 64
RWKV_ICLR_RANK = 64
RWKV_GATE_RANK = 128
RWKV_GN_EPS = 64e-5

SWA_WIDTH = 3 * D_MODEL // 8
SWA_Q_HEADS = SWA_WIDTH // HEAD_DIM
SWA_KV_HEADS = 3
SWA_GROUP = SWA_Q_HEADS // SWA_KV_HEADS
SWA_KV_WIDTH = SWA_KV_HEADS * HEAD_DIM
SWA_WINDOW = 128
SWA_BLOCK = 128
ROPE_THETA = 500000.0
ROPE_DIMS = HEAD_DIM // 4

HGRN_WIDTH = D_MODEL - RWKV_WIDTH - SWA_WIDTH
HGRN_HEAD_DIM = 128
HGRN_HEADS = HGRN_WIDTH // HGRN_HEAD_DIM
HGRN_CHUNK = 64

PAD_FRONT = SWA_BLOCK - N_META

PEER_HEADS = 8
PEER_N_KEYS = 128
PEER_N_EXPERTS = PEER_N_KEYS ** 2
PEER_QUERY_DIM = 256
PEER_HALF_DIM = PEER_QUERY_DIM // 2
PEER_TOPK = 16
PEER_TOKEN_BLOCK = 128

RWKV_SIZES = (RWKV_WIDTH, RWKV_WIDTH, RWKV_WIDTH, RWKV_DECAY_RANK, RWKV_ICLR_RANK, RWKV_GATE_RANK)
RWKV_COLS = sum(RWKV_SIZES)
SWA_SIZES = (SWA_WIDTH, SWA_KV_WIDTH, SWA_KV_WIDTH)
HGRN_SIZES = (HGRN_WIDTH, HGRN_WIDTH, HGRN_WIDTH, HGRN_WIDTH)
IN_COLS = RWKV_COLS + sum(SWA_SIZES) + sum(HGRN_SIZES)

kernel_name = 'hybrid_rwkv7_swa_hgrn2_peer'


def _offsets(sizes):
    offs, acc = [], 0
    for s in sizes[:-1]:
        acc += s
        offs.append(acc)
    return offs


def rms_norm(x, gain, eps=NORM_EPS):
    xf = x.astype(jnp.float32)
    y = xf * lax.rsqrt(jnp.mean(xf * xf, axis=-1, keepdims=True) + eps)
    return (y * gain.astype(jnp.float32)).astype(x.dtype)


def rope_tables(length):
    pos = jnp.arange(length, dtype=jnp.float32)
    inv_freq = ROPE_THETA ** (-jnp.arange(0, ROPE_DIMS, 2, dtype=jnp.float32) / ROPE_DIMS)
    ang = pos[:, None] * inv_freq[None, :]
    return jnp.cos(ang), jnp.sin(ang)


def apply_partial_rope(x, cos, sin):
    half = ROPE_DIMS // 2
    shape = (1, x.shape[1]) + (1,) * (x.ndim - 3) + (half,)
    c = cos.reshape(shape).astype(x.dtype)
    s = sin.reshape(shape).astype(x.dtype)
    x1 = x[..., :half]
    x2 = x[..., half:ROPE_DIMS]
    return jnp.concatenate([x1 * c - x2 * s, x2 * c + x1 * s, x[..., ROPE_DIMS:]], axis=-1)


def rwkv7_time_mix(pa, mu, w0, w2, a0, a2, g2, k_k, k_a, r_k, lnx_w, lnx_b):
    B, L, _ = pa.shape
    H, N = RWKV_HEADS, HEAD_DIM
    f32 = jnp.float32
    prev = jnp.pad(pa, ((0, 0), (1, 0), (0, 0)))[:, :-1]
    pa = pa + (prev - pa) * mu
    r, k, v, wl, al, gl = jnp.split(pa, _offsets(RWKV_SIZES), axis=-1)
    w = -jax.nn.softplus(-(w0 + jnp.tanh(wl) @ w2).astype(f32)) - 0.5
    decay = jnp.exp(-jnp.exp(w))
    a = jax.nn.sigmoid(a0 + al @ a2)
    g = jax.nn.sigmoid(gl) @ g2
    kk = (k * k_k).reshape(B, L, H, N).astype(f32)
    kk = kk / jnp.maximum(jnp.sqrt(jnp.sum(kk * kk, axis=-1, keepdims=True)), 1e-12)
    k = k * (1.0 + (a - 1.0) * k_a)

    def heads(t):
        return t.reshape(B, L, H, N).astype(f32).transpose(1, 0, 2, 3)

    xs = (heads(r), heads(k), heads(v), kk.transpose(1, 0, 2, 3), heads(a), heads(decay))

    def step(S, inp):
        r_t, k_t, v_t, kk_t, a_t, w_t = inp
        sa = jnp.einsum('bhvk,bhk->bhv', S, -kk_t)
        S = (S * w_t[:, :, None, :] + sa[..., None] * (kk_t * a_t)[:, :, None, :]
             + v_t[..., None] * k_t[:, :, None, :])
        return S, jnp.einsum('bhvk,bhk->bhv', S, r_t)

    _, y = lax.scan(step, jnp.zeros((B, H, N, N), f32), xs)
    y = y.transpose(1, 0, 2, 3)
    mean = jnp.mean(y, axis=-1, keepdims=True)
    var = jnp.mean(jnp.square(y - mean), axis=-1, keepdims=True)
    y = ((y - mean) * lax.rsqrt(var + RWKV_GN_EPS)).reshape(B, L, H * N)
    y = y * lnx_w.astype(f32) + lnx_b.astype(f32)
    rh = r.reshape(B, L, H, N).astype(f32)
    kh = k.reshape(B, L, H, N).astype(f32)
    vh = v.reshape(B, L, H, N).astype(f32)
    bonus = jnp.sum(rh * kh * r_k.reshape(H, N).astype(f32), axis=-1, keepdims=True) * vh
    y = (y + bonus.reshape(B, L, H * N)) * g.astype(f32)
    return y.astype(pa.dtype)


def swa_sink_attention(q, k, v, sinks, cos, sin):
    B, L, _ = q.shape
    Hk, G, Dh, Bk = SWA_KV_HEADS, SWA_GROUP, HEAD_DIM, SWA_BLOCK
    q = apply_partial_rope(q.reshape(B, L, Hk, G, Dh), cos, sin)
    k = apply_partial_rope(k.reshape(B, L, Hk, Dh), cos, sin)
    v = v.reshape(B, L, Hk, Dh)
    Lp = L + PAD_FRONT
    nb = Lp // Bk
    qb = jnp.pad(q, ((0, 0), (PAD_FRONT, 0), (0, 0), (0, 0), (0, 0))).reshape(B, nb, Bk, Hk, G, Dh)

    def band(t):
        tb = jnp.pad(t, ((0, 0), (PAD_FRONT, 0), (0, 0), (0, 0))).reshape(B, nb, Bk, Hk, Dh)
        prev = jnp.pad(tb, ((0, 0), (1, 0), (0, 0), (0, 0), (0, 0)))[:, :-1]
        return jnp.concatenate([prev, tb], axis=2)

    k_band, v_band = band(k), band(v)
    k_meta, v_meta = k[:, :N_META], v[:, :N_META]
    s_meta = jnp.einsum('bnqhgd,bmhd->bnhgqm', qb, k_meta)
    s_band = jnp.einsum('bnqhgd,bnkhd->bnhgqk', qb, k_band)
    s = jnp.concatenate([s_meta, s_band], axis=-1).astype(jnp.float32) * (HEAD_DIM ** -0.5)
    blk = jnp.arange(nb)[:, None]
    qpos = blk * Bk + jnp.arange(Bk)[None, :] - PAD_FRONT
    kpos = (blk - 1) * Bk + jnp.arange(2 * Bk)[None, :] - PAD_FRONT
    rel = qpos[:, :, None] - kpos[:, None, :]
    band_ok = (kpos[:, None, :] >= N_META) & (rel >= 0) & (rel < SWA_WINDOW)
    meta_ok = jnp.arange(N_META)[None, None, :] <= qpos[:, :, None]
    mask = jnp.concatenate([meta_ok, band_ok], axis=-1)
    s = jnp.where(mask[None, :, None, None], s, NEG_INF)
    sink = jnp.broadcast_to(sinks.reshape(1, 1, Hk, G, 1, 1).astype(jnp.float32), s.shape[:-1] + (1,))
    p = jax.nn.softmax(jnp.concatenate([s, sink], axis=-1), axis=-1).astype(v.dtype)
    out = (jnp.einsum('bnhgqm,bmhd->bnqhgd', p[..., :N_META], v_meta)
           + jnp.einsum('bnhgqk,bnkhd->bnqhgd', p[..., N_META:-1], v_band))
    return out.reshape(B, Lp, SWA_WIDTH)[:, PAD_FRONT:]


def hgrn2_mix(q, f_raw, i, g, lower_bound, norm_gain):
    B, L, _ = q.shape
    H, Dk, C = HGRN_HEADS, HGRN_HEAD_DIM, HGRN_CHUNK
    f32 = jnp.float32
    f = lower_bound.astype(f32) + (1.0 - lower_bound.astype(f32)) * jax.nn.sigmoid(f_raw.astype(f32))
    log_f = jnp.log(f)
    k = 1.0 - f
    pad = ((0, 0), (PAD_FRONT, 0), (0, 0))
    qp, kp, ip, lp = [jnp.pad(t.astype(f32), pad) for t in (q, k, i, log_f)]
    Lp = L + PAD_FRONT
    nc = Lp // C

    def chunks(t):
        return t.reshape(B, nc, C, H, Dk).transpose(1, 0, 3, 2, 4)

    tri = jnp.tril(jnp.ones((C, C), dtype=bool))

    def step(S, inp):
        qc, kc, ic, lc = inp
        Gc = jnp.cumsum(lc, axis=2)
        diff = Gc[:, :, :, None, :] - Gc[:, :, None, :, :]
        dec = jnp.exp(jnp.where(tri[:, :, None], diff, NEG_INF))
        attn = jnp.einsum('bhtk,bhtsk,bhsk->bhts', qc, dec, kc)
        o = attn @ ic + jnp.einsum('bhtk,bhkv->bhtv', qc * jnp.exp(Gc), S)
        g_last = Gc[:, :, -1:, :]
        S = (S * jnp.exp(g_last[:, :, 0, :])[..., None]
             + jnp.einsum('bhsk,bhsv->bhkv', kc * jnp.exp(g_last - Gc), ic))
        return S, o

    _, o = lax.scan(step, jnp.zeros((B, H, Dk, Dk), f32), (chunks(qp), chunks(kp), chunks(ip), chunks(lp)))
    o = o.transpose(1, 0, 3, 2, 4).reshape(B, Lp, H, Dk)[:, PAD_FRONT:]
    o = rms_norm(o, norm_gain.reshape(H, Dk)).reshape(B, L, HGRN_WIDTH)
    return (o * jax.nn.silu(g.astype(f32))).astype(q.dtype)


def peer_ffn(h, w_q, sub_keys, u_tab, v_tab):
    B, L, D = h.shape
    T = B * L
    t = h.reshape(T, D)
    q = (t @ w_q).reshape(T, PEER_HEADS, 2, PEER_HALF_DIM)
    s = jnp.einsum('thcd,hcnd->thcn', q, sub_keys).astype(jnp.float32)
    s1, i1 = lax.top_k(s[:, :, 0], PEER_TOPK)
    s2, i2 = lax.top_k(s[:, :, 1], PEER_TOPK)
    cand = (s1[..., :, None] + s2[..., None, :]).reshape(T, PEER_HEADS, PEER_TOPK * PEER_TOPK)
    cand_ids = (i1[..., :, None] * PEER_N_KEYS + i2[..., None, :]).reshape(T, PEER_HEADS, PEER_TOPK * PEER_TOPK)
    best, pos = lax.top_k(cand, PEER_TOPK)
    experts = jnp.take_along_axis(cand_ids, pos, axis=-1).reshape(T, PEER_HEADS * PEER_TOPK)
    gates = jax.nn.softmax(best, axis=-1).reshape(T, PEER_HEADS * PEER_TOPK)
    T_pad = -(-T // PEER_TOKEN_BLOCK) * PEER_TOKEN_BLOCK
    n_blk = T_pad // PEER_TOKEN_BLOCK
    extra = T_pad - T
    tb = jnp.pad(t, ((0, extra), (0, 0))).reshape(n_blk, PEER_TOKEN_BLOCK, D)
    eb = jnp.pad(experts, ((0, extra), (0, 0))).reshape(n_blk, PEER_TOKEN_BLOCK, -1)
    gb = jnp.pad(gates, ((0, extra), (0, 0))).reshape(n_blk, PEER_TOKEN_BLOCK, -1)

    def block(args):
        x_b, e_b, g_b = args
        hid = jax.nn.gelu(jnp.einsum('ted,td->te', u_tab[e_b], x_b).astype(jnp.float32), approximate=False)
        return jnp.einsum('te,ted->td', (g_b * hid).astype(v_tab.dtype), v_tab[e_b])

    out = lax.map(block, (tb, eb, gb))
    return out.reshape(T_pad, D)[:T].reshape(B, L, D).astype(h.dtype)


def setup_inputs(seed: int = 0) -> dict:
    key = jax.random.key(seed)
    ks = jax.random.split(key, 25)
    D = D_MODEL
    nrm = jax.random.normal
    f32 = jnp.float32
    return {
        'x': nrm(ks[0], (BATCH, SEQ, D), f32),
        'meta_tokens': nrm(ks[1], (N_META, D), f32),
        'norm_mix': 1.0 + 0.02 * nrm(ks[2], (DEPTH, D), f32),
        'norm_ffn': 1.0 + 0.02 * nrm(ks[3], (DEPTH, D), f32),
        'norm_final': 1.0 + 0.02 * nrm(ks[4], (D,), f32),
        'w_in': nrm(ks[5], (DEPTH, D, IN_COLS), f32) * D ** -0.5,
        'w_out': nrm(ks[6], (DEPTH, D, D), f32) * (0.5 * D ** -0.5),
        'rwkv_mu': jax.random.uniform(ks[7], (DEPTH, RWKV_COLS), f32),
        'rwkv_w0': -2.0 + 0.5 * nrm(ks[8], (DEPTH, RWKV_WIDTH), f32),
        'rwkv_w2': 0.1 * nrm(ks[9], (DEPTH, RWKV_DECAY_RANK, RWKV_WIDTH), f32),
        'rwkv_a0': 0.1 * nrm(ks[10], (DEPTH, RWKV_WIDTH), f32),
        'rwkv_a2': 0.1 * nrm(ks[11], (DEPTH, RWKV_ICLR_RANK, RWKV_WIDTH), f32),
        'rwkv_g2': nrm(ks[12], (DEPTH, RWKV_GATE_RANK, RWKV_WIDTH), f32) * RWKV_GATE_RANK ** -0.5,
        'rwkv_k_k': 0.85 + 0.02 * nrm(ks[13], (DEPTH, RWKV_WIDTH), f32),
        'rwkv_k_a': 1.0 + 0.02 * nrm(ks[14], (DEPTH, RWKV_WIDTH), f32),
        'rwkv_r_k': 0.1 * nrm(ks[15], (DEPTH, RWKV_WIDTH), f32),
        'rwkv_lnx_w': 1.0 + 0.02 * nrm(ks[16], (DEPTH, RWKV_WIDTH), f32),
        'rwkv_lnx_b': 0.01 * nrm(ks[17], (DEPTH, RWKV_WIDTH), f32),
        'swa_sinks': 0.5 * nrm(ks[18], (DEPTH, SWA_Q_HEADS), f32),
        'hgrn_lb': 0.5 * nrm(ks[19], (DEPTH, HGRN_WIDTH), f32),
        'hgrn_norm': 1.0 + 0.02 * nrm(ks[20], (DEPTH, HGRN_WIDTH), f32),
        'peer_wq': nrm(ks[21], (DEPTH, D, PEER_HEADS * PEER_QUERY_DIM), f32) * D ** -0.5,
        'peer_keys': nrm(ks[22], (DEPTH, PEER_HEADS, 2, PEER_N_KEYS, PEER_HALF_DIM), f32) * PEER_HALF_DIM ** -0.5,
        'peer_u': nrm(ks[23], (DEPTH, PEER_N_EXPERTS, D), f32) * D ** -0.5,
        'peer_v': 0.1 * nrm(ks[24], (DEPTH, PEER_N_EXPERTS, D), f32),
    }


def reference(x, meta_tokens, norm_mix, norm_ffn, norm_final, w_in, w_out,
              rwkv_mu, rwkv_w0, rwkv_w2, rwkv_a0, rwkv_a2, rwkv_g2, rwkv_k_k, rwkv_k_a, rwkv_r_k,
              rwkv_lnx_w, rwkv_lnx_b, swa_sinks, hgrn_lb, hgrn_norm,
              peer_wq, peer_keys, peer_u, peer_v):
    B = x.shape[0]
    meta = jnp.broadcast_to(meta_tokens.astype(x.dtype)[None], (B, N_META, D_MODEL))
    h = jnp.concatenate([meta, x], axis=1)
    L = h.shape[1]
    cos, sin = rope_tables(L)
    lb_all = jnp.cumsum(jax.nn.softmax(hgrn_lb.astype(jnp.float32), axis=0), axis=0)
    lb_all = lb_all - lb_all[0:1]
    for l in range(DEPTH):
        z = rms_norm(h, norm_mix[l])
        proj = z @ w_in[l]
        pa = proj[..., :RWKV_COLS]
        sq, sk, sv, hq, hf, hi, hg = jnp.split(proj[..., RWKV_COLS:], _offsets(SWA_SIZES + HGRN_SIZES), axis=-1)
        y_a = rwkv7_time_mix(pa, rwkv_mu[l], rwkv_w0[l], rwkv_w2[l], rwkv_a0[l], rwkv_a2[l], rwkv_g2[l],
                             rwkv_k_k[l], rwkv_k_a[l], rwkv_r_k[l], rwkv_lnx_w[l], rwkv_lnx_b[l])
        y_b = swa_sink_attention(sq, sk, sv, swa_sinks[l], cos, sin)
        y_c = hgrn2_mix(hq, hf, hi, hg, lb_all[l], hgrn_norm[l])
        mixed = jnp.concatenate([y_a.astype(h.dtype), y_b.astype(h.dtype), y_c.astype(h.dtype)], axis=-1)
        h = h + mixed @ w_out[l]
        h = h + peer_ffn(rms_norm(h, norm_ffn[l]), peer_wq[l], peer_keys[l], peer_u[l], peer_v[l])
    out = rms_norm(h, norm_final)[:, N_META:]
    return out
```

```python
import functools

import jax
import jax.numpy as jnp
from jax import lax
from jax.experimental import pallas as pl
from jax.experimental.pallas import tpu as pltpu

F32 = jnp.float32
BF16 = jnp.bfloat16
HI = lax.Precision.HIGHEST

N_META = 16
HEAD_DIM = 64
NORM_EPS = 1e-5
NEG_INF = -1e30
RWKV_GN_EPS = 64e-5
SWA_BLOCK = 128
PAD_FRONT = SWA_BLOCK - N_META
ROPE_THETA = 500000.0
ROPE_DIMS = HEAD_DIM // 4
HGRN_HEAD_DIM = 128
CHUNK = 64
PEER_TOPK = 16
LANES = 128
VMEM_LIMIT_MB = 56


def _cparams(sem, vmem_mb=VMEM_LIMIT_MB):
    return pltpu.CompilerParams(dimension_semantics=sem, vmem_limit_bytes=vmem_mb * 1024 * 1024)


def _pick(n, cands):
    for c in cands:
        if n % c == 0:
            return c
    raise ValueError(f"no tile in {cands} divides {n}")


def _roundup(n, m):
    return -(-n // m) * m


def _dot(a, b, prec=None):
    return jnp.dot(a, b, preferred_element_type=F32, precision=prec)


def _dot_nt(a, b, prec=None):
    return lax.dot_general(a, b, (((1,), (1,)), ((), ())), preferred_element_type=F32, precision=prec)


def _dot_tn(a, b, prec=None):
    return lax.dot_general(a, b, (((0,), (0,)), ((), ())), preferred_element_type=F32, precision=prec)


def _row_in_batch(i, tm, lp):
    pos0 = lax.rem(i * tm, lp)
    rb = pos0 + lax.broadcasted_iota(jnp.int32, (tm, 1), 0)
    return jnp.where(rb >= lp, rb - lp, rb)


def _sigmoid(x):
    return 1.0 / (1.0 + jnp.exp(-x))


def _norm_mm_body(x_ref, g_ref, w_ref, o_ref, z_ref, *, lp, tm):
    i = pl.program_id(0)

    @pl.when(pl.program_id(1) == 0)
    def _():
        x = x_ref[...]
        ms = jnp.mean(x * x, axis=-1, keepdims=True)
        y = x * lax.rsqrt(ms + NORM_EPS) * g_ref[...]
        z_ref[...] = jnp.where(_row_in_batch(i, tm, lp) >= PAD_FRONT, y, 0.0).astype(BF16)

    o_ref[...] = _dot(z_ref[...], w_ref[...])


def norm_matmul(h, gain, w_bf16, lp):
    tp, d = h.shape
    n = w_bf16.shape[1]
    tm = _pick(tp, (512, 256, 128))
    tn = _pick(n, (512, 256, 128))
    return pl.pallas_call(
        functools.partial(_norm_mm_body, lp=lp, tm=tm),
        grid=(tp // tm, n // tn),
        in_specs=[
            pl.BlockSpec((tm, d), lambda i, j: (i, 0)),
            pl.BlockSpec((1, d), lambda i, j: (0, 0)),
            pl.BlockSpec((d, tn), lambda i, j: (0, j)),
        ],
        out_specs=pl.BlockSpec((tm, tn), lambda i, j: (i, j)),
        out_shape=jax.ShapeDtypeStruct((tp, n), F32),
        scratch_shapes=[pltpu.VMEM((tm, d), BF16)],
        compiler_params=_cparams(("parallel", "arbitrary")),
        name="norm_in_proj",
    )(h, gain.reshape(1, d), w_bf16)


def _rwkv_prep_body(pa_ref, prev_ref, mu_ref, w0_ref, w2_ref, a0_ref, a2_ref, g2_ref, kk_ref, ka_ref,
                    out_ref, *, lp, tm, rw, rd, ri, rg):
    i = pl.program_id(0)
    x = pa_ref[...]
    rowi = lax.broadcasted_iota(jnp.int32, (tm, 1), 0)
    prev_row = jnp.where(lax.rem(i * tm, lp) == 0, 0.0, prev_ref[7:8, :])
    prev = jnp.where(rowi == 0, prev_row, pltpu.roll(x, 1, 0))
    xs = x + (prev - x) * mu_ref[...]
    r = xs[:, 0:rw]
    k = xs[:, rw:2 * rw]
    v = xs[:, 2 * rw:3 * rw]
    wl = xs[:, 3 * rw:3 * rw + rd]
    al = xs[:, 3 * rw + rd:3 * rw + rd + ri]
    gl = xs[:, 3 * rw + rd + ri:3 * rw + rd + ri + rg]
    wv = w0_ref[...] + _dot(jnp.tanh(wl), w2_ref[...], HI)
    nwv = -wv
    softplus = jnp.maximum(nwv, 0.0) + jnp.log(1.0 + jnp.exp(-jnp.abs(nwv)))
    logw = -jnp.exp(-softplus - 0.5)
    alr = _sigmoid(a0_ref[...] + _dot(al, a2_ref[...], HI))
    g = _dot(_sigmoid(gl), g2_ref[...], HI)
    kkr = k * kk_ref[...]
    shift = HEAD_DIM.bit_length() - 1
    er = lax.broadcasted_iota(jnp.int32, (LANES, LANES), 0) >> shift
    ec = lax.broadcasted_iota(jnp.int32, (LANES, LANES), 1) >> shift
    ones_bd = jnp.where(er == ec, 1.0, 0.0)
    sq = kkr * kkr
    ss = jnp.concatenate(
        [_dot(sq[:, s * LANES:(s + 1) * LANES], ones_bd, HI) for s in range(rw // LANES)], axis=1)
    kk = kkr / jnp.maximum(jnp.sqrt(ss), 1e-12)
    k2 = k * (1.0 + (alr - 1.0) * ka_ref[...])
    out_ref[0] = r
    out_ref[1] = k2
    out_ref[2] = v
    out_ref[3] = kk
    out_ref[4] = alr
    out_ref[5] = logw
    out_ref[6] = g


def rwkv_prep(proj, lp, p):
    tp = proj.shape[0]
    rw = p["w0"].shape[0]
    rd, ri, rg = p["w2"].shape[0], p["a2"].shape[0], p["g2"].shape[0]
    rc = 3 * rw + rd + ri + rg
    tm = 128
    row = lambda a: a.reshape(1, -1)
    full = lambda a: pl.BlockSpec(a.shape, lambda i: (0, 0))
    args = [row(p["mu"]), row(p["w0"]), p["w2"], row(p["a0"]), p["a2"], p["g2"], row(p["k_k"]), row(p["k_a"])]
    return pl.pallas_call(
        functools.partial(_rwkv_prep_body, lp=lp, tm=tm, rw=rw, rd=rd, ri=ri, rg=rg),
        grid=(tp // tm,),
        in_specs=[
            pl.BlockSpec((tm, rc), lambda i: (i, 0)),
            pl.BlockSpec((8, rc), lambda i: (jnp.maximum(i * (tm // 8) - 1, 0), 0)),
        ] + [full(a) for a in args],
        out_specs=pl.BlockSpec((7, tm, rw), lambda i: (0, i, 0)),
        out_shape=jax.ShapeDtypeStruct((7, tp, rw), F32),
        compiler_params=_cparams(("parallel",)),
        name="rwkv_prep",
    )(proj, proj, *args)


def _rwkv_head_chunk(r, k, v, kk, alr, logw, s0, ltri, strict, incl):
    c = r.shape[0]
    cum = _dot(ltri, logw, HI)
    p_in = jnp.exp(cum)
    p_ex = jnp.exp(cum - logw)
    p_inv = jnp.exp(-cum)
    a_t = -kk * p_ex
    r_t = r * p_in
    b_t = kk * alr * p_inv
    k_t = k * p_inv
    ar = jnp.concatenate([a_t, r_t], axis=0)
    bk = jnp.concatenate([b_t, k_t], axis=0)
    m = _dot_nt(ar, bk, HI)
    m_ab = jnp.where(strict, m[:c, :c], 0.0)
    m_ak = jnp.where(strict, m[:c, c:], 0.0)
    m_rb = jnp.where(incl, m[c:, :c], 0.0)
    m_rk = jnp.where(incl, m[c:, c:], 0.0)
    ars = _dot_nt(ar, s0, HI)
    u = ars[:c] + _dot(m_ak, v, HI)
    nk = m_ab
    steps = c.bit_length() - 1
    for it in range(steps):
        u = u + _dot(nk, u, HI)
        if it + 1 < steps:
            nk = _dot(nk, nk, HI)
    y = ars[c:] + _dot(m_rb, u, HI) + _dot(m_rk, v, HI)
    p_last = p_in[c - 1:c, :]
    s1 = s0 * p_last + _dot_tn(u, b_t * p_last, HI) + _dot_tn(v, k_t * p_last, HI)
    return y, s1


def _rwkv_scan_body(x_ref, lnw_ref, lnb_ref, rk_ref, o_ref, s_ref, *, heads, c):
    @pl.when(pl.program_id(2) == 0)
    def _():
        s_ref[...] = jnp.zeros_like(s_ref)

    ri = lax.broadcasted_iota(jnp.int32, (c, c), 0)
    ci = lax.broadcasted_iota(jnp.int32, (c, c), 1)
    strict = ci < ri
    incl = ci <= ri
    ltri = incl.astype(F32)
    outs = []
    for h in range(heads):
        sl = slice(h * HEAD_DIM, (h + 1) * HEAD_DIM)
        r, k, v, kk, alr, logw, g = [x_ref[q, :, sl] for q in range(7)]
        y, s1 = _rwkv_head_chunk(r, k, v, kk, alr, logw, s_ref[h], ltri, strict, incl)
        s_ref[h] = s1
        mean = jnp.mean(y, axis=-1, keepdims=True)
        yc = y - mean
        var = jnp.mean(yc * yc, axis=-1, keepdims=True)
        yn = yc * lax.rsqrt(var + RWKV_GN_EPS) * lnw_ref[:, sl] + lnb_ref[:, sl]
        bonus = jnp.sum(r * k * rk_ref[:, sl], axis=-1, keepdims=True) * v
        outs.append((yn + bonus) * g)
    o_ref[...] = jnp.concatenate(outs, axis=1).astype(o_ref.dtype)


def rwkv_scan(xs, batch, lp, p):
    _, tp, rw = xs.shape
    c = CHUNK
    nc = lp // c
    heads = LANES // HEAD_DIM
    gw = heads * HEAD_DIM
    row = lambda a: a.reshape(1, -1)
    vec = pl.BlockSpec((1, gw), lambda b, hg, ch: (0, hg))
    return pl.pallas_call(
        functools.partial(_rwkv_scan_body, heads=heads, c=c),
        grid=(batch, rw // gw, nc),
        in_specs=[pl.BlockSpec((7, c, gw), lambda b, hg, ch: (0, b * nc + ch, hg)), vec, vec, vec],
        out_specs=pl.BlockSpec((c, gw), lambda b, hg, ch: (b * nc + ch, hg)),
        out_shape=jax.ShapeDtypeStruct((tp, rw), BF16),
        scratch_shapes=[pltpu.VMEM((heads, HEAD_DIM, HEAD_DIM), F32)],
        compiler_params=_cparams(("parallel", "parallel", "arbitrary")),
        name="rwkv_scan",
    )(xs, row(p["lnx_w"]), row(p["lnx_b"]), row(p["r_k"]))


def _rope(x, cos, sa, sb):
    parts = []
    for s in range(x.shape[1] // LANES):
        sl = slice(s * LANES, (s + 1) * LANES)
        xs = x[:, sl]
        parts.append(xs * cos[:, sl] + pltpu.roll(xs, LANES - ROPE_DIMS // 2, 1) * sa[:, sl]
                     + pltpu.roll(xs, ROPE_DIMS // 2, 1) * sb[:, sl])
    return parts[0] if len(parts) == 1 else jnp.concatenate(parts, axis=1)


def _kv_rope_body(kv_ref, cos_ref, sa_ref, sb_ref, o_ref):
    o_ref[...] = _rope(kv_ref[...], cos_ref[...], sa_ref[...], sb_ref[...])


def swa_kv_rope(proj, kv_off, kvb, tabs, batch, lp):
    tp = proj.shape[0]
    nb = lp // SWA_BLOCK
    tab = pl.BlockSpec((SWA_BLOCK, kvb), lambda b, n: (n, 0))
    return pl.pallas_call(
        _kv_rope_body,
        grid=(batch, nb),
        in_specs=[pl.BlockSpec((SWA_BLOCK, kvb), lambda b, n: (b * nb + n, kv_off // kvb)), tab, tab, tab],
        out_specs=pl.BlockSpec((SWA_BLOCK, kvb), lambda b, n: (b * nb + n, 0)),
        out_shape=jax.ShapeDtypeStruct((tp, kvb), F32),
        compiler_params=_cparams(("parallel", "parallel")),
        name="swa_kv_rope",
    )(proj, *tabs)


def _swa_body(sink_ref, q_ref, kc_ref, kp_ref, km_ref, cos_ref, sa_ref, sb_ref, o_ref, *, group, kvw):
    n = pl.program_id(1)
    hk = pl.program_id(2)
    blk = SWA_BLOCK
    q = _rope(q_ref[...], cos_ref[...], sa_ref[...], sb_ref[...]) * (HEAD_DIM ** -0.5)
    qi = lax.broadcasted_iota(jnp.int32, (blk, blk), 0)
    kj = lax.broadcasted_iota(jnp.int32, (blk, blk), 1)
    far = 1 << 20
    cur_ok = kj + jnp.where(n >= 1, 0, far) <= qi
    prev_ok = kj > qi + jnp.where(n >= 2, 0, far)
    mi = lax.broadcasted_iota(jnp.int32, (blk, N_META), 0)
    mj = lax.broadcasted_iota(jnp.int32, (blk, N_META), 1)
    meta_ok = mj <= mi - PAD_FRONT + jnp.where(n >= 1, far, 0)
    outs = []
    for gi in range(group):
        qg = q[:, gi * HEAD_DIM:(gi + 1) * HEAD_DIM]
        s_c = jnp.where(cur_ok, _dot_nt(qg, kc_ref[:, 0:HEAD_DIM], HI), NEG_INF)
        s_p = jnp.where(prev_ok, _dot_nt(qg, kp_ref[:, 0:HEAD_DIM], HI), NEG_INF)
        s_m = jnp.where(meta_ok, _dot_nt(qg, km_ref[:, 0:HEAD_DIM], HI), NEG_INF)
        sink = sink_ref[hk * group + gi]
        mx = jnp.maximum(jnp.maximum(jnp.max(s_c, axis=-1, keepdims=True), jnp.max(s_p, axis=-1, keepdims=True)),
                         jnp.maximum(jnp.max(s_m, axis=-1, keepdims=True), sink))
        p_c = jnp.exp(s_c - mx)
        p_p = jnp.exp(s_p - mx)
        p_m = jnp.exp(s_m - mx)
        den = (jnp.sum(p_c, axis=-1, keepdims=True) + jnp.sum(p_p, axis=-1, keepdims=True)
               + jnp.sum(p_m, axis=-1, keepdims=True) + jnp.exp(sink - mx))
        acc = (_dot(p_c, kc_ref[:, HEAD_DIM:2 * HEAD_DIM], HI) + _dot(p_p, kp_ref[:, HEAD_DIM:2 * HEAD_DIM], HI)
               + _dot(p_m, km_ref[:, HEAD_DIM:2 * HEAD_DIM], HI))
        outs.append(acc / den)
    o_ref[...] = jnp.concatenate(outs, axis=1).astype(o_ref.dtype)


def swa_attention(proj, q_off, kvh, sinks, qtabs, batch, lp, n_q_heads, n_kv_heads):
    tp = proj.shape[0]
    nb = lp // SWA_BLOCK
    group = n_q_heads // n_kv_heads
    qb = group * HEAD_DIM
    sw = n_q_heads * HEAD_DIM
    tab = pl.BlockSpec((SWA_BLOCK, qb), lambda b, n, hk: (n, 0))
    kv_spec = lambda f: pl.BlockSpec((None, SWA_BLOCK, LANES), f)
    return pl.pallas_call(
        functools.partial(_swa_body, group=group, kvw=n_kv_heads * HEAD_DIM),
        grid=(batch, nb, n_kv_heads),
        in_specs=[
            pl.BlockSpec(memory_space=pltpu.SMEM),
            pl.BlockSpec((SWA_BLOCK, qb), lambda b, n, hk: (b * nb + n, q_off // qb + hk)),
            kv_spec(lambda b, n, hk: (hk, b * nb + n, 0)),
            kv_spec(lambda b, n, hk: (hk, b * nb + jnp.maximum(n - 1, 0), 0)),
            pl.BlockSpec((None, N_META, LANES),
                         lambda b, n, hk: (hk, b * (lp // N_META) + PAD_FRONT // N_META, 0)),
            tab, tab, tab,
        ],
        out_specs=pl.BlockSpec((SWA_BLOCK, qb), lambda b, n, hk: (b * nb + n, hk)),
        out_shape=jax.ShapeDtypeStruct((tp, sw), BF16),
        compiler_params=_cparams(("parallel", "parallel", "parallel")),
        name="swa_attention",
    )(sinks, proj, kvh, kvh, kvh, *qtabs)


def _hgrn_body(q_ref, f_ref, i_ref, g_ref, lb_ref, gain_ref, o_ref, st_ref, *, c):
    @pl.when(pl.program_id(2) == 0)
    def _():
        st_ref[...] = jnp.zeros_like(st_ref)

    q = q_ref[...]
    iv = i_ref[...]
    lb = lb_ref[...]
    f = lb + (1.0 - lb) * _sigmoid(f_ref[...])
    logf = jnp.log(f)
    kx = 1.0 - f
    ri = lax.broadcasted_iota(jnp.int32, (c, c), 0)
    ci = lax.broadcasted_iota(jnp.int32, (c, c), 1)
    incl = ci <= ri
    gc = _dot(incl.astype(F32), logf, HI)
    attn = jnp.zeros((c, c), F32)
    for s in range(c):
        dec = jnp.exp(jnp.minimum(gc - gc[s:s + 1, :], 0.0))
        col = jnp.sum(q * kx[s:s + 1, :] * dec, axis=-1, keepdims=True)
        attn = jnp.where(ci == s, col, attn)
    attn = jnp.where(incl, attn, 0.0)
    st = st_ref[...]
    o = _dot(attn, iv, HI) + _dot_nt(q * jnp.exp(gc), st, HI)
    g_last = gc[c - 1:c, :]
    st_ref[...] = st * jnp.exp(g_last) + _dot_tn(iv, kx * jnp.exp(g_last - gc), HI)
    on = o * lax.rsqrt(jnp.mean(o * o, axis=-1, keepdims=True) + NORM_EPS) * gain_ref[...]
    g = g_ref[...]
    o_ref[...] = (on * (g * _sigmoid(g))).astype(o_ref.dtype)


def hgrn_mix(proj, hg_off, lb, gain, batch, lp):
    tp = proj.shape[0]
    hw = lb.shape[0]
    nh = hw // HGRN_HEAD_DIM
    c = CHUNK
    nc = lp // c
    dk = HGRN_HEAD_DIM
    base = hg_off // dk
    blk = lambda part: pl.BlockSpec((c, dk), lambda b, h, ch: (b * nc + ch, base + part * nh + h))
    vec = pl.BlockSpec((1, dk), lambda b, h, ch: (0, h))
    return pl.pallas_call(
        functools.partial(_hgrn_body, c=c),
        grid=(batch, nh, nc),
        in_specs=[blk(0), blk(1), blk(2), blk(3), vec, vec],
        out_specs=pl.BlockSpec((c, dk), lambda b, h, ch: (b * nc + ch, h)),
        out_shape=jax.ShapeDtypeStruct((tp, hw), BF16),
        scratch_shapes=[pltpu.VMEM((dk, dk), F32)],
        compiler_params=_cparams(("parallel", "parallel", "arbitrary")),
        name="hgrn_mix",
    )(proj, proj, proj, proj, lb.reshape(1, hw), gain.reshape(1, hw))


def _out_proj_body(h_ref, ya_ref, yb_ref, yc_ref, wa_ref, wb_ref, wc_ref, o_ref):
    o_ref[...] = (h_ref[...] + _dot(ya_ref[...], wa_ref[...]) + _dot(yb_ref[...], wb_ref[...])
                  + _dot(yc_ref[...], wc_ref[...]))


def out_proj(h, ya, yb, yc, w_bf16):
    tp, d = h.shape
    wa, wb, wc = ya.shape[1], yb.shape[1], yc.shape[1]
    tm = _pick(tp, (512, 256, 128))
    tn = _pick(d, (512, 256, 128))
    return pl.pallas_call(
        _out_proj_body,
        grid=(tp // tm, d // tn),
        in_specs=[
            pl.BlockSpec((tm, tn), lambda i, j: (i, j)),
            pl.BlockSpec((tm, wa), lambda i, j: (i, 0)),
            pl.BlockSpec((tm, wb), lambda i, j: (i, 0)),
            pl.BlockSpec((tm, wc), lambda i, j: (i, 0)),
            pl.BlockSpec((wa, tn), lambda i, j: (0, j)),
            pl.BlockSpec((wb, tn), lambda i, j: (0, j)),
            pl.BlockSpec((wc, tn), lambda i, j: (0, j)),
        ],
        out_specs=pl.BlockSpec((tm, tn), lambda i, j: (i, j)),
        out_shape=jax.ShapeDtypeStruct((tp, d), F32),
        compiler_params=_cparams(("parallel", "parallel")),
        name="out_proj",
    )(h, ya, yb, yc, w_bf16[:wa], w_bf16[wa:wa + wb], w_bf16[wa + wb:])


def _norm_t_body(h_ref, g_ref, o_ref):
    x = h_ref[...]
    ms = jnp.mean(x * x, axis=-1, keepdims=True)
    y = x * lax.rsqrt(ms + NORM_EPS) * g_ref[...]
    o_ref[...] = y.T.astype(o_ref.dtype)


def norm_transposed(h, gain):
    tp, d = h.shape
    tm = _pick(tp, (512, 256, 128))
    return pl.pallas_call(
        _norm_t_body,
        grid=(tp // tm,),
        in_specs=[pl.BlockSpec((tm, d), lambda i: (i, 0)), pl.BlockSpec((1, d), lambda i: (0, 0))],
        out_specs=pl.BlockSpec((d, tm), lambda i: (0, i)),
        out_shape=jax.ShapeDtypeStruct((d, tp), BF16),
        compiler_params=_cparams(("parallel",)),
        name="peer_norm_t",
    )(h, gain.reshape(1, d))


def _top_values(s, k):
    rows = lax.broadcasted_iota(jnp.int32, (k, 1), 0)
    out = jnp.zeros((k, s.shape[1]), F32)
    for r in range(k):
        m = jnp.max(s, axis=0, keepdims=True)
        out = jnp.where(rows == r, m, out)
        s = jnp.where(s == m, -jnp.inf, s)
    return out


def _peer_route_body(wq_ref, zn_ref, keys_ref, s1_ref, s2_ref, e1_ref, e2_ref, thr_ref, *, half, topk):
    qt = _dot(wq_ref[...], zn_ref[...])
    s1 = _dot(keys_ref[0], qt[:half], HI)
    s2 = _dot(keys_ref[1], qt[half:], HI)
    v1 = _top_values(s1, topk)
    v2 = _top_values(s2, topk)
    cand = jnp.concatenate([v1[i:i + 1, :] + v2 for i in range(topk)], axis=0)
    best = _top_values(cand, topk)
    mx = best[0:1, :]
    z = jnp.sum(jnp.exp(best - mx), axis=0, keepdims=True)
    s1_ref[...] = s1
    s2_ref[...] = s2
    e1_ref[...] = jnp.exp(s1 - v1[0:1, :]) / z
    e2_ref[...] = jnp.exp(s2 - v2[0:1, :])
    thr_ref[...] = best[topk - 1:topk, :]


def peer_route(znt, wqt_bf16, keys):
    d, tp = znt.shape
    nh, _, nk, half = keys.shape
    tm = _pick(tp, (512, 256, 128))
    sc = pl.BlockSpec((None, nk, tm), lambda i, h: (h, 0, i))
    big = jax.ShapeDtypeStruct((nh, nk, tp), F32)
    return pl.pallas_call(
        functools.partial(_peer_route_body, half=half, topk=PEER_TOPK),
        grid=(tp // tm, nh),
        in_specs=[
            pl.BlockSpec((2 * half, d), lambda i, h: (h, 0)),
            pl.BlockSpec((d, tm), lambda i, h: (0, i)),
            pl.BlockSpec((None, 2, nk, half), lambda i, h: (h, 0, 0, 0)),
        ],
        out_specs=[sc, sc, sc, sc, pl.BlockSpec((None, 1, tm), lambda i, h: (h, 0, i))],
        out_shape=[big, big, big, big, jax.ShapeDtypeStruct((nh, 1, tp), F32)],
        compiler_params=_cparams(("parallel", "arbitrary")),
        name="peer_route",
    )(wqt_bf16, znt, keys)


def _gelu(x):
    return 0.5 * x * (1.0 + lax.erf(x * (2.0 ** -0.5)))


def _peer_expert_body(zn_ref, u_ref, vt_ref, s1_ref, s2_ref, e1_ref, e2_ref, thr_ref, o_ref, a_ref, *, nk, nh):
    j = pl.program_id(1)

    @pl.when(j == 0)
    def _():
        o_ref[...] = jnp.zeros_like(o_ref)

    te = u_ref.shape[0]
    act = _gelu(_dot(u_ref[...], zn_ref[...]))
    for al in range(te // nk):
        a = j * (te // nk) + al
        gate = None
        for h in range(nh):
            cand = s1_ref[h, pl.ds(a, 1), :] + s2_ref[h]
            gh = jnp.where(cand >= thr_ref[h], e1_ref[h, pl.ds(a, 1), :] * e2_ref[h], 0.0)
            gate = gh if gate is None else gate + gh
        a_ref[al * nk:(al + 1) * nk, :] = (gate * act[al * nk:(al + 1) * nk, :]).astype(BF16)
    o_ref[...] += _dot(vt_ref[...], a_ref[...])


def peer_experts(znt, u_bf16, vt_bf16, s1, s2, e1, e2, thr):
    d, tp = znt.shape
    ne = u_bf16.shape[0]
    nh, nk, _ = s1.shape
    tm = _pick(tp, (512, 256, 128))
    te = _pick(ne, (512, 256, 128))
    once = dict(pipeline_mode=pl.Buffered(1))
    aux = pl.BlockSpec((nh, nk, tm), lambda i, j: (0, 0, i), **once)
    return pl.pallas_call(
        functools.partial(_peer_expert_body, nk=nk, nh=nh),
        grid=(tp // tm, ne // te),
        in_specs=[
            pl.BlockSpec((d, tm), lambda i, j: (0, i), **once),
            pl.BlockSpec((te, d), lambda i, j: (j, 0)),
            pl.BlockSpec((d, te), lambda i, j: (0, j)),
            aux, aux, aux, aux,
            pl.BlockSpec((nh, 1, tm), lambda i, j: (0, 0, i), **once),
        ],
        out_specs=pl.BlockSpec((d, tm), lambda i, j: (0, i)),
        out_shape=jax.ShapeDtypeStruct((d, tp), F32),
        scratch_shapes=[pltpu.VMEM((te, tm), BF16)],
        compiler_params=_cparams(("parallel", "arbitrary")),
        name="peer_experts",
    )(znt, u_bf16, vt_bf16, s1, s2, e1, e2, thr)


def _residual_t_body(h_ref, yt_ref, o_ref):
    o_ref[...] = h_ref[...] + yt_ref[...].T


def residual_add_transposed(h, yt):
    tp, d = h.shape
    tm = _pick(tp, (512, 256, 128))
    return pl.pallas_call(
        _residual_t_body,
        grid=(tp // tm,),
        in_specs=[pl.BlockSpec((tm, d), lambda i: (i, 0)), pl.BlockSpec((d, tm), lambda i: (0, i))],
        out_specs=pl.BlockSpec((tm, d), lambda i: (i, 0)),
        out_shape=jax.ShapeDtypeStruct((tp, d), F32),
        compiler_params=_cparams(("parallel",)),
        name="peer_residual",
    )(h, yt)


def _final_norm_body(h_ref, g_ref, o_ref):
    x = h_ref[...]
    ms = jnp.mean(x * x, axis=-1, keepdims=True)
    o_ref[...] = x * lax.rsqrt(ms + NORM_EPS) * g_ref[...]


def final_norm(h, gain, batch, lp):
    tp, d = h.shape
    nb = lp // SWA_BLOCK
    seq = lp - SWA_BLOCK
    out = pl.pallas_call(
        _final_norm_body,
        grid=(batch, nb - 1),
        in_specs=[pl.BlockSpec((SWA_BLOCK, d), lambda b, n: (b * nb + n + 1, 0)),
                  pl.BlockSpec((1, d), lambda b, n: (0, 0))],
        out_specs=pl.BlockSpec((SWA_BLOCK, d), lambda b, n: (b * (nb - 1) + n, 0)),
        out_shape=jax.ShapeDtypeStruct((batch * seq, d), F32),
        compiler_params=_cparams(("parallel", "parallel")),
        name="final_norm",
    )(h, gain.reshape(1, d))
    return out.reshape(batch, seq, d)


def _rope_tables(lp, width, rot_heads):
    pos = jnp.maximum(jnp.arange(lp, dtype=F32) - PAD_FRONT, 0.0)
    inv_freq = ROPE_THETA ** (-jnp.arange(0, ROPE_DIMS, 2, dtype=F32) / ROPE_DIMS)
    ang = pos[:, None] * inv_freq[None, :]
    cos, sin = jnp.cos(ang), jnp.sin(ang)
    half = ROPE_DIMS // 2
    rest = HEAD_DIM - ROPE_DIMS
    one = jnp.ones((lp, rest), F32)
    zero = jnp.zeros((lp, rest), F32)
    zh = jnp.zeros((lp, half), F32)
    c_head = jnp.concatenate([cos, cos, one], axis=1)
    a_head = jnp.concatenate([-sin, zh, zero], axis=1)
    b_head = jnp.concatenate([zh, sin, zero], axis=1)
    n_heads = width // HEAD_DIM
    idn = jnp.ones((lp, HEAD_DIM), F32)
    zer = jnp.zeros((lp, HEAD_DIM), F32)
    build = lambda rot, flat: jnp.concatenate([rot if i < rot_heads else flat for i in range(n_heads)], axis=1)
    return build(c_head, idn), build(a_head, zer), build(b_head, zer)


def _layout(d, rc, sw, kvw, hw):
    qb = sw * HEAD_DIM // kvw
    kvb = _roundup(2 * kvw, LANES)
    q_off = _roundup(rc, qb)
    kv_off = _roundup(q_off + sw, kvb)
    hg_off = _roundup(kv_off + kvb, LANES)
    total = _roundup(hg_off + 4 * hw, 512)
    return q_off, kv_off, kvb, hg_off, total


def kernel(x, meta_tokens, norm_mix, norm_ffn, norm_final, w_in, w_out, rwkv_mu, rwkv_w0, rwkv_w2, rwkv_a0,
           rwkv_a2, rwkv_g2, rwkv_k_k, rwkv_k_a, rwkv_r_k, rwkv_lnx_w, rwkv_lnx_b, swa_sinks, hgrn_lb, hgrn_norm,
           peer_wq, peer_keys, peer_u, peer_v):
    batch, seq, d = x.shape
    depth = w_in.shape[0]
    lp = seq + SWA_BLOCK
    rw = rwkv_w0.shape[1]
    rc = rwkv_mu.shape[1]
    n_q = swa_sinks.shape[1]
    sw = n_q * HEAD_DIM
    hw = hgrn_lb.shape[1]
    kvw = (w_in.shape[2] - rc - sw - 4 * hw) // 2
    n_kv = kvw // HEAD_DIM
    q_off, kv_off, kvb, hg_off, n_tot = _layout(d, rc, sw, kvw, hw)

    pad = jnp.zeros((batch, PAD_FRONT, d), x.dtype)
    meta = jnp.broadcast_to(meta_tokens.astype(x.dtype)[None], (batch, N_META, d))
    h = jnp.concatenate([pad, meta, x], axis=1).reshape(batch * lp, d)

    q_tabs = _rope_tables(lp, sw // n_kv, sw // n_kv // HEAD_DIM)
    kv_tabs = _rope_tables(lp, kvb, n_kv)
    lb_all = jnp.cumsum(jax.nn.softmax(hgrn_lb.astype(F32), axis=0), axis=0)
    lb_all = lb_all - lb_all[0:1]

    for l in range(depth):
        w = w_in[l]
        zc = lambda n: jnp.zeros((d, n), w.dtype)
        w_re = jnp.concatenate([
            w[:, :rc], zc(q_off - rc),
            w[:, rc:rc + sw], zc(kv_off - q_off - sw),
            w[:, rc + sw:rc + sw + 2 * kvw], zc(hg_off - kv_off - 2 * kvw),
            w[:, rc + sw + 2 * kvw:], zc(n_tot - hg_off - 4 * hw)], axis=1).astype(BF16)
        proj = norm_matmul(h, norm_mix[l], w_re, lp)

        rp = dict(mu=rwkv_mu[l], w0=rwkv_w0[l], w2=rwkv_w2[l], a0=rwkv_a0[l], a2=rwkv_a2[l], g2=rwkv_g2[l],
                  k_k=rwkv_k_k[l], k_a=rwkv_k_a[l], r_k=rwkv_r_k[l], lnx_w=rwkv_lnx_w[l], lnx_b=rwkv_lnx_b[l])
        y_a = rwkv_scan(rwkv_prep(proj, lp, rp), batch, lp, rp)

        kv = swa_kv_rope(proj, kv_off, kvb, kv_tabs, batch, lp)
        kvh = jnp.stack([jnp.concatenate([kv[:, i * HEAD_DIM:(i + 1) * HEAD_DIM],
                                          kv[:, kvw + i * HEAD_DIM:kvw + (i + 1) * HEAD_DIM]], axis=1)
                         for i in range(n_kv)])
        y_b = swa_attention(proj, q_off, kvh, swa_sinks[l], q_tabs, batch, lp, n_q, n_kv)

        y_c = hgrn_mix(proj, hg_off, lb_all[l], hgrn_norm[l], batch, lp)

        h = out_proj(h, y_a, y_b, y_c, w_out[l].astype(BF16))

        znt = norm_transposed(h, norm_ffn[l])
        s1, s2, e1, e2, thr = peer_route(znt, peer_wq[l].T.astype(BF16), peer_keys[l])
        yt = peer_experts(znt, peer_u[l].astype(BF16), peer_v[l].T.astype(BF16), s1, s2, e1, e2, thr)
        h = residual_add_transposed(h, yt)

    return final_norm(h, norm_final, batch, lp)
```

```python
import functools

import jax
import jax.numpy as jnp
from jax import lax
from jax.experimental import pallas as pl
from jax.experimental.pallas import tpu as pltpu

F32 = jnp.float32
BF16 = jnp.bfloat16
HI = lax.Precision.HIGHEST

N_META = 16
HEAD_DIM = 64
NORM_EPS = 1e-5
NEG_INF = -1e30
RWKV_GN_EPS = 64e-5
SWA_BLOCK = 128
PAD_FRONT = SWA_BLOCK - N_META
ROPE_THETA = 500000.0
ROPE_DIMS = HEAD_DIM // 4
HGRN_HEAD_DIM = 128
CHUNK = 64
PEER_TOPK = 16
LANES = 128
VMEM_LIMIT_MB = 56
RWKV_HEADS_PER_STEP = 24
SWA_MODE = "bf16"
HGRN_HEADS_PER_STEP = 4
HGRN_SUB = 16
RWKV_SCAN_MODE = "bf16"


def _cparams(sem, vmem_mb=VMEM_LIMIT_MB):
    return pltpu.CompilerParams(dimension_semantics=sem, vmem_limit_bytes=vmem_mb * 1024 * 1024)


def _pick(n, cands):
    for c in cands:
        if n % c == 0:
            return c
    raise ValueError(f"no tile in {cands} divides {n}")


def _roundup(n, m):
    return -(-n // m) * m


def _dot(a, b, prec=None):
    return jnp.dot(a, b, preferred_element_type=F32, precision=prec)


def _dot_nt(a, b, prec=None):
    return lax.dot_general(a, b, (((1,), (1,)), ((), ())), preferred_element_type=F32, precision=prec)


def _dot_tn(a, b, prec=None):
    return lax.dot_general(a, b, (((0,), (0,)), ((), ())), preferred_element_type=F32, precision=prec)


def _row_in_batch(i, tm, lp):
    pos0 = lax.rem(i * tm, lp)
    rb = pos0 + lax.broadcasted_iota(jnp.int32, (tm, 1), 0)
    return jnp.where(rb >= lp, rb - lp, rb)


def _sigmoid(x):
    return 1.0 / (1.0 + jnp.exp(-x))


def _norm_mm_body(x_ref, g_ref, w_ref, o_ref, z_ref, *, lp, tm):
    i = pl.program_id(0)

    @pl.when(pl.program_id(1) == 0)
    def _():
        x = x_ref[...]
        ms = jnp.mean(x * x, axis=-1, keepdims=True)
        y = x * lax.rsqrt(ms + NORM_EPS) * g_ref[...]
        z_ref[...] = jnp.where(_row_in_batch(i, tm, lp) >= PAD_FRONT, y, 0.0).astype(BF16)

    o_ref[...] = _dot(z_ref[...], w_ref[...])


def norm_matmul(h, gain, w_bf16, lp):
    tp, d = h.shape
    n = w_bf16.shape[1]
    tm = _pick(tp, (512, 256, 128))
    tn = _pick(n, (512, 256, 128))
    return pl.pallas_call(
        functools.partial(_norm_mm_body, lp=lp, tm=tm),
        grid=(tp // tm, n // tn),
        in_specs=[
            pl.BlockSpec((tm, d), lambda i, j: (i, 0)),
            pl.BlockSpec((1, d), lambda i, j: (0, 0)),
            pl.BlockSpec((d, tn), lambda i, j: (0, j)),
        ],
        out_specs=pl.BlockSpec((tm, tn), lambda i, j: (i, j)),
        out_shape=jax.ShapeDtypeStruct((tp, n), F32),
        scratch_shapes=[pltpu.VMEM((tm, d), BF16)],
        compiler_params=_cparams(("parallel", "arbitrary")),
        name="norm_in_proj",
    )(h, gain.reshape(1, d), w_bf16)


def _mm(a, b, dims, mode):
    if mode == "bf16":
        return lax.dot_general(a.astype(BF16), b.astype(BF16), (dims, ((), ())), preferred_element_type=F32)
    return lax.dot_general(a, b, (dims, ((), ())), preferred_element_type=F32, precision=HI)


_NN = ((1,), (0,))
_NT = ((1,), (1,))
_TN = ((0,), (0,))


def _head_sums(x):
    shift = HEAD_DIM.bit_length() - 1
    er = lax.broadcasted_iota(jnp.int32, (LANES, LANES), 0) >> shift
    ec = lax.broadcasted_iota(jnp.int32, (LANES, LANES), 1) >> shift
    ones_bd = jnp.where(er == ec, 1.0, 0.0)
    return jnp.concatenate(
        [_dot(x[:, s * LANES:(s + 1) * LANES], ones_bd, HI) for s in range(x.shape[1] // LANES)], axis=1)


def _rwkv_prep_body(pa_ref, prev_ref, mu_ref, w0_ref, w2_ref, a0_ref, a2_ref, g2_ref, kk_ref, ka_ref, rk_ref,
                    lnw_ref, lnb_ref, out_ref, dl_ref, *, lp, tm, rw, rd, ri, rg, c):
    i = pl.program_id(0)
    x = pa_ref[...]
    rowi = lax.broadcasted_iota(jnp.int32, (tm, 1), 0)
    prev_row = jnp.where(lax.rem(i * tm, lp) == 0, 0.0, prev_ref[7:8, :])
    prev = jnp.where(rowi == 0, prev_row, pltpu.roll(x, 1, 0))
    xs = x + (prev - x) * mu_ref[...]
    r = xs[:, 0:rw]
    k = xs[:, rw:2 * rw]
    v = xs[:, 2 * rw:3 * rw]
    wl = xs[:, 3 * rw:3 * rw + rd]
    al = xs[:, 3 * rw + rd:3 * rw + rd + ri]
    gl = xs[:, 3 * rw + rd + ri:3 * rw + rd + ri + rg]
    wv = w0_ref[...] + _dot(jnp.tanh(wl), w2_ref[...], HI)
    nwv = -wv
    softplus = jnp.maximum(nwv, 0.0) + jnp.log(1.0 + jnp.exp(-jnp.abs(nwv)))
    logw = -jnp.exp(-softplus - 0.5)
    alr = _sigmoid(a0_ref[...] + _dot(al, a2_ref[...], HI))
    g = _dot(_sigmoid(gl), g2_ref[...], HI)
    kkr = k * kk_ref[...]
    kk = kkr / jnp.maximum(jnp.sqrt(_head_sums(kkr * kkr)), 1e-12)
    k2 = k * (1.0 + (alr - 1.0) * ka_ref[...])
    shift = c.bit_length() - 1
    tr = lax.broadcasted_iota(jnp.int32, (tm, tm), 0)
    tc = lax.broadcasted_iota(jnp.int32, (tm, tm), 1)
    lall = jnp.where((tr >> shift) == (tc >> shift), 1.0, 0.0)
    ltri = jnp.where(tc <= tr, lall, 0.0)
    cum = _dot(ltri, logw, HI)
    tot = _dot(lall, logw, HI)
    p_inv = jnp.exp(-cum)
    p_rem = jnp.exp(tot - cum)
    kb = kk * alr
    out_ref[0] = -kk * jnp.exp(cum - logw)
    out_ref[1] = r * jnp.exp(cum)
    out_ref[2] = kb * p_inv
    out_ref[3] = k2 * p_inv
    out_ref[4] = v
    out_ref[5] = kb * p_rem
    out_ref[6] = k2 * p_rem
    out_ref[7] = lnw_ref[...] * g
    out_ref[8] = (lnb_ref[...] + _head_sums(r * k2 * rk_ref[...]) * v) * g
    for q in range(tm // c):
        dl_ref[q] = jnp.exp(tot[q * c:q * c + 1, :])


def rwkv_prep(proj, lp, p):
    tp = proj.shape[0]
    rw = p["w0"].shape[0]
    rd, ri, rg = p["w2"].shape[0], p["a2"].shape[0], p["g2"].shape[0]
    rc = 3 * rw + rd + ri + rg
    tm = 128
    c = CHUNK
    row = lambda a: a.reshape(1, -1)
    full = lambda a: pl.BlockSpec(a.shape, lambda i: (0, 0))
    args = [row(p["mu"]), row(p["w0"]), p["w2"], row(p["a0"]), p["a2"], p["g2"], row(p["k_k"]), row(p["k_a"]),
            row(p["r_k"]), row(p["lnx_w"]), row(p["lnx_b"])]
    return pl.pallas_call(
        functools.partial(_rwkv_prep_body, lp=lp, tm=tm, rw=rw, rd=rd, ri=ri, rg=rg, c=c),
        grid=(tp // tm,),
        in_specs=[
            pl.BlockSpec((tm, rc), lambda i: (i, 0)),
            pl.BlockSpec((8, rc), lambda i: (jnp.maximum(i * (tm // 8) - 1, 0), 0)),
        ] + [full(a) for a in args],
        out_specs=[pl.BlockSpec((9, tm, rw), lambda i: (0, i, 0)),
                   pl.BlockSpec((tm // c, 1, rw), lambda i: (i, 0, 0))],
        out_shape=[jax.ShapeDtypeStruct((9, tp, rw), F32), jax.ShapeDtypeStruct((tp // c, 1, rw), F32)],
        compiler_params=_cparams(("parallel",)),
        name="rwkv_prep",
    )(proj, proj, *args)


def _rwkv_scan_body(x_ref, dl_ref, o_ref, s_ref, *, heads, c, mode):
    @pl.when(pl.program_id(2) == 0)
    def _():
        s_ref[...] = jnp.zeros_like(s_ref)

    ri = lax.broadcasted_iota(jnp.int32, (c, c), 0)
    ci = lax.broadcasted_iota(jnp.int32, (c, c), 1)
    strict = ci < ri
    incl = ci <= ri
    hs = range(heads)
    ld = lambda q, h: x_ref[q, :, h * HEAD_DIM:(h + 1) * HEAD_DIM]
    cat0 = lambda a, b: jnp.concatenate([a, b], axis=0)
    cat1 = lambda a, b: jnp.concatenate([a, b], axis=1)
    ar = [cat0(ld(0, h), ld(1, h)) for h in hs]
    m = [_mm(ar[h], cat0(ld(2, h), ld(3, h)), _NT, mode) for h in hs]
    s0 = [s_ref[h] for h in hs]
    ars = [_mm(ar[h], s0[h], _NT, mode) for h in hs]
    v = [ld(4, h) for h in hs]
    nk = [jnp.where(strict, m[h][:c, :c], 0.0) for h in hs]
    m_ak = [jnp.where(strict, m[h][:c, c:], 0.0) for h in hs]
    m_r = [cat1(jnp.where(incl, m[h][c:, :c], 0.0), jnp.where(incl, m[h][c:, c:], 0.0)) for h in hs]
    u = [ars[h][:c] + _mm(m_ak[h], v[h], _NN, mode) for h in hs]
    steps = c.bit_length() - 1
    for it in range(steps):
        if it + 1 < steps:
            both = [_mm(nk[h], cat1(u[h], nk[h]), _NN, mode) for h in hs]
            u = [u[h] + both[h][:, :c] for h in hs]
            nk = [both[h][:, c:] for h in hs]
        else:
            u = [u[h] + _mm(nk[h], u[h], _NN, mode) for h in hs]
    uv = [cat0(u[h], v[h]) for h in hs]
    y = [ars[h][c:] + _mm(m_r[h], uv[h], _NN, mode) for h in hs]
    for h in hs:
        dl = dl_ref[0, :, h * HEAD_DIM:(h + 1) * HEAD_DIM]
        s_ref[h] = s0[h] * dl + _mm(uv[h], cat0(ld(5, h), ld(6, h)), _TN, mode)
    outs = []
    for h in hs:
        mean = jnp.mean(y[h], axis=-1, keepdims=True)
        yc = y[h] - mean
        var = jnp.mean(yc * yc, axis=-1, keepdims=True)
        outs.append(yc * lax.rsqrt(var + RWKV_GN_EPS) * ld(7, h) + ld(8, h))
    o_ref[...] = jnp.concatenate(outs, axis=1).astype(o_ref.dtype)


def rwkv_scan(xs, dl, batch, lp):
    _, tp, rw = xs.shape
    c = CHUNK
    nc = lp // c
    heads = _pick(rw // HEAD_DIM, (RWKV_HEADS_PER_STEP, 2))
    gw = heads * HEAD_DIM
    return pl.pallas_call(
        functools.partial(_rwkv_scan_body, heads=heads, c=c, mode=RWKV_SCAN_MODE),
        grid=(batch, rw // gw, nc),
        in_specs=[pl.BlockSpec((9, c, gw), lambda b, hg, ch: (0, b * nc + ch, hg)),
                  pl.BlockSpec((1, 1, gw), lambda b, hg, ch: (b * nc + ch, 0, hg))],
        out_specs=pl.BlockSpec((c, gw), lambda b, hg, ch: (b * nc + ch, hg)),
        out_shape=jax.ShapeDtypeStruct((tp, rw), BF16),
        scratch_shapes=[pltpu.VMEM((heads, HEAD_DIM, HEAD_DIM), F32)],
        compiler_params=_cparams(("parallel", "parallel", "arbitrary")),
        name="rwkv_scan",
    )(xs, dl)


def _rope(x, cos, sa, sb):
    parts = []
    for s in range(x.shape[1] // LANES):
        sl = slice(s * LANES, (s + 1) * LANES)
        xs = x[:, sl]
        parts.append(xs * cos[:, sl] + pltpu.roll(xs, LANES - ROPE_DIMS // 2, 1) * sa[:, sl]
                     + pltpu.roll(xs, ROPE_DIMS // 2, 1) * sb[:, sl])
    return parts[0] if len(parts) == 1 else jnp.concatenate(parts, axis=1)


def _kv_rope_body(kv_ref, cos_ref, sa_ref, sb_ref, o_ref):
    o_ref[...] = _rope(kv_ref[...], cos_ref[...], sa_ref[...], sb_ref[...])


def swa_kv_rope(proj, kv_off, kvb, tabs, batch, lp):
    tp = proj.shape[0]
    nb = lp // SWA_BLOCK
    tab = pl.BlockSpec((SWA_BLOCK, kvb), lambda b, n: (n, 0))
    return pl.pallas_call(
        _kv_rope_body,
        grid=(batch, nb),
        in_specs=[pl.BlockSpec((SWA_BLOCK, kvb), lambda b, n: (b * nb + n, kv_off // kvb)), tab, tab, tab],
        out_specs=pl.BlockSpec((SWA_BLOCK, kvb), lambda b, n: (b * nb + n, 0)),
        out_shape=jax.ShapeDtypeStruct((tp, kvb), F32),
        compiler_params=_cparams(("parallel", "parallel")),
        name="swa_kv_rope",
    )(proj, *tabs)


def _swa_body(sink_ref, q_ref, kc_ref, kp_ref, km_ref, cos_ref, sa_ref, sb_ref, o_ref, *, group, kvw):
    n = pl.program_id(1)
    hk = pl.program_id(2)
    blk = SWA_BLOCK
    mode = SWA_MODE
    q = _rope(q_ref[...], cos_ref[...], sa_ref[...], sb_ref[...]) * (HEAD_DIM ** -0.5)
    qs = jnp.concatenate([q[:, gi * HEAD_DIM:(gi + 1) * HEAD_DIM] for gi in range(group)], axis=0)
    rows = group * blk
    qi = lax.broadcasted_iota(jnp.int32, (rows, blk), 0) & (blk - 1)
    kj = lax.broadcasted_iota(jnp.int32, (rows, blk), 1)
    far = 1 << 20
    cur_ok = kj + jnp.where(n >= 1, 0, far) <= qi
    prev_ok = kj > qi + jnp.where(n >= 2, 0, far)
    mi = lax.broadcasted_iota(jnp.int32, (rows, N_META), 0) & (blk - 1)
    mj = lax.broadcasted_iota(jnp.int32, (rows, N_META), 1)
    meta_ok = mj <= mi - PAD_FRONT + jnp.where(n >= 1, far, 0)
    sink = jnp.concatenate([jnp.full((blk, 1), sink_ref[hk * group + gi], F32) for gi in range(group)], axis=0)
    s_c = jnp.where(cur_ok, _mm(qs, kc_ref[:, 0:HEAD_DIM], _NT, mode), NEG_INF)
    s_p = jnp.where(prev_ok, _mm(qs, kp_ref[:, 0:HEAD_DIM], _NT, mode), NEG_INF)
    s_m = jnp.where(meta_ok, _mm(qs, km_ref[:, 0:HEAD_DIM], _NT, mode), NEG_INF)
    mx = jnp.maximum(jnp.maximum(jnp.max(s_c, axis=-1, keepdims=True), jnp.max(s_p, axis=-1, keepdims=True)),
                     jnp.maximum(jnp.max(s_m, axis=-1, keepdims=True), sink))
    p_c = jnp.exp(s_c - mx)
    p_p = jnp.exp(s_p - mx)
    p_m = jnp.exp(s_m - mx)
    den = (jnp.sum(p_c, axis=-1, keepdims=True) + jnp.sum(p_p, axis=-1, keepdims=True)
           + jnp.sum(p_m, axis=-1, keepdims=True) + jnp.exp(sink - mx))
    acc = (_mm(p_c, kc_ref[:, HEAD_DIM:2 * HEAD_DIM], _NN, mode) + _mm(p_p, kp_ref[:, HEAD_DIM:2 * HEAD_DIM], _NN, mode)
           + _mm(p_m, km_ref[:, HEAD_DIM:2 * HEAD_DIM], _NN, mode))
    out = acc / den
    o_ref[...] = jnp.concatenate([out[gi * blk:(gi + 1) * blk, :] for gi in range(group)], axis=1).astype(o_ref.dtype)


def swa_attention(proj, q_off, kvh, sinks, qtabs, batch, lp, n_q_heads, n_kv_heads):
    tp = proj.shape[0]
    nb = lp // SWA_BLOCK
    group = n_q_heads // n_kv_heads
    qb = group * HEAD_DIM
    sw = n_q_heads * HEAD_DIM
    tab = pl.BlockSpec((SWA_BLOCK, qb), lambda b, n, hk: (n, 0))
    kv_spec = lambda f: pl.BlockSpec((None, SWA_BLOCK, LANES), f)
    return pl.pallas_call(
        functools.partial(_swa_body, group=group, kvw=n_kv_heads * HEAD_DIM),
        grid=(batch, nb, n_kv_heads),
        in_specs=[
            pl.BlockSpec(memory_space=pltpu.SMEM),
            pl.BlockSpec((SWA_BLOCK, qb), lambda b, n, hk: (b * nb + n, q_off // qb + hk)),
            kv_spec(lambda b, n, hk: (hk, b * nb + n, 0)),
            kv_spec(lambda b, n, hk: (hk, b * nb + jnp.maximum(n - 1, 0), 0)),
            pl.BlockSpec((None, N_META, LANES),
                         lambda b, n, hk: (hk, b * (lp // N_META) + PAD_FRONT // N_META, 0)),
            tab, tab, tab,
        ],
        out_specs=pl.BlockSpec((SWA_BLOCK, qb), lambda b, n, hk: (b * nb + n, hk)),
        out_shape=jax.ShapeDtypeStruct((tp, sw), BF16),
        compiler_params=_cparams(("parallel", "parallel", "parallel")),
        name="swa_attention",
    )(sinks, proj, kvh, kvh, kvh, *qtabs)


def _hgrn_body(q_ref, f_ref, i_ref, g_ref, lb_ref, gain_ref, o_ref, st_ref, *, c, heads, sub):
    @pl.when(pl.program_id(2) == 0)
    def _():
        st_ref[...] = jnp.zeros_like(st_ref)

    dk = HGRN_HEAD_DIM
    hs = range(heads)
    nb = c // sub
    ri = lax.broadcasted_iota(jnp.int32, (c, c), 0)
    ci = lax.broadcasted_iota(jnp.int32, (c, c), 1)
    incl = ci <= ri
    ltri = jnp.where(incl, 1.0, 0.0)
    cs = lax.broadcasted_iota(jnp.int32, (sub, c), 1)
    col = lambda ref, h: ref[:, h * dk:(h + 1) * dk]
    q = [col(q_ref, h) for h in hs]
    iv = [col(i_ref, h) for h in hs]
    f = [col(lb_ref, h) + (1.0 - col(lb_ref, h)) * _sigmoid(col(f_ref, h)) for h in hs]
    kx = [1.0 - f[h] for h in hs]
    gc = [_dot(ltri, jnp.log(f[h]), HI) for h in hs]
    st = [st_ref[h] for h in hs]
    o_inter = [_dot_nt(q[h] * jnp.exp(gc[h]), st[h], HI) for h in hs]
    for h in hs:
        g_last = gc[h][c - 1:c, :]
        st_ref[h] = st[h] * jnp.exp(g_last) + _dot_tn(iv[h], kx[h] * jnp.exp(g_last - gc[h]), HI)
    attn = []
    for h in hs:
        blocks = []
        for b in range(nb):
            rows = slice(b * sub, (b + 1) * sub)
            qb, gb = q[h][rows, :], gc[h][rows, :]
            if b == 0:
                blk = jnp.zeros((sub, c), F32)
            else:
                ref = gc[h][b * sub:b * sub + 1, :]
                k_sc = kx[h] * jnp.exp(jnp.minimum(ref - gc[h], 0.0))
                blk = _dot_nt(qb * jnp.exp(gb - ref), k_sc, HI)
            for s in range(b * sub, (b + 1) * sub):
                dec = jnp.exp(jnp.minimum(gb - gc[h][s:s + 1, :], 0.0))
                blk = jnp.where(cs == s, jnp.sum(qb * kx[h][s:s + 1, :] * dec, axis=-1, keepdims=True), blk)
            blocks.append(blk)
        attn.append(jnp.where(incl, jnp.concatenate(blocks, axis=0), 0.0))
    outs = []
    for h in hs:
        o = _dot(attn[h], iv[h], HI) + o_inter[h]
        on = o * lax.rsqrt(jnp.mean(o * o, axis=-1, keepdims=True) + NORM_EPS) * col(gain_ref, h)
        g = col(g_ref, h)
        outs.append(on * (g * _sigmoid(g)))
    o_ref[...] = jnp.concatenate(outs, axis=1).astype(o_ref.dtype)


def hgrn_mix(proj, hg_off, lb, gain, batch, lp):
    tp = proj.shape[0]
    hw = lb.shape[0]
    dk = HGRN_HEAD_DIM
    nh = hw // dk
    c = CHUNK
    nc = lp // c
    heads = _pick(nh, (HGRN_HEADS_PER_STEP, 2, 1))
    gw = heads * dk
    blk = lambda part: pl.BlockSpec((c, gw), lambda b, h, ch: (b * nc + ch, (hg_off + part * hw) // gw + h))
    vec = pl.BlockSpec((1, gw), lambda b, h, ch: (0, h))
    return pl.pallas_call(
        functools.partial(_hgrn_body, c=c, heads=heads, sub=HGRN_SUB),
        grid=(batch, nh // heads, nc),
        in_specs=[blk(0), blk(1), blk(2), blk(3), vec, vec],
        out_specs=pl.BlockSpec((c, gw), lambda b, h, ch: (b * nc + ch, h)),
        out_shape=jax.ShapeDtypeStruct((tp, hw), BF16),
        scratch_shapes=[pltpu.VMEM((heads, dk, dk), F32)],
        compiler_params=_cparams(("parallel", "parallel", "arbitrary")),
        name="hgrn_mix",
    )(proj, proj, proj, proj, lb.reshape(1, hw), gain.reshape(1, hw))


def _out_proj_body(h_ref, ya_ref, yb_ref, yc_ref, wa_ref, wb_ref, wc_ref, o_ref):
    o_ref[...] = (h_ref[...] + _dot(ya_ref[...], wa_ref[...]) + _dot(yb_ref[...], wb_ref[...])
                  + _dot(yc_ref[...], wc_ref[...]))


def out_proj(h, ya, yb, yc, w_bf16):
    tp, d = h.shape
    wa, wb, wc = ya.shape[1], yb.shape[1], yc.shape[1]
    tm = _pick(tp, (512, 256, 128))
    tn = _pick(d, (512, 256, 128))
    return pl.pallas_call(
        _out_proj_body,
        grid=(tp // tm, d // tn),
        in_specs=[
            pl.BlockSpec((tm, tn), lambda i, j: (i, j)),
            pl.BlockSpec((tm, wa), lambda i, j: (i, 0)),
            pl.BlockSpec((tm, wb), lambda i, j: (i, 0)),
            pl.BlockSpec((tm, wc), lambda i, j: (i, 0)),
            pl.BlockSpec((wa, tn), lambda i, j: (0, j)),
            pl.BlockSpec((wb, tn), lambda i, j: (0, j)),
            pl.BlockSpec((wc, tn), lambda i, j: (0, j)),
        ],
        out_specs=pl.BlockSpec((tm, tn), lambda i, j: (i, j)),
        out_shape=jax.ShapeDtypeStruct((tp, d), F32),
        compiler_params=_cparams(("parallel", "parallel")),
        name="out_proj",
    )(h, ya, yb, yc, w_bf16[:wa], w_bf16[wa:wa + wb], w_bf16[wa + wb:])


def _norm_t_body(h_ref, g_ref, o_ref):
    x = h_ref[...]
    ms = jnp.mean(x * x, axis=-1, keepdims=True)
    y = x * lax.rsqrt(ms + NORM_EPS) * g_ref[...]
    o_ref[...] = y.T.astype(o_ref.dtype)


def norm_transposed(h, gain):
    tp, d = h.shape
    tm = _pick(tp, (512, 256, 128))
    return pl.pallas_call(
        _norm_t_body,
        grid=(tp // tm,),
        in_specs=[pl.BlockSpec((tm, d), lambda i: (i, 0)), pl.BlockSpec((1, d), lambda i: (0, 0))],
        out_specs=pl.BlockSpec((d, tm), lambda i: (0, i)),
        out_shape=jax.ShapeDtypeStruct((d, tp), BF16),
        compiler_params=_cparams(("parallel",)),
        name="peer_norm_t",
    )(h, gain.reshape(1, d))


def _top_values(s, k):
    rows = lax.broadcasted_iota(jnp.int32, (k, 1), 0)
    out = jnp.zeros((k, s.shape[1]), F32)
    for r in range(k):
        m = jnp.max(s, axis=0, keepdims=True)
        out = jnp.where(rows == r, m, out)
        s = jnp.where(s == m, -jnp.inf, s)
    return out


def _peer_route_body(wq_ref, zn_ref, keys_ref, s1_ref, s2_ref, e1_ref, e2_ref, thr_ref, *, half, topk):
    qt = _dot(wq_ref[...], zn_ref[...])
    s1 = _dot(keys_ref[0], qt[:half], HI)
    s2 = _dot(keys_ref[1], qt[half:], HI)
    v1 = _top_values(s1, topk)
    v2 = _top_values(s2, topk)
    cand = jnp.concatenate([v1[i:i + 1, :] + v2 for i in range(topk)], axis=0)
    best = _top_values(cand, topk)
    mx = best[0:1, :]
    z = jnp.sum(jnp.exp(best - mx), axis=0, keepdims=True)
    s1_ref[...] = s1
    s2_ref[...] = s2
    e1_ref[...] = jnp.exp(s1 - v1[0:1, :]) / z
    e2_ref[...] = jnp.exp(s2 - v2[0:1, :])
    thr_ref[...] = best[topk - 1:topk, :]


def peer_route(znt, wqt_bf16, keys):
    d, tp = znt.shape
    nh, _, nk, half = keys.shape
    tm = _pick(tp, (512, 256, 128))
    sc = pl.BlockSpec((None, nk, tm), lambda i, h: (h, 0, i))
    big = jax.ShapeDtypeStruct((nh, nk, tp), F32)
    return pl.pallas_call(
        functools.partial(_peer_route_body, half=half, topk=PEER_TOPK),
        grid=(tp // tm, nh),
        in_specs=[
            pl.BlockSpec((2 * half, d), lambda i, h: (h, 0)),
            pl.BlockSpec((d, tm), lambda i, h: (0, i)),
            pl.BlockSpec((None, 2, nk, half), lambda i, h: (h, 0, 0, 0)),
        ],
        out_specs=[sc, sc, sc, sc, pl.BlockSpec((None, 1, tm), lambda i, h: (h, 0, i))],
        out_shape=[big, big, big, big, jax.ShapeDtypeStruct((nh, 1, tp), F32)],
        compiler_params=_cparams(("parallel", "arbitrary")),
        name="peer_route",
    )(wqt_bf16, znt, keys)


def _gelu(x):
    return 0.5 * x * (1.0 + lax.erf(x * (2.0 ** -0.5)))


def _peer_expert_body(zn_ref, u_ref, vt_ref, s1_ref, s2_ref, e1_ref, e2_ref, thr_ref, o_ref, a0_ref, a1_ref,
                      hid_ref, *, nk, nh, nj):
    j = pl.program_id(1)
    te = u_ref.shape[0]
    d = vt_ref.shape[0]
    nq = te // nk
    rows = d // nq

    def project(q):
        hid_ref[q % 2] = _dot(u_ref[q * nk:(q + 1) * nk, :], zn_ref[...])

    def activate(q, dst_ref):
        a = j * nq + q
        gate = None
        for h in range(nh):
            cand = s1_ref[h, pl.ds(a, 1), :] + s2_ref[h]
            gh = jnp.where(cand >= thr_ref[h], e1_ref[h, pl.ds(a, 1), :] * e2_ref[h], 0.0)
            gate = gh if gate is None else gate + gh
        dst_ref[q * nk:(q + 1) * nk, :] = (gate * _gelu(hid_ref[q % 2])).astype(BF16)

    def apply(q, src_ref):
        sl = slice(q * rows, (q + 1) * rows)
        o_ref[sl, :] += _dot(vt_ref[sl, :], src_ref[...])

    def step(dst_ref, src_ref):
        project(0)
        for q in range(nq):
            if q + 1 < nq:
                project(q + 1)
            activate(q, dst_ref)
            if src_ref is not None:
                apply(q, src_ref)

    @pl.when(j == 0)
    def _():
        o_ref[...] = jnp.zeros_like(o_ref)
        step(a0_ref, None)

    for par, (dst_ref, src_ref) in enumerate(((a0_ref, a1_ref), (a1_ref, a0_ref))):
        @pl.when((j > 0) & (j < nj) & (lax.rem(j, 2) == par))
        def _(dst_ref=dst_ref, src_ref=src_ref):
            step(dst_ref, src_ref)

    @pl.when(j == nj)
    def _():
        for q in range(nq):
            apply(q, a1_ref if nj % 2 == 0 else a0_ref)


def peer_experts(znt, u_bf16, vt_bf16, s1, s2, e1, e2, thr):
    d, tp = znt.shape
    ne = u_bf16.shape[0]
    nh, nk, _ = s1.shape
    tm = _pick(tp, (512, 256, 128))
    te = _pick(ne, (512, 256, 128))
    nj = ne // te
    once = dict(pipeline_mode=pl.Buffered(1))
    aux = pl.BlockSpec((nh, nk, tm), lambda i, j: (0, 0, i), **once)
    return pl.pallas_call(
        functools.partial(_peer_expert_body, nk=nk, nh=nh, nj=nj),
        grid=(tp // tm, nj + 1),
        in_specs=[
            pl.BlockSpec((d, tm), lambda i, j: (0, i), **once),
            pl.BlockSpec((te, d), lambda i, j: (jnp.minimum(j, nj - 1), 0)),
            pl.BlockSpec((d, te), lambda i, j: (0, jnp.maximum(j - 1, 0))),
            aux, aux, aux, aux,
            pl.BlockSpec((nh, 1, tm), lambda i, j: (0, 0, i), **once),
        ],
        out_specs=pl.BlockSpec((d, tm), lambda i, j: (0, i)),
        out_shape=jax.ShapeDtypeStruct((d, tp), F32),
        scratch_shapes=[pltpu.VMEM((te, tm), BF16), pltpu.VMEM((te, tm), BF16), pltpu.VMEM((2, nk, tm), F32)],
        compiler_params=_cparams(("parallel", "arbitrary")),
        name="peer_experts",
    )(znt, u_bf16, vt_bf16, s1, s2, e1, e2, thr)


def _residual_t_body(h_ref, yt_ref, o_ref):
    o_ref[...] = h_ref[...] + yt_ref[...].T


def residual_add_transposed(h, yt):
    tp, d = h.shape
    tm = _pick(tp, (512, 256, 128))
    return pl.pallas_call(
        _residual_t_body,
        grid=(tp // tm,),
        in_specs=[pl.BlockSpec((tm, d), lambda i: (i, 0)), pl.BlockSpec((d, tm), lambda i: (0, i))],
        out_specs=pl.BlockSpec((tm, d), lambda i: (i, 0)),
        out_shape=jax.ShapeDtypeStruct((tp, d), F32),
        compiler_params=_cparams(("parallel",)),
        name="peer_residual",
    )(h, yt)


def _final_norm_body(h_ref, g_ref, o_ref):
    x = h_ref[...]
    ms = jnp.mean(x * x, axis=-1, keepdims=True)
    o_ref[...] = x * lax.rsqrt(ms + NORM_EPS) * g_ref[...]


def final_norm(h, gain, batch, lp):
    tp, d = h.shape
    nb = lp // SWA_BLOCK
    seq = lp - SWA_BLOCK
    out = pl.pallas_call(
        _final_norm_body,
        grid=(batch, nb - 1),
        in_specs=[pl.BlockSpec((SWA_BLOCK, d), lambda b, n: (b * nb + n + 1, 0)),
                  pl.BlockSpec((1, d), lambda b, n: (0, 0))],
        out_specs=pl.BlockSpec((SWA_BLOCK, d), lambda b, n: (b * (nb - 1) + n, 0)),
        out_shape=jax.ShapeDtypeStruct((batch * seq, d), F32),
        compiler_params=_cparams(("parallel", "parallel")),
        name="final_norm",
    )(h, gain.reshape(1, d))
    return out.reshape(batch, seq, d)


def _rope_tables(lp, width, rot_heads):
    pos = jnp.maximum(jnp.arange(lp, dtype=F32) - PAD_FRONT, 0.0)
    inv_freq = ROPE_THETA ** (-jnp.arange(0, ROPE_DIMS, 2, dtype=F32) / ROPE_DIMS)
    ang = pos[:, None] * inv_freq[None, :]
    cos, sin = jnp.cos(ang), jnp.sin(ang)
    half = ROPE_DIMS // 2
    rest = HEAD_DIM - ROPE_DIMS
    one = jnp.ones((lp, rest), F32)
    zero = jnp.zeros((lp, rest), F32)
    zh = jnp.zeros((lp, half), F32)
    c_head = jnp.concatenate([cos, cos, one], axis=1)
    a_head = jnp.concatenate([-sin, zh, zero], axis=1)
    b_head = jnp.concatenate([zh, sin, zero], axis=1)
    n_heads = width // HEAD_DIM
    idn = jnp.ones((lp, HEAD_DIM), F32)
    zer = jnp.zeros((lp, HEAD_DIM), F32)
    build = lambda rot, flat: jnp.concatenate([rot if i < rot_heads else flat for i in range(n_heads)], axis=1)
    return build(c_head, idn), build(a_head, zer), build(b_head, zer)


def _layout(d, rc, sw, kvw, hw):
    qb = sw * HEAD_DIM // kvw
    kvb = _roundup(2 * kvw, LANES)
    hg_off = _roundup(rc, hw)
    q_off = _roundup(hg_off + 4 * hw, qb)
    kv_off = _roundup(q_off + sw, kvb)
    total = _roundup(kv_off + kvb, 512)
    return q_off, kv_off, kvb, hg_off, total


def kernel(x, meta_tokens, norm_mix, norm_ffn, norm_final, w_in, w_out, rwkv_mu, rwkv_w0, rwkv_w2, rwkv_a0,
           rwkv_a2, rwkv_g2, rwkv_k_k, rwkv_k_a, rwkv_r_k, rwkv_lnx_w, rwkv_lnx_b, swa_sinks, hgrn_lb, hgrn_norm,
           peer_wq, peer_keys, peer_u, peer_v):
    batch, seq, d = x.shape
    depth = w_in.shape[0]
    lp = seq + SWA_BLOCK
    rw = rwkv_w0.shape[1]
    rc = rwkv_mu.shape[1]
    n_q = swa_sinks.shape[1]
    sw = n_q * HEAD_DIM
    hw = hgrn_lb.shape[1]
    kvw = (w_in.shape[2] - rc - sw - 4 * hw) // 2
    n_kv = kvw // HEAD_DIM
    q_off, kv_off, kvb, hg_off, n_tot = _layout(d, rc, sw, kvw, hw)

    pad = jnp.zeros((batch, PAD_FRONT, d), x.dtype)
    meta = jnp.broadcast_to(meta_tokens.astype(x.dtype)[None], (batch, N_META, d))
    h = jnp.concatenate([pad, meta, x], axis=1).reshape(batch * lp, d)

    q_tabs = _rope_tables(lp, sw // n_kv, sw // n_kv // HEAD_DIM)
    kv_tabs = _rope_tables(lp, kvb, n_kv)
    lb_all = jnp.cumsum(jax.nn.softmax(hgrn_lb.astype(F32), axis=0), axis=0)
    lb_all = lb_all - lb_all[0:1]

    for l in range(depth):
        w = w_in[l]
        zc = lambda n: jnp.zeros((d, n), w.dtype)
        w_re = jnp.concatenate([
            w[:, :rc], zc(hg_off - rc),
            w[:, rc + sw + 2 * kvw:], zc(q_off - hg_off - 4 * hw),
            w[:, rc:rc + sw], zc(kv_off - q_off - sw),
            w[:, rc + sw:rc + sw + 2 * kvw], zc(n_tot - kv_off - 2 * kvw)], axis=1).astype(BF16)
        proj = norm_matmul(h, norm_mix[l], w_re, lp)

        rp = dict(mu=rwkv_mu[l], w0=rwkv_w0[l], w2=rwkv_w2[l], a0=rwkv_a0[l], a2=rwkv_a2[l], g2=rwkv_g2[l],
                  k_k=rwkv_k_k[l], k_a=rwkv_k_a[l], r_k=rwkv_r_k[l], lnx_w=rwkv_lnx_w[l], lnx_b=rwkv_lnx_b[l])
        y_a = rwkv_scan(*rwkv_prep(proj, lp, rp), batch, lp)

        kv = swa_kv_rope(proj, kv_off, kvb, kv_tabs, batch, lp)
        kvh = jnp.stack([jnp.concatenate([kv[:, i * HEAD_DIM:(i + 1) * HEAD_DIM],
                                          kv[:, kvw + i * HEAD_DIM:kvw + (i + 1) * HEAD_DIM]], axis=1)
                         for i in range(n_kv)])
        y_b = swa_attention(proj, q_off, kvh, swa_sinks[l], q_tabs, batch, lp, n_q, n_kv)

        y_c = hgrn_mix(proj, hg_off, lb_all[l], hgrn_norm[l], batch, lp)

        h = out_proj(h, y_a, y_b, y_c, w_out[l].astype(BF16))

        znt = norm_transposed(h, norm_ffn[l])
        s1, s2, e1, e2, thr = peer_route(znt, peer_wq[l].T.astype(BF16), peer_keys[l])
        yt = peer_experts(znt, peer_u[l].astype(BF16), peer_v[l].T.astype(BF16), s1, s2, e1, e2, thr)
        h = residual_add_transposed(h, yt)

    return final_norm(h, norm_final, batch, lp)
```

```python
import functools

import jax
import jax.numpy as jnp
from jax import lax
from jax.experimental import pallas as pl
from jax.experimental.pallas import tpu as pltpu

F32 = jnp.float32
BF16 = jnp.bfloat16
HI = lax.Precision.HIGHEST

N_META = 16
HEAD_DIM = 64
NORM_EPS = 1e-5
NEG_INF = -1e30
RWKV_GN_EPS = 64e-5
SWA_BLOCK = 128
PAD_FRONT = SWA_BLOCK - N_META
ROPE_THETA = 500000.0
ROPE_DIMS = HEAD_DIM // 4
HGRN_HEAD_DIM = 128
CHUNK = 64
PEER_TOPK = 16
LANES = 128
MXU_WIDTH = 256
VMEM_LIMIT_MB = 56
RWKV_HEADS_PER_STEP = 24
SWA_MODE = "bf16"
PEER_SPLIT = 4
HGRN_HEADS_PER_STEP = 8
HGRN_MODE = "bf16"
HGRN_SUB = 16
RWKV_SCAN_MODE = "bf16"


def _cparams(sem, vmem_mb=VMEM_LIMIT_MB):
    return pltpu.CompilerParams(dimension_semantics=sem, vmem_limit_bytes=vmem_mb * 1024 * 1024)


def _pick(n, cands):
    for c in cands:
        if n % c == 0:
            return c
    raise ValueError(f"no tile in {cands} divides {n}")


def _roundup(n, m):
    return -(-n // m) * m


def _dot(a, b, prec=None):
    return jnp.dot(a, b, preferred_element_type=F32, precision=prec)


def _dot_nt(a, b, prec=None):
    return lax.dot_general(a, b, (((1,), (1,)), ((), ())), preferred_element_type=F32, precision=prec)


def _dot_tn(a, b, prec=None):
    return lax.dot_general(a, b, (((0,), (0,)), ((), ())), preferred_element_type=F32, precision=prec)


def _row_in_batch(i, tm, lp):
    pos0 = lax.rem(i * tm, lp)
    rb = pos0 + lax.broadcasted_iota(jnp.int32, (tm, 1), 0)
    return jnp.where(rb >= lp, rb - lp, rb)


def _sigmoid(x):
    return 1.0 / (1.0 + jnp.exp(-x))


def _norm_mm_body(x_ref, g_ref, w_ref, o_ref, z_ref, *, lp, tm):
    i = pl.program_id(0)

    @pl.when(pl.program_id(1) == 0)
    def _():
        x = x_ref[...]
        ms = jnp.mean(x * x, axis=-1, keepdims=True)
        y = x * lax.rsqrt(ms + NORM_EPS) * g_ref[...]
        z_ref[...] = jnp.where(_row_in_batch(i, tm, lp) >= PAD_FRONT, y, 0.0).astype(BF16)

    o_ref[...] = _dot(z_ref[...], w_ref[...])


def norm_matmul(h, gain, w_bf16, lp):
    tp, d = h.shape
    n = w_bf16.shape[1]
    tm = _pick(tp, (512, 256, 128))
    tn = _pick(n, (1024, 512, 256, 128))
    return pl.pallas_call(
        functools.partial(_norm_mm_body, lp=lp, tm=tm),
        grid=(tp // tm, n // tn),
        in_specs=[
            pl.BlockSpec((tm, d), lambda i, j: (i, 0)),
            pl.BlockSpec((1, d), lambda i, j: (0, 0)),
            pl.BlockSpec((d, tn), lambda i, j: (0, j)),
        ],
        out_specs=pl.BlockSpec((tm, tn), lambda i, j: (i, j)),
        out_shape=jax.ShapeDtypeStruct((tp, n), F32),
        scratch_shapes=[pltpu.VMEM((tm, d), BF16)],
        compiler_params=_cparams(("parallel", "arbitrary")),
        name="norm_in_proj",
    )(h, gain.reshape(1, d), w_bf16)


def _mm(a, b, dims, mode):
    if mode == "bf16":
        return lax.dot_general(a.astype(BF16), b.astype(BF16), (dims, ((), ())), preferred_element_type=F32)
    return lax.dot_general(a, b, (dims, ((), ())), preferred_element_type=F32, precision=HI)


_NN = ((1,), (0,))
_NT = ((1,), (1,))
_TN = ((0,), (0,))


def _split3(x):
    hi = x.astype(BF16)
    r1 = x - hi.astype(F32)
    mid = r1.astype(BF16)
    lo = (r1 - mid.astype(F32)).astype(BF16)
    return hi, mid, lo


def _select_sum(sel, x, x_on_left=False):
    sel = sel.astype(BF16)
    terms = [_dot(t, sel) if x_on_left else _dot(sel, t) for t in _split3(x)]
    return (terms[2] + terms[1]) + terms[0]


def _head_sums(x):
    shift = HEAD_DIM.bit_length() - 1
    er = lax.broadcasted_iota(jnp.int32, (LANES, LANES), 0) >> shift
    ec = lax.broadcasted_iota(jnp.int32, (LANES, LANES), 1) >> shift
    ones_bd = jnp.where(er == ec, 1.0, 0.0)
    return jnp.concatenate(
        [_select_sum(ones_bd, x[:, s * LANES:(s + 1) * LANES], x_on_left=True)
         for s in range(x.shape[1] // LANES)], axis=1)


def _rwkv_prep_body(pa_ref, prev_ref, mu_ref, w0_ref, w2_ref, a0_ref, a2_ref, g2_ref, kk_ref, ka_ref, rk_ref,
                    lnw_ref, lnb_ref, out_ref, post_ref, dl_ref, *, lp, tm, rw, rd, ri, rg, c):
    i = pl.program_id(0)
    x = pa_ref[...]
    rowi = lax.broadcasted_iota(jnp.int32, (tm, 1), 0)
    prev_row = jnp.where(lax.rem(i * tm, lp) == 0, 0.0, prev_ref[7:8, :])
    prev = jnp.where(rowi == 0, prev_row, pltpu.roll(x, 1, 0))
    xs = x + (prev - x) * mu_ref[...]
    r = xs[:, 0:rw]
    k = xs[:, rw:2 * rw]
    v = xs[:, 2 * rw:3 * rw]
    wl = xs[:, 3 * rw:3 * rw + rd]
    al = xs[:, 3 * rw + rd:3 * rw + rd + ri]
    gl = xs[:, 3 * rw + rd + ri:3 * rw + rd + ri + rg]
    wv = w0_ref[...] + _dot(jnp.tanh(wl), w2_ref[...], HI)
    nwv = -wv
    softplus = jnp.maximum(nwv, 0.0) + jnp.log(1.0 + jnp.exp(-jnp.abs(nwv)))
    logw = -jnp.exp(-softplus - 0.5)
    alr = _sigmoid(a0_ref[...] + _dot(al, a2_ref[...], HI))
    g = _dot(_sigmoid(gl), g2_ref[...], HI)
    kkr = k * kk_ref[...]
    kk = kkr / jnp.maximum(jnp.sqrt(_head_sums(kkr * kkr)), 1e-12)
    k2 = k * (1.0 + (alr - 1.0) * ka_ref[...])
    shift = c.bit_length() - 1
    tr = lax.broadcasted_iota(jnp.int32, (tm, tm), 0)
    tc = lax.broadcasted_iota(jnp.int32, (tm, tm), 1)
    ltri = jnp.where(tc <= tr, jnp.where((tr >> shift) == (tc >> shift), 1.0, 0.0), 0.0)
    cum = _select_sum(ltri, logw)
    tot = jnp.concatenate(
        [jnp.broadcast_to(cum[(q + 1) * c - 1:(q + 1) * c, :], (c, rw)) for q in range(tm // c)], axis=0)
    p_inv = jnp.exp(-cum)
    p_rem = jnp.exp(tot - cum)
    kb = kk * alr
    od = out_ref.dtype
    out_ref[0] = (-kk * jnp.exp(cum - logw)).astype(od)
    out_ref[1] = (r * jnp.exp(cum)).astype(od)
    out_ref[2] = (kb * p_inv).astype(od)
    out_ref[3] = (k2 * p_inv).astype(od)
    out_ref[4] = v.astype(od)
    out_ref[5] = (kb * p_rem).astype(od)
    out_ref[6] = (k2 * p_rem).astype(od)
    post_ref[0] = lnw_ref[...] * g
    post_ref[1] = (lnb_ref[...] + _head_sums(r * k2 * rk_ref[...]) * v) * g
    for q in range(tm // c):
        dl_ref[q] = jnp.exp(tot[q * c:q * c + 1, :])


def rwkv_prep(proj, lp, p):
    tp = proj.shape[0]
    rw = p["w0"].shape[0]
    rd, ri, rg = p["w2"].shape[0], p["a2"].shape[0], p["g2"].shape[0]
    rc = 3 * rw + rd + ri + rg
    tm = 128
    c = CHUNK
    row = lambda a: a.reshape(1, -1)
    full = lambda a: pl.BlockSpec(a.shape, lambda i: (0, 0))
    args = [row(p["mu"]), row(p["w0"]), p["w2"], row(p["a0"]), p["a2"], p["g2"], row(p["k_k"]), row(p["k_a"]),
            row(p["r_k"]), row(p["lnx_w"]), row(p["lnx_b"])]
    return pl.pallas_call(
        functools.partial(_rwkv_prep_body, lp=lp, tm=tm, rw=rw, rd=rd, ri=ri, rg=rg, c=c),
        grid=(tp // tm,),
        in_specs=[
            pl.BlockSpec((tm, rc), lambda i: (i, 0)),
            pl.BlockSpec((8, rc), lambda i: (jnp.maximum(i * (tm // 8) - 1, 0), 0)),
        ] + [full(a) for a in args],
        out_specs=[pl.BlockSpec((7, tm, rw), lambda i: (0, i, 0)),
                   pl.BlockSpec((2, tm, rw), lambda i: (0, i, 0)),
                   pl.BlockSpec((tm // c, 1, rw), lambda i: (i, 0, 0))],
        out_shape=[jax.ShapeDtypeStruct((7, tp, rw), BF16 if RWKV_SCAN_MODE == "bf16" else F32),
                   jax.ShapeDtypeStruct((2, tp, rw), F32), jax.ShapeDtypeStruct((tp // c, 1, rw), F32)],
        compiler_params=_cparams(("parallel",)),
        name="rwkv_prep",
    )(proj, proj, *args)


def _rwkv_scan_body(x_ref, post_ref, dl_ref, o_ref, s_ref, *, heads, c, mode):
    @pl.when(pl.program_id(2) == 0)
    def _():
        s_ref[...] = jnp.zeros_like(s_ref)

    ri = lax.broadcasted_iota(jnp.int32, (c, c), 0)
    ci = lax.broadcasted_iota(jnp.int32, (c, c), 1)
    strict = ci < ri
    incl = ci <= ri
    hs = range(heads)
    ld = lambda q, h: x_ref[q, :, h * HEAD_DIM:(h + 1) * HEAD_DIM]
    cat0 = lambda a, b: jnp.concatenate([a, b], axis=0)
    cat1 = lambda a, b: jnp.concatenate([a, b], axis=1)
    ar = [cat0(ld(0, h), ld(1, h)) for h in hs]
    m = [_mm(ar[h], cat0(ld(2, h), ld(3, h)), _NT, mode) for h in hs]
    s0 = [s_ref[h] for h in hs]
    ars = [_mm(ar[h], s0[h], _NT, mode) for h in hs]
    v = [ld(4, h) for h in hs]
    nk = [jnp.where(strict, m[h][:c, :c], 0.0) for h in hs]
    m_ak = [jnp.where(strict, m[h][:c, c:], 0.0) for h in hs]
    m_r = [cat1(jnp.where(incl, m[h][c:, :c], 0.0), jnp.where(incl, m[h][c:, c:], 0.0)) for h in hs]
    u = [ars[h][:c] + _mm(m_ak[h], v[h], _NN, mode) for h in hs]
    steps = c.bit_length() - 1
    for it in range(steps):
        if it + 1 < steps:
            both = [_mm(nk[h], cat1(u[h], nk[h]), _NN, mode) for h in hs]
            u = [u[h] + both[h][:, :c] for h in hs]
            nk = [both[h][:, c:] for h in hs]
        else:
            u = [u[h] + _mm(nk[h], u[h], _NN, mode) for h in hs]
    uv = [cat0(u[h].astype(v[h].dtype), v[h]) for h in hs]
    y = [ars[h][c:] + _mm(m_r[h], uv[h], _NN, mode) for h in hs]
    for h in hs:
        dl = dl_ref[0, :, h * HEAD_DIM:(h + 1) * HEAD_DIM]
        s_ref[h] = s0[h] * dl + _mm(uv[h], cat0(ld(5, h), ld(6, h)), _TN, mode)
    outs = []
    for h in hs:
        mean = jnp.mean(y[h], axis=-1, keepdims=True)
        yc = y[h] - mean
        var = jnp.mean(yc * yc, axis=-1, keepdims=True)
        hl = slice(h * HEAD_DIM, (h + 1) * HEAD_DIM)
        outs.append(yc * lax.rsqrt(var + RWKV_GN_EPS) * post_ref[0, :, hl] + post_ref[1, :, hl])
    o_ref[...] = jnp.concatenate(outs, axis=1).astype(o_ref.dtype)


def rwkv_scan(xs, post, dl, batch, lp):
    _, tp, rw = xs.shape
    c = CHUNK
    nc = lp // c
    heads = _pick(rw // HEAD_DIM, (RWKV_HEADS_PER_STEP, 2))
    gw = heads * HEAD_DIM
    return pl.pallas_call(
        functools.partial(_rwkv_scan_body, heads=heads, c=c, mode=RWKV_SCAN_MODE),
        grid=(batch, rw // gw, nc),
        in_specs=[pl.BlockSpec((7, c, gw), lambda b, hg, ch: (0, b * nc + ch, hg)),
                  pl.BlockSpec((2, c, gw), lambda b, hg, ch: (0, b * nc + ch, hg)),
                  pl.BlockSpec((1, 1, gw), lambda b, hg, ch: (b * nc + ch, 0, hg))],
        out_specs=pl.BlockSpec((c, gw), lambda b, hg, ch: (b * nc + ch, hg)),
        out_shape=jax.ShapeDtypeStruct((tp, rw), BF16),
        scratch_shapes=[pltpu.VMEM((heads, HEAD_DIM, HEAD_DIM), F32)],
        compiler_params=_cparams(("parallel", "parallel", "arbitrary")),
        name="rwkv_scan",
    )(xs, post, dl)


def _rope(x, cos, sa, sb):
    parts = []
    for s in range(x.shape[1] // LANES):
        sl = slice(s * LANES, (s + 1) * LANES)
        xs = x[:, sl]
        parts.append(xs * cos[:, sl] + pltpu.roll(xs, LANES - ROPE_DIMS // 2, 1) * sa[:, sl]
                     + pltpu.roll(xs, ROPE_DIMS // 2, 1) * sb[:, sl])
    return parts[0] if len(parts) == 1 else jnp.concatenate(parts, axis=1)


def _kv_rope_body(kv_ref, cos_ref, sa_ref, sb_ref, o_ref):
    o_ref[...] = _rope(kv_ref[...], cos_ref[...], sa_ref[...], sb_ref[...])


def swa_kv_rope(proj, kv_off, kvb, tabs, batch, lp):
    tp = proj.shape[0]
    nb = lp // SWA_BLOCK
    tab = pl.BlockSpec((SWA_BLOCK, kvb), lambda b, n: (n, 0))
    return pl.pallas_call(
        _kv_rope_body,
        grid=(batch, nb),
        in_specs=[pl.BlockSpec((SWA_BLOCK, kvb), lambda b, n: (b * nb + n, kv_off // kvb)), tab, tab, tab],
        out_specs=pl.BlockSpec((SWA_BLOCK, kvb), lambda b, n: (b * nb + n, 0)),
        out_shape=jax.ShapeDtypeStruct((tp, kvb), F32),
        compiler_params=_cparams(("parallel", "parallel")),
        name="swa_kv_rope",
    )(proj, *tabs)


def _swa_body(sink_ref, q_ref, kc_ref, kp_ref, km_ref, cos_ref, sa_ref, sb_ref, o_ref, *, group, kvw):
    n = pl.program_id(1)
    hk = pl.program_id(2)
    blk = SWA_BLOCK
    mode = SWA_MODE
    q = _rope(q_ref[...], cos_ref[...], sa_ref[...], sb_ref[...]) * (HEAD_DIM ** -0.5)
    qs = jnp.concatenate([q[:, gi * HEAD_DIM:(gi + 1) * HEAD_DIM] for gi in range(group)], axis=0)
    rows = group * blk
    qi = lax.broadcasted_iota(jnp.int32, (rows, blk), 0) & (blk - 1)
    kj = lax.broadcasted_iota(jnp.int32, (rows, blk), 1)
    far = 1 << 20
    cur_ok = kj + jnp.where(n >= 1, 0, far) <= qi
    prev_ok = kj > qi + jnp.where(n >= 2, 0, far)
    mi = lax.broadcasted_iota(jnp.int32, (rows, N_META), 0) & (blk - 1)
    mj = lax.broadcasted_iota(jnp.int32, (rows, N_META), 1)
    meta_ok = mj <= mi - PAD_FRONT + jnp.where(n >= 1, far, 0)
    sink = jnp.concatenate([jnp.full((blk, 1), sink_ref[hk * group + gi], F32) for gi in range(group)], axis=0)
    s_c = jnp.where(cur_ok, _mm(qs, kc_ref[:, 0:HEAD_DIM], _NT, mode), NEG_INF)
    s_p = jnp.where(prev_ok, _mm(qs, kp_ref[:, 0:HEAD_DIM], _NT, mode), NEG_INF)
    s_m = jnp.where(meta_ok, _mm(qs, km_ref[:, 0:HEAD_DIM], _NT, mode), NEG_INF)
    mx = jnp.maximum(jnp.maximum(jnp.max(s_c, axis=-1, keepdims=True), jnp.max(s_p, axis=-1, keepdims=True)),
                     jnp.maximum(jnp.max(s_m, axis=-1, keepdims=True), sink))
    p_c = jnp.exp(s_c - mx)
    p_p = jnp.exp(s_p - mx)
    p_m = jnp.exp(s_m - mx)
    den = (jnp.sum(p_c, axis=-1, keepdims=True) + jnp.sum(p_p, axis=-1, keepdims=True)
           + jnp.sum(p_m, axis=-1, keepdims=True) + jnp.exp(sink - mx))
    acc = (_mm(p_c, kc_ref[:, HEAD_DIM:2 * HEAD_DIM], _NN, mode) + _mm(p_p, kp_ref[:, HEAD_DIM:2 * HEAD_DIM], _NN, mode)
           + _mm(p_m, km_ref[:, HEAD_DIM:2 * HEAD_DIM], _NN, mode))
    out = acc / den
    o_ref[...] = jnp.concatenate([out[gi * blk:(gi + 1) * blk, :] for gi in range(group)], axis=1).astype(o_ref.dtype)


def swa_attention(proj, q_off, kvh, sinks, qtabs, batch, lp, n_q_heads, n_kv_heads):
    tp = proj.shape[0]
    nb = lp // SWA_BLOCK
    group = n_q_heads // n_kv_heads
    qb = group * HEAD_DIM
    sw = n_q_heads * HEAD_DIM
    tab = pl.BlockSpec((SWA_BLOCK, qb), lambda b, n, hk: (n, 0))
    kv_spec = lambda f: pl.BlockSpec((None, SWA_BLOCK, LANES), f)
    return pl.pallas_call(
        functools.partial(_swa_body, group=group, kvw=n_kv_heads * HEAD_DIM),
        grid=(batch, nb, n_kv_heads),
        in_specs=[
            pl.BlockSpec(memory_space=pltpu.SMEM),
            pl.BlockSpec((SWA_BLOCK, qb), lambda b, n, hk: (b * nb + n, q_off // qb + hk)),
            kv_spec(lambda b, n, hk: (hk, b * nb + n, 0)),
            kv_spec(lambda b, n, hk: (hk, b * nb + jnp.maximum(n - 1, 0), 0)),
            pl.BlockSpec((None, N_META, LANES),
                         lambda b, n, hk: (hk, b * (lp // N_META) + PAD_FRONT // N_META, 0)),
            tab, tab, tab,
        ],
        out_specs=pl.BlockSpec((SWA_BLOCK, qb), lambda b, n, hk: (b * nb + n, hk)),
        out_shape=jax.ShapeDtypeStruct((tp, sw), BF16),
        compiler_params=_cparams(("parallel", "parallel", "parallel")),
        name="swa_attention",
    )(sinks, proj, kvh, kvh, kvh, *qtabs)


def _hgrn_body(q_ref, f_ref, i_ref, g_ref, lb_ref, gain_ref, o_ref, st_ref, *, c, heads, sub):
    @pl.when(pl.program_id(2) == 0)
    def _():
        st_ref[...] = jnp.zeros_like(st_ref)

    dk = HGRN_HEAD_DIM
    hs = range(heads)
    nb = c // sub
    ri = lax.broadcasted_iota(jnp.int32, (c, c), 0)
    ci = lax.broadcasted_iota(jnp.int32, (c, c), 1)
    incl = ci <= ri
    ltri = jnp.where(incl, 1.0, 0.0)
    cs = lax.broadcasted_iota(jnp.int32, (sub, c), 1)
    col = lambda ref, h: ref[:, h * dk:(h + 1) * dk]
    q = [col(q_ref, h) for h in hs]
    iv = [col(i_ref, h) for h in hs]
    f = [col(lb_ref, h) + (1.0 - col(lb_ref, h)) * _sigmoid(col(f_ref, h)) for h in hs]
    kx = [1.0 - f[h] for h in hs]
    gc = [_select_sum(ltri, jnp.log(f[h])) for h in hs]
    st = [st_ref[h] for h in hs]
    mode = HGRN_MODE
    o_inter = [_mm(q[h] * jnp.exp(gc[h]), st[h], _NT, mode) for h in hs]
    for h in hs:
        g_last = gc[h][c - 1:c, :]
        st_ref[h] = st[h] * jnp.exp(g_last) + _mm(iv[h], kx[h] * jnp.exp(g_last - gc[h]), _TN, mode)
    attn = []
    for h in hs:
        blocks = []
        for b in range(nb):
            rows = slice(b * sub, (b + 1) * sub)
            qb, gb = q[h][rows, :], gc[h][rows, :]
            if b == 0:
                blk = jnp.zeros((sub, c), F32)
            else:
                ref = gc[h][b * sub:b * sub + 1, :]
                k_sc = kx[h] * jnp.exp(jnp.minimum(ref - gc[h], 0.0))
                blk = _mm(qb * jnp.exp(gb - ref), k_sc, _NT, mode)
            for s in range(b * sub, (b + 1) * sub):
                dec = jnp.exp(jnp.minimum(gb - gc[h][s:s + 1, :], 0.0))
                blk = jnp.where(cs == s, jnp.sum(qb * kx[h][s:s + 1, :] * dec, axis=-1, keepdims=True), blk)
            blocks.append(blk)
        attn.append(jnp.where(incl, jnp.concatenate(blocks, axis=0), 0.0))
    outs = []
    for h in hs:
        o = _mm(attn[h], iv[h], _NN, mode) + o_inter[h]
        on = o * lax.rsqrt(jnp.mean(o * o, axis=-1, keepdims=True) + NORM_EPS) * col(gain_ref, h)
        g = col(g_ref, h)
        outs.append(on * (g * _sigmoid(g)))
    o_ref[...] = jnp.concatenate(outs, axis=1).astype(o_ref.dtype)


def hgrn_mix(proj, hg_off, lb, gain, batch, lp):
    tp = proj.shape[0]
    hw = lb.shape[0]
    dk = HGRN_HEAD_DIM
    nh = hw // dk
    c = CHUNK
    nc = lp // c
    heads = _pick(nh, (HGRN_HEADS_PER_STEP, 2, 1))
    gw = heads * dk
    blk = lambda part: pl.BlockSpec((c, gw), lambda b, h, ch: (b * nc + ch, (hg_off + part * hw) // gw + h))
    vec = pl.BlockSpec((1, gw), lambda b, h, ch: (0, h))
    return pl.pallas_call(
        functools.partial(_hgrn_body, c=c, heads=heads, sub=HGRN_SUB),
        grid=(batch, nh // heads, nc),
        in_specs=[blk(0), blk(1), blk(2), blk(3), vec, vec],
        out_specs=pl.BlockSpec((c, gw), lambda b, h, ch: (b * nc + ch, h)),
        out_shape=jax.ShapeDtypeStruct((tp, hw), BF16),
        scratch_shapes=[pltpu.VMEM((heads, dk, dk), F32)],
        compiler_params=_cparams(("parallel", "parallel", "arbitrary")),
        name="hgrn_mix",
    )(proj, proj, proj, proj, lb.reshape(1, hw), gain.reshape(1, hw))


def _out_proj_body(h_ref, ya_ref, yb_ref, yc_ref, wa_ref, wb_ref, wc_ref, o_ref):
    o_ref[...] = (h_ref[...] + _dot(ya_ref[...], wa_ref[...]) + _dot(yb_ref[...], wb_ref[...])
                  + _dot(yc_ref[...], wc_ref[...]))


def out_proj(h, ya, yb, yc, w_bf16):
    tp, d = h.shape
    wa, wb, wc = ya.shape[1], yb.shape[1], yc.shape[1]
    tm = _pick(tp, (512, 256, 128))
    tn = _pick(d, (1024, 512, 256, 128))
    return pl.pallas_call(
        _out_proj_body,
        grid=(tp // tm, d // tn),
        in_specs=[
            pl.BlockSpec((tm, tn), lambda i, j: (i, j)),
            pl.BlockSpec((tm, wa), lambda i, j: (i, 0)),
            pl.BlockSpec((tm, wb), lambda i, j: (i, 0)),
            pl.BlockSpec((tm, wc), lambda i, j: (i, 0)),
            pl.BlockSpec((wa, tn), lambda i, j: (0, j)),
            pl.BlockSpec((wb, tn), lambda i, j: (0, j)),
            pl.BlockSpec((wc, tn), lambda i, j: (0, j)),
        ],
        out_specs=pl.BlockSpec((tm, tn), lambda i, j: (i, j)),
        out_shape=jax.ShapeDtypeStruct((tp, d), F32),
        compiler_params=_cparams(("parallel", "parallel")),
        name="out_proj",
    )(h, ya, yb, yc, w_bf16[:wa], w_bf16[wa:wa + wb], w_bf16[wa + wb:])


def _norm_t_body(h_ref, g_ref, o_ref):
    x = h_ref[...]
    ms = jnp.mean(x * x, axis=-1, keepdims=True)
    y = x * lax.rsqrt(ms + NORM_EPS) * g_ref[...]
    o_ref[...] = y.T.astype(o_ref.dtype)


def norm_transposed(h, gain):
    tp, d = h.shape
    tm = _pick(tp, (512, 256, 128))
    return pl.pallas_call(
        _norm_t_body,
        grid=(tp // tm,),
        in_specs=[pl.BlockSpec((tm, d), lambda i: (i, 0)), pl.BlockSpec((1, d), lambda i: (0, 0))],
        out_specs=pl.BlockSpec((d, tm), lambda i: (0, i)),
        out_shape=jax.ShapeDtypeStruct((d, tp), BF16),
        compiler_params=_cparams(("parallel",)),
        name="peer_norm_t",
    )(h, gain.reshape(1, d))


def _top_values(s, k):
    rows = lax.broadcasted_iota(jnp.int32, (k, 1), 0)
    out = jnp.zeros((k, s.shape[1]), F32)
    for r in range(k):
        m = jnp.max(s, axis=0, keepdims=True)
        out = jnp.where(rows == r, m, out)
        s = jnp.where(s == m, -jnp.inf, s)
    return out


def _peer_route_body(wq_ref, zn_ref, keys_ref, s1_ref, s2_ref, e1_ref, e2_ref, thr_ref, *, half, topk):
    qt = _dot(wq_ref[...], zn_ref[...])
    s1 = _dot(keys_ref[0], qt[:half], HI)
    s2 = _dot(keys_ref[1], qt[half:], HI)
    v1 = _top_values(s1, topk)
    v2 = _top_values(s2, topk)
    sub8 = lax.broadcasted_iota(jnp.int32, (8, 1), 0)
    parts = [v1[0:1, :] + v2]
    i = 1
    while topk // (i + 1) > 1:
        n_j = topk // (i + 1)
        parts.append(jnp.where(sub8 < n_j, v1[i:i + 1, :] + v2[0:8, :], -jnp.inf))
        i += 1
    parts.append(v1[i:, :] + v2[0:1, :])
    best = _top_values(jnp.concatenate(parts, axis=0), topk)
    mx = best[0:1, :]
    z = jnp.sum(jnp.exp(best - mx), axis=0, keepdims=True)
    s1_ref[...] = s1
    s2_ref[...] = s2
    e1_ref[...] = jnp.exp(s1 - v1[0:1, :]) / z
    e2_ref[...] = jnp.exp(s2 - v2[0:1, :])
    thr_ref[...] = best[topk - 1:topk, :]


def peer_route(znt, wqt_bf16, keys):
    d, tp = znt.shape
    nh, _, nk, half = keys.shape
    tm = _pick(tp, (512, 256, 128))
    sc = pl.BlockSpec((None, nk, tm), lambda i, h: (h, 0, i))
    big = jax.ShapeDtypeStruct((nh, nk, tp), F32)
    return pl.pallas_call(
        functools.partial(_peer_route_body, half=half, topk=PEER_TOPK),
        grid=(tp // tm, nh),
        in_specs=[
            pl.BlockSpec((2 * half, d), lambda i, h: (h, 0)),
            pl.BlockSpec((d, tm), lambda i, h: (0, i)),
            pl.BlockSpec((None, 2, nk, half), lambda i, h: (h, 0, 0, 0)),
        ],
        out_specs=[sc, sc, sc, sc, pl.BlockSpec((None, 1, tm), lambda i, h: (h, 0, i))],
        out_shape=[big, big, big, big, jax.ShapeDtypeStruct((nh, 1, tp), F32)],
        compiler_params=_cparams(("parallel", "arbitrary")),
        name="peer_route",
    )(wqt_bf16, znt, keys)


def _gelu(x):
    return 0.5 * x * (1.0 + lax.erf(x * (2.0 ** -0.5)))


def _peer_expert_body(zn_ref, u_ref, vt_ref, s1_ref, s2_ref, e1_ref, e2_ref, thr_ref, o_ref, a0_ref, a1_ref,
                      hid0_ref, hid1_ref, *, nk, nh, nj, split):
    j = pl.program_id(1)
    te = u_ref.shape[0]
    d = vt_ref.shape[0]
    nq = te // nk
    sub = nk // split
    rows = d // (nq * split)

    def project(q, hid_w):
        hid_w[q * nk:(q + 1) * nk, :] = _dot(u_ref[q * nk:(q + 1) * nk, :], zn_ref[...])

    def activate(q, r, hid_r, a_w):
        a = (j - 1) * nq + q
        rs = slice(r * sub, (r + 1) * sub)
        gate = None
        for h in range(nh):
            cand = s1_ref[h, pl.ds(a, 1), :] + s2_ref[h, rs, :]
            gh = jnp.where(cand >= thr_ref[h], e1_ref[h, pl.ds(a, 1), :] * e2_ref[h, rs, :], 0.0)
            gate = gh if gate is None else gate + gh
        es = slice(q * nk + r * sub, q * nk + (r + 1) * sub)
        a_w[es, :] = (gate * _gelu(hid_r[es, :])).astype(BF16)

    def apply(p, a_r):
        sl = slice(p * rows, (p + 1) * rows)
        o_ref[sl, :] += _dot(vt_ref[sl, :], a_r[...])

    def step(do_project, do_activate, do_apply, par):
        hid_w, hid_r = (hid0_ref, hid1_ref) if par == 0 else (hid1_ref, hid0_ref)
        a_w, a_r = (a1_ref, a0_ref) if par == 0 else (a0_ref, a1_ref)
        for q in range(nq):
            if do_project:
                project(q, hid_w)
            for r in range(split):
                if do_activate:
                    activate(q, r, hid_r, a_w)
                if do_apply:
                    apply(q * split + r, a_r)

    @pl.when(j == 0)
    def _():
        o_ref[...] = jnp.zeros_like(o_ref)
        step(True, False, False, 0)

    @pl.when(j == 1)
    def _():
        step(True, True, False, 1)

    for par in range(2):
        @pl.when((j >= 2) & (j < nj) & (lax.rem(j, 2) == par))
        def _(par=par):
            step(True, True, True, par)

    @pl.when(j == nj)
    def _():
        step(False, True, True, nj % 2)

    @pl.when(j == nj + 1)
    def _():
        step(False, False, True, (nj + 1) % 2)


def peer_experts(znt, u_bf16, vt_bf16, s1, s2, e1, e2, thr):
    d, tp = znt.shape
    ne = u_bf16.shape[0]
    nh, nk, _ = s1.shape
    tm = _pick(tp, (512, 256, 128))
    te = _pick(ne, (512, 256, 128))
    nj = ne // te
    assert nj >= 2
    once = dict(pipeline_mode=pl.Buffered(1))
    aux = pl.BlockSpec((nh, nk, tm), lambda i, j: (0, 0, i), **once)
    return pl.pallas_call(
        functools.partial(_peer_expert_body, nk=nk, nh=nh, nj=nj, split=PEER_SPLIT),
        grid=(tp // tm, nj + 2),
        in_specs=[
            pl.BlockSpec((d, tm), lambda i, j: (0, i), **once),
            pl.BlockSpec((te, d), lambda i, j: (jnp.minimum(j, nj - 1), 0)),
            pl.BlockSpec((d, te), lambda i, j: (0, jnp.clip(j - 2, 0, nj - 1))),
            aux, aux, aux, aux,
            pl.BlockSpec((nh, 1, tm), lambda i, j: (0, 0, i), **once),
        ],
        out_specs=pl.BlockSpec((d, tm), lambda i, j: (0, i)),
        out_shape=jax.ShapeDtypeStruct((d, tp), F32),
        scratch_shapes=[pltpu.VMEM((te, tm), BF16), pltpu.VMEM((te, tm), BF16),
                        pltpu.VMEM((te, tm), F32), pltpu.VMEM((te, tm), F32)],
        compiler_params=_cparams(("parallel", "arbitrary")),
        name="peer_experts",
    )(znt, u_bf16, vt_bf16, s1, s2, e1, e2, thr)


def _residual_t_body(h_ref, yt_ref, o_ref):
    o_ref[...] = h_ref[...] + yt_ref[...].T


def residual_add_transposed(h, yt):
    tp, d = h.shape
    tm = _pick(tp, (512, 256, 128))
    return pl.pallas_call(
        _residual_t_body,
        grid=(tp // tm,),
        in_specs=[pl.BlockSpec((tm, d), lambda i: (i, 0)), pl.BlockSpec((d, tm), lambda i: (0, i))],
        out_specs=pl.BlockSpec((tm, d), lambda i: (i, 0)),
        out_shape=jax.ShapeDtypeStruct((tp, d), F32),
        compiler_params=_cparams(("parallel",)),
        name="peer_residual",
    )(h, yt)


def _final_norm_body(h_ref, g_ref, o_ref):
    x = h_ref[...]
    ms = jnp.mean(x * x, axis=-1, keepdims=True)
    o_ref[...] = x * lax.rsqrt(ms + NORM_EPS) * g_ref[...]


def final_norm(h, gain, batch, lp):
    tp, d = h.shape
    nb = lp // SWA_BLOCK
    seq = lp - SWA_BLOCK
    out = pl.pallas_call(
        _final_norm_body,
        grid=(batch, nb - 1),
        in_specs=[pl.BlockSpec((SWA_BLOCK, d), lambda b, n: (b * nb + n + 1, 0)),
                  pl.BlockSpec((1, d), lambda b, n: (0, 0))],
        out_specs=pl.BlockSpec((SWA_BLOCK, d), lambda b, n: (b * (nb - 1) + n, 0)),
        out_shape=jax.ShapeDtypeStruct((batch * seq, d), F32),
        compiler_params=_cparams(("parallel", "parallel")),
        name="final_norm",
    )(h, gain.reshape(1, d))
    return out.reshape(batch, seq, d)


def _rope_tables(lp, width, rot_heads):
    pos = jnp.maximum(jnp.arange(lp, dtype=F32) - PAD_FRONT, 0.0)
    inv_freq = ROPE_THETA ** (-jnp.arange(0, ROPE_DIMS, 2, dtype=F32) / ROPE_DIMS)
    ang = pos[:, None] * inv_freq[None, :]
    cos, sin = jnp.cos(ang), jnp.sin(ang)
    half = ROPE_DIMS // 2
    rest = HEAD_DIM - ROPE_DIMS
    one = jnp.ones((lp, rest), F32)
    zero = jnp.zeros((lp, rest), F32)
    zh = jnp.zeros((lp, half), F32)
    c_head = jnp.concatenate([cos, cos, one], axis=1)
    a_head = jnp.concatenate([-sin, zh, zero], axis=1)
    b_head = jnp.concatenate([zh, sin, zero], axis=1)
    n_heads = width // HEAD_DIM
    idn = jnp.ones((lp, HEAD_DIM), F32)
    zer = jnp.zeros((lp, HEAD_DIM), F32)
    build = lambda rot, flat: jnp.concatenate([rot if i < rot_heads else flat for i in range(n_heads)], axis=1)
    return build(c_head, idn), build(a_head, zer), build(b_head, zer)


def _layout(d, rc, sw, kvw, hw):
    qb = sw * HEAD_DIM // kvw
    kvb = _roundup(2 * kvw, LANES)
    hg_off = _roundup(rc, hw)
    q_off = _roundup(hg_off + 4 * hw, qb)
    kv_off = _roundup(q_off + sw, kvb)
    total = _roundup(kv_off + kvb, 512)
    return q_off, kv_off, kvb, hg_off, total


def kernel(x, meta_tokens, norm_mix, norm_ffn, norm_final, w_in, w_out, rwkv_mu, rwkv_w0, rwkv_w2, rwkv_a0,
           rwkv_a2, rwkv_g2, rwkv_k_k, rwkv_k_a, rwkv_r_k, rwkv_lnx_w, rwkv_lnx_b, swa_sinks, hgrn_lb, hgrn_norm,
           peer_wq, peer_keys, peer_u, peer_v):
    batch, seq, d = x.shape
    depth = w_in.shape[0]
    lp = seq + SWA_BLOCK
    rw = rwkv_w0.shape[1]
    rc = rwkv_mu.shape[1]
    n_q = swa_sinks.shape[1]
    sw = n_q * HEAD_DIM
    hw = hgrn_lb.shape[1]
    kvw = (w_in.shape[2] - rc - sw - 4 * hw) // 2
    n_kv = kvw // HEAD_DIM
    q_off, kv_off, kvb, hg_off, n_tot = _layout(d, rc, sw, kvw, hw)

    pad = jnp.zeros((batch, PAD_FRONT, d), x.dtype)
    meta = jnp.broadcast_to(meta_tokens.astype(x.dtype)[None], (batch, N_META, d))
    h = jnp.concatenate([pad, meta, x], axis=1).reshape(batch * lp, d)

    q_tabs = _rope_tables(lp, sw // n_kv, sw // n_kv // HEAD_DIM)
    kv_tabs = _rope_tables(lp, kvb, n_kv)
    lb_all = jnp.cumsum(jax.nn.softmax(hgrn_lb.astype(F32), axis=0), axis=0)
    lb_all = lb_all - lb_all[0:1]

    for l in range(depth):
        w = w_in[l]
        zc = lambda n: jnp.zeros((d, n), w.dtype)
        w_re = jnp.concatenate([
            w[:, :rc], zc(hg_off - rc),
            w[:, rc + sw + 2 * kvw:], zc(q_off - hg_off - 4 * hw),
            w[:, rc:rc + sw], zc(kv_off - q_off - sw),
            w[:, rc + sw:rc + sw + 2 * kvw], zc(n_tot - kv_off - 2 * kvw)], axis=1).astype(BF16)
        proj = norm_matmul(h, norm_mix[l], w_re, lp)

        rp = dict(mu=rwkv_mu[l], w0=rwkv_w0[l], w2=rwkv_w2[l], a0=rwkv_a0[l], a2=rwkv_a2[l], g2=rwkv_g2[l],
                  k_k=rwkv_k_k[l], k_a=rwkv_k_a[l], r_k=rwkv_r_k[l], lnx_w=rwkv_lnx_w[l], lnx_b=rwkv_lnx_b[l])
        y_a = rwkv_scan(*rwkv_prep(proj, lp, rp), batch, lp)

        kv = swa_kv_rope(proj, kv_off, kvb, kv_tabs, batch, lp)
        kvh = jnp.stack([jnp.concatenate([kv[:, i * HEAD_DIM:(i + 1) * HEAD_DIM],
                                          kv[:, kvw + i * HEAD_DIM:kvw + (i + 1) * HEAD_DIM]], axis=1)
                         for i in range(n_kv)])
        y_b = swa_attention(proj, q_off, kvh, swa_sinks[l], q_tabs, batch, lp, n_q, n_kv)

        y_c = hgrn_mix(proj, hg_off, lb_all[l], hgrn_norm[l], batch, lp)

        h = out_proj(h, y_a, y_b, y_c, w_out[l].astype(BF16))

        znt = norm_transposed(h, norm_ffn[l])
        s1, s2, e1, e2, thr = peer_route(znt, peer_wq[l].T.astype(BF16), peer_keys[l])
        yt = peer_experts(znt, peer_u[l].astype(BF16), peer_v[l].T.astype(BF16), s1, s2, e1, e2, thr)
        h = residual_add_transposed(h, yt)

    return final_norm(h, norm_final, batch, lp)
```

```python
import functools

import jax
import jax.numpy as jnp
from jax import lax
from jax.experimental import pallas as pl
from jax.experimental.pallas import tpu as pltpu

F32 = jnp.float32
BF16 = jnp.bfloat16
HI = lax.Precision.HIGHEST

N_META = 16
HEAD_DIM = 64
NORM_EPS = 1e-5
NEG_INF = -1e30
RWKV_GN_EPS = 64e-5
SWA_BLOCK = 128
PAD_FRONT = SWA_BLOCK - N_META
ROPE_THETA = 500000.0
ROPE_DIMS = HEAD_DIM // 4
HGRN_HEAD_DIM = 128
CHUNK = 64
PEER_TOPK = 16
LANES = 128
MXU_WIDTH = 256
VMEM_LIMIT_MB = 56
RWKV_HEADS_PER_STEP = 24
SWA_MODE = "bf16"
PEER_SPLIT = 4
HGRN_HEADS_PER_STEP = 8
HGRN_MODE = "bf16"
HGRN_SUB = 16
RWKV_SCAN_MODE = "bf16"


def _cparams(sem, vmem_mb=VMEM_LIMIT_MB):
    return pltpu.CompilerParams(dimension_semantics=sem, vmem_limit_bytes=vmem_mb * 1024 * 1024)


def _pick(n, cands):
    for c in cands:
        if n % c == 0:
            return c
    raise ValueError(f"no tile in {cands} divides {n}")


def _roundup(n, m):
    return -(-n // m) * m


def _dot(a, b, prec=None):
    return jnp.dot(a, b, preferred_element_type=F32, precision=prec)


def _dot_nt(a, b, prec=None):
    return lax.dot_general(a, b, (((1,), (1,)), ((), ())), preferred_element_type=F32, precision=prec)


def _dot_tn(a, b, prec=None):
    return lax.dot_general(a, b, (((0,), (0,)), ((), ())), preferred_element_type=F32, precision=prec)


def _row_in_batch(i, tm, lp):
    pos0 = lax.rem(i * tm, lp)
    rb = pos0 + lax.broadcasted_iota(jnp.int32, (tm, 1), 0)
    return jnp.where(rb >= lp, rb - lp, rb)


def _sigmoid(x):
    return 1.0 / (1.0 + jnp.exp(-x))


def _norm_mm_body(x_ref, g_ref, w_ref, o_ref, z_ref, *, lp, tm):
    i = pl.program_id(0)

    @pl.when(pl.program_id(1) == 0)
    def _():
        x = x_ref[...]
        ms = jnp.mean(x * x, axis=-1, keepdims=True)
        y = x * lax.rsqrt(ms + NORM_EPS) * g_ref[...]
        z_ref[...] = jnp.where(_row_in_batch(i, tm, lp) >= PAD_FRONT, y, 0.0).astype(BF16)

    o_ref[...] = _dot(z_ref[...], w_ref[...])


def _col_tiles(w, tn):
    k, n = w.shape
    return w.reshape(k, n // tn, tn).transpose(1, 0, 2)


def norm_matmul(h, gain, w_bf16, lp):
    tp, d = h.shape
    n = w_bf16.shape[1]
    tm = _pick(tp, (512, 256, 128))
    tn = _pick(n, (1024, 512, 256, 128))
    w_bf16 = _col_tiles(w_bf16, tn)
    return pl.pallas_call(
        functools.partial(_norm_mm_body, lp=lp, tm=tm),
        grid=(tp // tm, n // tn),
        in_specs=[
            pl.BlockSpec((tm, d), lambda i, j: (i, 0)),
            pl.BlockSpec((1, d), lambda i, j: (0, 0)),
            pl.BlockSpec((None, d, tn), lambda i, j: (j, 0, 0)),
        ],
        out_specs=pl.BlockSpec((tm, tn), lambda i, j: (i, j)),
        out_shape=jax.ShapeDtypeStruct((tp, n), F32),
        scratch_shapes=[pltpu.VMEM((tm, d), BF16)],
        compiler_params=_cparams(("parallel", "arbitrary")),
        name="norm_in_proj",
    )(h, gain.reshape(1, d), w_bf16)


def _mm(a, b, dims, mode):
    if mode == "bf16":
        return lax.dot_general(a.astype(BF16), b.astype(BF16), (dims, ((), ())), preferred_element_type=F32)
    return lax.dot_general(a, b, (dims, ((), ())), preferred_element_type=F32, precision=HI)


_NN = ((1,), (0,))
_NT = ((1,), (1,))
_TN = ((0,), (0,))


def _split3(x):
    hi = x.astype(BF16)
    r1 = x - hi.astype(F32)
    mid = r1.astype(BF16)
    lo = (r1 - mid.astype(F32)).astype(BF16)
    return hi, mid, lo


def _select_sum(sel, x, x_on_left=False):
    sel = sel.astype(BF16)
    terms = [_dot(t, sel) if x_on_left else _dot(sel, t) for t in _split3(x)]
    return (terms[2] + terms[1]) + terms[0]


def _head_sums(x):
    shift = HEAD_DIM.bit_length() - 1
    er = lax.broadcasted_iota(jnp.int32, (LANES, LANES), 0) >> shift
    ec = lax.broadcasted_iota(jnp.int32, (LANES, LANES), 1) >> shift
    ones_bd = jnp.where(er == ec, 1.0, 0.0)
    return jnp.concatenate(
        [_select_sum(ones_bd, x[:, s * LANES:(s + 1) * LANES], x_on_left=True)
         for s in range(x.shape[1] // LANES)], axis=1)


def _rwkv_prep_body(pa_ref, prev_ref, mu_ref, w0_ref, w2_ref, a0_ref, a2_ref, g2_ref, kk_ref, ka_ref, rk_ref,
                    lnw_ref, lnb_ref, out_ref, post_ref, dl_ref, *, lp, tm, rw, rd, ri, rg, c):
    i = pl.program_id(0)
    x = pa_ref[...]
    rowi = lax.broadcasted_iota(jnp.int32, (tm, 1), 0)
    prev_row = jnp.where(lax.rem(i * tm, lp) == 0, 0.0, prev_ref[7:8, :])
    prev = jnp.where(rowi == 0, prev_row, pltpu.roll(x, 1, 0))
    xs = x + (prev - x) * mu_ref[...]
    r = xs[:, 0:rw]
    k = xs[:, rw:2 * rw]
    v = xs[:, 2 * rw:3 * rw]
    wl = xs[:, 3 * rw:3 * rw + rd]
    al = xs[:, 3 * rw + rd:3 * rw + rd + ri]
    gl = xs[:, 3 * rw + rd + ri:3 * rw + rd + ri + rg]
    wv = w0_ref[...] + _dot(jnp.tanh(wl), w2_ref[...], HI)
    nwv = -wv
    softplus = jnp.maximum(nwv, 0.0) + jnp.log(1.0 + jnp.exp(-jnp.abs(nwv)))
    logw = -jnp.exp(-softplus - 0.5)
    alr = _sigmoid(a0_ref[...] + _dot(al, a2_ref[...], HI))
    g = _dot(_sigmoid(gl), g2_ref[...], HI)
    kkr = k * kk_ref[...]
    kk = kkr / jnp.maximum(jnp.sqrt(_head_sums(kkr * kkr)), 1e-12)
    k2 = k * (1.0 + (alr - 1.0) * ka_ref[...])
    shift = c.bit_length() - 1
    tr = lax.broadcasted_iota(jnp.int32, (tm, tm), 0)
    tc = lax.broadcasted_iota(jnp.int32, (tm, tm), 1)
    ltri = jnp.where(tc <= tr, jnp.where((tr >> shift) == (tc >> shift), 1.0, 0.0), 0.0)
    cum = _select_sum(ltri, logw)
    tot = jnp.concatenate(
        [jnp.broadcast_to(cum[(q + 1) * c - 1:(q + 1) * c, :], (c, rw)) for q in range(tm // c)], axis=0)
    p_inv = jnp.exp(-cum)
    p_rem = jnp.exp(tot - cum)
    kb = kk * alr
    od = out_ref.dtype
    out_ref[0] = (-kk * jnp.exp(cum - logw)).astype(od)
    out_ref[1] = (r * jnp.exp(cum)).astype(od)
    out_ref[2] = (kb * p_inv).astype(od)
    out_ref[3] = (k2 * p_inv).astype(od)
    out_ref[4] = v.astype(od)
    out_ref[5] = (kb * p_rem).astype(od)
    out_ref[6] = (k2 * p_rem).astype(od)
    post_ref[0] = lnw_ref[...] * g
    post_ref[1] = (lnb_ref[...] + _head_sums(r * k2 * rk_ref[...]) * v) * g
    for q in range(tm // c):
        dl_ref[q] = jnp.exp(tot[q * c:q * c + 1, :])


def rwkv_prep(proj, lp, p):
    tp = proj.shape[0]
    rw = p["w0"].shape[0]
    rd, ri, rg = p["w2"].shape[0], p["a2"].shape[0], p["g2"].shape[0]
    rc = 3 * rw + rd + ri + rg
    tm = 128
    c = CHUNK
    row = lambda a: a.reshape(1, -1)
    full = lambda a: pl.BlockSpec(a.shape, lambda i: (0, 0))
    args = [row(p["mu"]), row(p["w0"]), p["w2"], row(p["a0"]), p["a2"], p["g2"], row(p["k_k"]), row(p["k_a"]),
            row(p["r_k"]), row(p["lnx_w"]), row(p["lnx_b"])]
    return pl.pallas_call(
        functools.partial(_rwkv_prep_body, lp=lp, tm=tm, rw=rw, rd=rd, ri=ri, rg=rg, c=c),
        grid=(tp // tm,),
        in_specs=[
            pl.BlockSpec((tm, rc), lambda i: (i, 0)),
            pl.BlockSpec((8, rc), lambda i: (jnp.maximum(i * (tm // 8) - 1, 0), 0)),
        ] + [full(a) for a in args],
        out_specs=[pl.BlockSpec((7, tm, rw), lambda i: (0, i, 0)),
                   pl.BlockSpec((2, tm, rw), lambda i: (0, i, 0)),
                   pl.BlockSpec((tm // c, 1, rw), lambda i: (i, 0, 0))],
        out_shape=[jax.ShapeDtypeStruct((7, tp, rw), BF16 if RWKV_SCAN_MODE == "bf16" else F32),
                   jax.ShapeDtypeStruct((2, tp, rw), F32), jax.ShapeDtypeStruct((tp // c, 1, rw), F32)],
        compiler_params=_cparams(("parallel",)),
        name="rwkv_prep",
    )(proj, proj, *args)


def _rwkv_scan_body(x_ref, post_ref, dl_ref, o_ref, s_ref, *, heads, c, mode):
    @pl.when(pl.program_id(2) == 0)
    def _():
        s_ref[...] = jnp.zeros_like(s_ref)

    ri = lax.broadcasted_iota(jnp.int32, (c, c), 0)
    ci = lax.broadcasted_iota(jnp.int32, (c, c), 1)
    strict = ci < ri
    incl = ci <= ri
    hs = range(heads)
    ld = lambda q, h: x_ref[q, :, h * HEAD_DIM:(h + 1) * HEAD_DIM]
    cat0 = lambda a, b: jnp.concatenate([a, b], axis=0)
    cat1 = lambda a, b: jnp.concatenate([a, b], axis=1)
    ar = [cat0(ld(0, h), ld(1, h)) for h in hs]
    m = [_mm(ar[h], cat0(ld(2, h), ld(3, h)), _NT, mode) for h in hs]
    s0 = [s_ref[h] for h in hs]
    ars = [_mm(ar[h], s0[h], _NT, mode) for h in hs]
    v = [ld(4, h) for h in hs]
    nk = [jnp.where(strict, m[h][:c, :c], 0.0) for h in hs]
    m_ak = [jnp.where(strict, m[h][:c, c:], 0.0) for h in hs]
    m_r = [cat1(jnp.where(incl, m[h][c:, :c], 0.0), jnp.where(incl, m[h][c:, c:], 0.0)) for h in hs]
    u = [ars[h][:c] + _mm(m_ak[h], v[h], _NN, mode) for h in hs]
    steps = c.bit_length() - 1
    for it in range(steps):
        if it + 1 < steps:
            both = [_mm(nk[h], cat1(u[h], nk[h]), _NN, mode) for h in hs]
            u = [u[h] + both[h][:, :c] for h in hs]
            nk = [both[h][:, c:] for h in hs]
        else:
            u = [u[h] + _mm(nk[h], u[h], _NN, mode) for h in hs]
    uv = [cat0(u[h].astype(v[h].dtype), v[h]) for h in hs]
    y = [ars[h][c:] + _mm(m_r[h], uv[h], _NN, mode) for h in hs]
    for h in hs:
        dl = dl_ref[0, :, h * HEAD_DIM:(h + 1) * HEAD_DIM]
        s_ref[h] = s0[h] * dl + _mm(uv[h], cat0(ld(5, h), ld(6, h)), _TN, mode)
    outs = []
    for h in hs:
        mean = jnp.mean(y[h], axis=-1, keepdims=True)
        yc = y[h] - mean
        var = jnp.mean(yc * yc, axis=-1, keepdims=True)
        hl = slice(h * HEAD_DIM, (h + 1) * HEAD_DIM)
        outs.append(yc * lax.rsqrt(var + RWKV_GN_EPS) * post_ref[0, :, hl] + post_ref[1, :, hl])
    o_ref[...] = jnp.concatenate(outs, axis=1).astype(o_ref.dtype)


def rwkv_scan(xs, post, dl, batch, lp):
    _, tp, rw = xs.shape
    c = CHUNK
    nc = lp // c
    heads = _pick(rw // HEAD_DIM, (RWKV_HEADS_PER_STEP, 2))
    gw = heads * HEAD_DIM
    return pl.pallas_call(
        functools.partial(_rwkv_scan_body, heads=heads, c=c, mode=RWKV_SCAN_MODE),
        grid=(batch, rw // gw, nc),
        in_specs=[pl.BlockSpec((7, c, gw), lambda b, hg, ch: (0, b * nc + ch, hg)),
                  pl.BlockSpec((2, c, gw), lambda b, hg, ch: (0, b * nc + ch, hg)),
                  pl.BlockSpec((1, 1, gw), lambda b, hg, ch: (b * nc + ch, 0, hg))],
        out_specs=pl.BlockSpec((c, gw), lambda b, hg, ch: (b * nc + ch, hg)),
        out_shape=jax.ShapeDtypeStruct((tp, rw), BF16),
        scratch_shapes=[pltpu.VMEM((heads, HEAD_DIM, HEAD_DIM), F32)],
        compiler_params=_cparams(("parallel", "parallel", "arbitrary")),
        name="rwkv_scan",
    )(xs, post, dl)


def _rope(x, cos, sa, sb):
    parts = []
    for s in range(x.shape[1] // LANES):
        sl = slice(s * LANES, (s + 1) * LANES)
        xs = x[:, sl]
        parts.append(xs * cos[:, sl] + pltpu.roll(xs, LANES - ROPE_DIMS // 2, 1) * sa[:, sl]
                     + pltpu.roll(xs, ROPE_DIMS // 2, 1) * sb[:, sl])
    return parts[0] if len(parts) == 1 else jnp.concatenate(parts, axis=1)


def _kv_rope_body(kv_ref, cos_ref, sa_ref, sb_ref, o_ref):
    o_ref[...] = _rope(kv_ref[...], cos_ref[...], sa_ref[...], sb_ref[...])


def swa_kv_rope(proj, kv_off, kvb, tabs, batch, lp):
    tp = proj.shape[0]
    nb = lp // SWA_BLOCK
    tab = pl.BlockSpec((SWA_BLOCK, kvb), lambda b, n: (n, 0))
    return pl.pallas_call(
        _kv_rope_body,
        grid=(batch, nb),
        in_specs=[pl.BlockSpec((SWA_BLOCK, kvb), lambda b, n: (b * nb + n, kv_off // kvb)), tab, tab, tab],
        out_specs=pl.BlockSpec((SWA_BLOCK, kvb), lambda b, n: (b * nb + n, 0)),
        out_shape=jax.ShapeDtypeStruct((tp, kvb), F32),
        compiler_params=_cparams(("parallel", "parallel")),
        name="swa_kv_rope",
    )(proj, *tabs)


def _swa_body(sink_ref, q_ref, kc_ref, kp_ref, km_ref, cos_ref, sa_ref, sb_ref, o_ref, *, group, kvw):
    n = pl.program_id(1)
    hk = pl.program_id(2)
    blk = SWA_BLOCK
    mode = SWA_MODE
    q = _rope(q_ref[...], cos_ref[...], sa_ref[...], sb_ref[...]) * (HEAD_DIM ** -0.5)
    qs = jnp.concatenate([q[:, gi * HEAD_DIM:(gi + 1) * HEAD_DIM] for gi in range(group)], axis=0)
    rows = group * blk
    qi = lax.broadcasted_iota(jnp.int32, (rows, blk), 0) & (blk - 1)
    kj = lax.broadcasted_iota(jnp.int32, (rows, blk), 1)
    far = 1 << 20
    cur_ok = kj + jnp.where(n >= 1, 0, far) <= qi
    prev_ok = kj > qi + jnp.where(n >= 2, 0, far)
    mi = lax.broadcasted_iota(jnp.int32, (rows, N_META), 0) & (blk - 1)
    mj = lax.broadcasted_iota(jnp.int32, (rows, N_META), 1)
    meta_ok = mj <= mi - PAD_FRONT + jnp.where(n >= 1, far, 0)
    sink = jnp.concatenate([jnp.full((blk, 1), sink_ref[hk * group + gi], F32) for gi in range(group)], axis=0)
    s_c = jnp.where(cur_ok, _mm(qs, kc_ref[:, 0:HEAD_DIM], _NT, mode), NEG_INF)
    s_p = jnp.where(prev_ok, _mm(qs, kp_ref[:, 0:HEAD_DIM], _NT, mode), NEG_INF)
    s_m = jnp.where(meta_ok, _mm(qs, km_ref[:, 0:HEAD_DIM], _NT, mode), NEG_INF)
    mx = jnp.maximum(jnp.maximum(jnp.max(s_c, axis=-1, keepdims=True), jnp.max(s_p, axis=-1, keepdims=True)),
                     jnp.maximum(jnp.max(s_m, axis=-1, keepdims=True), sink))
    p_c = jnp.exp(s_c - mx)
    p_p = jnp.exp(s_p - mx)
    p_m = jnp.exp(s_m - mx)
    den = (jnp.sum(p_c, axis=-1, keepdims=True) + jnp.sum(p_p, axis=-1, keepdims=True)
           + jnp.sum(p_m, axis=-1, keepdims=True) + jnp.exp(sink - mx))
    acc = (_mm(p_c, kc_ref[:, HEAD_DIM:2 * HEAD_DIM], _NN, mode) + _mm(p_p, kp_ref[:, HEAD_DIM:2 * HEAD_DIM], _NN, mode)
           + _mm(p_m, km_ref[:, HEAD_DIM:2 * HEAD_DIM], _NN, mode))
    out = acc / den
    o_ref[...] = jnp.concatenate([out[gi * blk:(gi + 1) * blk, :] for gi in range(group)], axis=1).astype(o_ref.dtype)


def swa_attention(proj, q_off, kvh, sinks, qtabs, batch, lp, n_q_heads, n_kv_heads):
    tp = proj.shape[0]
    nb = lp // SWA_BLOCK
    group = n_q_heads // n_kv_heads
    qb = group * HEAD_DIM
    sw = n_q_heads * HEAD_DIM
    tab = pl.BlockSpec((SWA_BLOCK, qb), lambda b, n, hk: (n, 0))
    kv_spec = lambda f: pl.BlockSpec((None, SWA_BLOCK, LANES), f)
    return pl.pallas_call(
        functools.partial(_swa_body, group=group, kvw=n_kv_heads * HEAD_DIM),
        grid=(batch, nb, n_kv_heads),
        in_specs=[
            pl.BlockSpec(memory_space=pltpu.SMEM),
            pl.BlockSpec((SWA_BLOCK, qb), lambda b, n, hk: (b * nb + n, q_off // qb + hk)),
            kv_spec(lambda b, n, hk: (hk, b * nb + n, 0)),
            kv_spec(lambda b, n, hk: (hk, b * nb + jnp.maximum(n - 1, 0), 0)),
            pl.BlockSpec((None, N_META, LANES),
                         lambda b, n, hk: (hk, b * (lp // N_META) + PAD_FRONT // N_META, 0)),
            tab, tab, tab,
        ],
        out_specs=pl.BlockSpec((SWA_BLOCK, qb), lambda b, n, hk: (b * nb + n, hk)),
        out_shape=jax.ShapeDtypeStruct((tp, sw), BF16),
        compiler_params=_cparams(("parallel", "parallel", "parallel")),
        name="swa_attention",
    )(sinks, proj, kvh, kvh, kvh, *qtabs)


def _hgrn_body(q_ref, f_ref, i_ref, g_ref, lb_ref, gain_ref, o_ref, st_ref, *, c, heads, sub):
    @pl.when(pl.program_id(2) == 0)
    def _():
        st_ref[...] = jnp.zeros_like(st_ref)

    dk = HGRN_HEAD_DIM
    hs = range(heads)
    nb = c // sub
    ri = lax.broadcasted_iota(jnp.int32, (c, c), 0)
    ci = lax.broadcasted_iota(jnp.int32, (c, c), 1)
    incl = ci <= ri
    ltri = jnp.where(incl, 1.0, 0.0)
    cs = lax.broadcasted_iota(jnp.int32, (sub, c), 1)
    col = lambda ref, h: ref[:, h * dk:(h + 1) * dk]
    q = [col(q_ref, h) for h in hs]
    iv = [col(i_ref, h) for h in hs]
    f = [col(lb_ref, h) + (1.0 - col(lb_ref, h)) * _sigmoid(col(f_ref, h)) for h in hs]
    kx = [1.0 - f[h] for h in hs]
    gc = [_select_sum(ltri, jnp.log(f[h])) for h in hs]
    st = [st_ref[h] for h in hs]
    mode = HGRN_MODE
    o_inter = [_mm(q[h] * jnp.exp(gc[h]), st[h], _NT, mode) for h in hs]
    for h in hs:
        g_last = gc[h][c - 1:c, :]
        st_ref[h] = st[h] * jnp.exp(g_last) + _mm(iv[h], kx[h] * jnp.exp(g_last - gc[h]), _TN, mode)
    attn = []
    for h in hs:
        blocks = []
        for b in range(nb):
            rows = slice(b * sub, (b + 1) * sub)
            qb, gb = q[h][rows, :], gc[h][rows, :]
            if b == 0:
                blk = jnp.zeros((sub, c), F32)
            else:
                ref = gc[h][b * sub:b * sub + 1, :]
                k_sc = kx[h] * jnp.exp(jnp.minimum(ref - gc[h], 0.0))
                blk = _mm(qb * jnp.exp(gb - ref), k_sc, _NT, mode)
            for s in range(b * sub, (b + 1) * sub):
                dec = jnp.exp(jnp.minimum(gb - gc[h][s:s + 1, :], 0.0))
                blk = jnp.where(cs == s, jnp.sum(qb * kx[h][s:s + 1, :] * dec, axis=-1, keepdims=True), blk)
            blocks.append(blk)
        attn.append(jnp.where(incl, jnp.concatenate(blocks, axis=0), 0.0))
    outs = []
    for h in hs:
        o = _mm(attn[h], iv[h], _NN, mode) + o_inter[h]
        on = o * lax.rsqrt(jnp.mean(o * o, axis=-1, keepdims=True) + NORM_EPS) * col(gain_ref, h)
        g = col(g_ref, h)
        outs.append(on * (g * _sigmoid(g)))
    o_ref[...] = jnp.concatenate(outs, axis=1).astype(o_ref.dtype)


def hgrn_mix(proj, hg_off, lb, gain, batch, lp):
    tp = proj.shape[0]
    hw = lb.shape[0]
    dk = HGRN_HEAD_DIM
    nh = hw // dk
    c = CHUNK
    nc = lp // c
    heads = _pick(nh, (HGRN_HEADS_PER_STEP, 2, 1))
    gw = heads * dk
    blk = lambda part: pl.BlockSpec((c, gw), lambda b, h, ch: (b * nc + ch, (hg_off + part * hw) // gw + h))
    vec = pl.BlockSpec((1, gw), lambda b, h, ch: (0, h))
    return pl.pallas_call(
        functools.partial(_hgrn_body, c=c, heads=heads, sub=HGRN_SUB),
        grid=(batch, nh // heads, nc),
        in_specs=[blk(0), blk(1), blk(2), blk(3), vec, vec],
        out_specs=pl.BlockSpec((c, gw), lambda b, h, ch: (b * nc + ch, h)),
        out_shape=jax.ShapeDtypeStruct((tp, hw), BF16),
        scratch_shapes=[pltpu.VMEM((heads, dk, dk), F32)],
        compiler_params=_cparams(("parallel", "parallel", "arbitrary")),
        name="hgrn_mix",
    )(proj, proj, proj, proj, lb.reshape(1, hw), gain.reshape(1, hw))


def _out_proj_body(h_ref, ya_ref, yb_ref, yc_ref, wa_ref, wb_ref, wc_ref, o_ref):
    o_ref[...] = (h_ref[...] + _dot(ya_ref[...], wa_ref[...]) + _dot(yb_ref[...], wb_ref[...])
                  + _dot(yc_ref[...], wc_ref[...]))


def out_proj(h, ya, yb, yc, w_bf16):
    tp, d = h.shape
    wa, wb, wc = ya.shape[1], yb.shape[1], yc.shape[1]
    tm = _pick(tp, (512, 256, 128))
    tn = _pick(d, (1024, 512, 256, 128))
    return pl.pallas_call(
        _out_proj_body,
        grid=(tp // tm, d // tn),
        in_specs=[
            pl.BlockSpec((tm, tn), lambda i, j: (i, j)),
            pl.BlockSpec((tm, wa), lambda i, j: (i, 0)),
            pl.BlockSpec((tm, wb), lambda i, j: (i, 0)),
            pl.BlockSpec((tm, wc), lambda i, j: (i, 0)),
            pl.BlockSpec((None, wa, tn), lambda i, j: (j, 0, 0)),
            pl.BlockSpec((None, wb, tn), lambda i, j: (j, 0, 0)),
            pl.BlockSpec((None, wc, tn), lambda i, j: (j, 0, 0)),
        ],
        out_specs=pl.BlockSpec((tm, tn), lambda i, j: (i, j)),
        out_shape=jax.ShapeDtypeStruct((tp, d), F32),
        compiler_params=_cparams(("parallel", "parallel")),
        name="out_proj",
    )(h, ya, yb, yc, _col_tiles(w_bf16[:wa], tn), _col_tiles(w_bf16[wa:wa + wb], tn),
      _col_tiles(w_bf16[wa + wb:], tn))


def _norm_t_body(h_ref, g_ref, o_ref):
    x = h_ref[...]
    ms = jnp.mean(x * x, axis=-1, keepdims=True)
    y = x * lax.rsqrt(ms + NORM_EPS) * g_ref[...]
    o_ref[...] = y.T.astype(o_ref.dtype)


def norm_transposed(h, gain):
    tp, d = h.shape
    tm = _pick(tp, (512, 256, 128))
    return pl.pallas_call(
        _norm_t_body,
        grid=(tp // tm,),
        in_specs=[pl.BlockSpec((tm, d), lambda i: (i, 0)), pl.BlockSpec((1, d), lambda i: (0, 0))],
        out_specs=pl.BlockSpec((None, d, tm), lambda i: (i, 0, 0)),
        out_shape=jax.ShapeDtypeStruct((tp // tm, d, tm), BF16),
        compiler_params=_cparams(("parallel",)),
        name="peer_norm_t",
    )(h, gain.reshape(1, d))


def _top_values(s, k):
    rows = lax.broadcasted_iota(jnp.int32, (k, 1), 0)
    out = jnp.zeros((k, s.shape[1]), F32)
    for r in range(k):
        m = jnp.max(s, axis=0, keepdims=True)
        out = jnp.where(rows == r, m, out)
        s = jnp.where(s == m, -jnp.inf, s)
    return out


def _peer_route_body(wq_ref, zn_ref, keys_ref, s1_ref, s2_ref, e1_ref, e2_ref, thr_ref, *, half, topk):
    qt = _dot(wq_ref[...], zn_ref[...])
    s1 = _dot(keys_ref[0], qt[:half], HI)
    s2 = _dot(keys_ref[1], qt[half:], HI)
    v1 = _top_values(s1, topk)
    v2 = _top_values(s2, topk)
    sub8 = lax.broadcasted_iota(jnp.int32, (8, 1), 0)
    parts = [v1[0:1, :] + v2]
    i = 1
    while topk // (i + 1) > 1:
        n_j = topk // (i + 1)
        parts.append(jnp.where(sub8 < n_j, v1[i:i + 1, :] + v2[0:8, :], -jnp.inf))
        i += 1
    parts.append(v1[i:, :] + v2[0:1, :])
    best = _top_values(jnp.concatenate(parts, axis=0), topk)
    mx = best[0:1, :]
    z = jnp.sum(jnp.exp(best - mx), axis=0, keepdims=True)
    s1_ref[...] = s1
    s2_ref[...] = s2
    e1_ref[...] = jnp.exp(s1 - v1[0:1, :]) / z
    e2_ref[...] = jnp.exp(s2 - v2[0:1, :])
    thr_ref[...] = best[topk - 1:topk, :]


def peer_route(znt, wqt_bf16, keys):
    nt, d, tm = znt.shape
    nh, _, nk, half = keys.shape
    sc = pl.BlockSpec((None, None, nk, tm), lambda i, h: (i, h, 0, 0))
    big = jax.ShapeDtypeStruct((nt, nh, nk, tm), F32)
    return pl.pallas_call(
        functools.partial(_peer_route_body, half=half, topk=PEER_TOPK),
        grid=(nt, nh),
        in_specs=[
            pl.BlockSpec((2 * half, d), lambda i, h: (h, 0)),
            pl.BlockSpec((None, d, tm), lambda i, h: (i, 0, 0)),
            pl.BlockSpec((None, 2, nk, half), lambda i, h: (h, 0, 0, 0)),
        ],
        out_specs=[sc, sc, sc, sc, pl.BlockSpec((None, None, 1, tm), lambda i, h: (i, h, 0, 0))],
        out_shape=[big, big, big, big, jax.ShapeDtypeStruct((nt, nh, 1, tm), F32)],
        compiler_params=_cparams(("parallel", "arbitrary")),
        name="peer_route",
    )(wqt_bf16, znt, keys)


def _gelu(x):
    return 0.5 * x * (1.0 + lax.erf(x * (2.0 ** -0.5)))


def _peer_expert_body(zn_ref, u_ref, vt_ref, s1_ref, s2_ref, e1_ref, e2_ref, thr_ref, o_ref, a0_ref, a1_ref,
                      hid0_ref, hid1_ref, *, nk, nh, nj, split):
    j = pl.program_id(1)
    te = u_ref.shape[0]
    d = vt_ref.shape[0]
    nq = te // nk
    sub = nk // split
    rows = d // (nq * split)

    def project(q, hid_w):
        hid_w[q * nk:(q + 1) * nk, :] = _dot(u_ref[q * nk:(q + 1) * nk, :], zn_ref[...])

    def activate(q, r, hid_r, a_w):
        a = (j - 1) * nq + q
        rs = slice(r * sub, (r + 1) * sub)
        gate = None
        for h in range(nh):
            cand = s1_ref[h, pl.ds(a, 1), :] + s2_ref[h, rs, :]
            gh = jnp.where(cand >= thr_ref[h], e1_ref[h, pl.ds(a, 1), :] * e2_ref[h, rs, :], 0.0)
            gate = gh if gate is None else gate + gh
        es = slice(q * nk + r * sub, q * nk + (r + 1) * sub)
        a_w[es, :] = (gate * _gelu(hid_r[es, :])).astype(BF16)

    def apply(p, a_r):
        sl = slice(p * rows, (p + 1) * rows)
        o_ref[sl, :] += _dot(vt_ref[sl, :], a_r[...])

    def step(do_project, do_activate, do_apply, par):
        hid_w, hid_r = (hid0_ref, hid1_ref) if par == 0 else (hid1_ref, hid0_ref)
        a_w, a_r = (a1_ref, a0_ref) if par == 0 else (a0_ref, a1_ref)
        for q in range(nq):
            if do_project:
                project(q, hid_w)
            for r in range(split):
                if do_activate:
                    activate(q, r, hid_r, a_w)
                if do_apply:
                    apply(q * split + r, a_r)

    @pl.when(j == 0)
    def _():
        o_ref[...] = jnp.zeros_like(o_ref)
        step(True, False, False, 0)

    @pl.when(j == 1)
    def _():
        step(True, True, False, 1)

    for par in range(2):
        @pl.when((j >= 2) & (j < nj) & (lax.rem(j, 2) == par))
        def _(par=par):
            step(True, True, True, par)

    @pl.when(j == nj)
    def _():
        step(False, True, True, nj % 2)

    @pl.when(j == nj + 1)
    def _():
        step(False, False, True, (nj + 1) % 2)


def peer_experts(znt, u_bf16, v_bf16, s1, s2, e1, e2, thr):
    nt, d, tm = znt.shape
    ne = u_bf16.shape[0]
    _, nh, nk, _ = s1.shape
    te = _pick(ne, (512, 256, 128))
    nj = ne // te
    assert nj >= 2
    vt_bf16 = v_bf16.reshape(nj, te, d).transpose(0, 2, 1)
    once = dict(pipeline_mode=pl.Buffered(1))
    aux = pl.BlockSpec((None, nh, nk, tm), lambda i, j: (i, 0, 0, 0), **once)
    return pl.pallas_call(
        functools.partial(_peer_expert_body, nk=nk, nh=nh, nj=nj, split=PEER_SPLIT),
        grid=(nt, nj + 2),
        in_specs=[
            pl.BlockSpec((None, d, tm), lambda i, j: (i, 0, 0), **once),
            pl.BlockSpec((te, d), lambda i, j: (jnp.minimum(j, nj - 1), 0)),
            pl.BlockSpec((None, d, te), lambda i, j: (jnp.clip(j - 2, 0, nj - 1), 0, 0)),
            aux, aux, aux, aux,
            pl.BlockSpec((None, nh, 1, tm), lambda i, j: (i, 0, 0, 0), **once),
        ],
        out_specs=pl.BlockSpec((None, d, tm), lambda i, j: (i, 0, 0)),
        out_shape=jax.ShapeDtypeStruct((nt, d, tm), F32),
        scratch_shapes=[pltpu.VMEM((te, tm), BF16), pltpu.VMEM((te, tm), BF16),
                        pltpu.VMEM((te, tm), F32), pltpu.VMEM((te, tm), F32)],
        compiler_params=_cparams(("parallel", "arbitrary")),
        name="peer_experts",
    )(znt, u_bf16, vt_bf16, s1, s2, e1, e2, thr)


def _residual_t_body(h_ref, yt_ref, o_ref):
    o_ref[...] = h_ref[...] + yt_ref[...].T


def residual_add_transposed(h, yt):
    tp, d = h.shape
    tm = yt.shape[2]
    return pl.pallas_call(
        _residual_t_body,
        grid=(tp // tm,),
        in_specs=[pl.BlockSpec((tm, d), lambda i: (i, 0)), pl.BlockSpec((None, d, tm), lambda i: (i, 0, 0))],
        out_specs=pl.BlockSpec((tm, d), lambda i: (i, 0)),
        out_shape=jax.ShapeDtypeStruct((tp, d), F32),
        compiler_params=_cparams(("parallel",)),
        name="peer_residual",
    )(h, yt)


def _final_norm_body(h_ref, g_ref, o_ref):
    x = h_ref[...]
    ms = jnp.mean(x * x, axis=-1, keepdims=True)
    o_ref[...] = x * lax.rsqrt(ms + NORM_EPS) * g_ref[...]


def final_norm(h, gain, batch, lp):
    tp, d = h.shape
    nb = lp // SWA_BLOCK
    seq = lp - SWA_BLOCK
    out = pl.pallas_call(
        _final_norm_body,
        grid=(batch, nb - 1),
        in_specs=[pl.BlockSpec((SWA_BLOCK, d), lambda b, n: (b * nb + n + 1, 0)),
                  pl.BlockSpec((1, d), lambda b, n: (0, 0))],
        out_specs=pl.BlockSpec((SWA_BLOCK, d), lambda b, n: (b * (nb - 1) + n, 0)),
        out_shape=jax.ShapeDtypeStruct((batch * seq, d), F32),
        compiler_params=_cparams(("parallel", "parallel")),
        name="final_norm",
    )(h, gain.reshape(1, d))
    return out.reshape(batch, seq, d)


def _rope_tables(lp, width, rot_heads):
    pos = jnp.maximum(jnp.arange(lp, dtype=F32) - PAD_FRONT, 0.0)
    inv_freq = ROPE_THETA ** (-jnp.arange(0, ROPE_DIMS, 2, dtype=F32) / ROPE_DIMS)
    ang = pos[:, None] * inv_freq[None, :]
    cos, sin = jnp.cos(ang), jnp.sin(ang)
    half = ROPE_DIMS // 2
    rest = HEAD_DIM - ROPE_DIMS
    one = jnp.ones((lp, rest), F32)
    zero = jnp.zeros((lp, rest), F32)
    zh = jnp.zeros((lp, half), F32)
    c_head = jnp.concatenate([cos, cos, one], axis=1)
    a_head = jnp.concatenate([-sin, zh, zero], axis=1)
    b_head = jnp.concatenate([zh, sin, zero], axis=1)
    n_heads = width // HEAD_DIM
    idn = jnp.ones((lp, HEAD_DIM), F32)
    zer = jnp.zeros((lp, HEAD_DIM), F32)
    build = lambda rot, flat: jnp.concatenate([rot if i < rot_heads else flat for i in range(n_heads)], axis=1)
    return build(c_head, idn), build(a_head, zer), build(b_head, zer)


def _layout(d, rc, sw, kvw, hw):
    qb = sw * HEAD_DIM // kvw
    kvb = _roundup(2 * kvw, LANES)
    hg_off = _roundup(rc, hw)
    q_off = _roundup(hg_off + 4 * hw, qb)
    kv_off = _roundup(q_off + sw, kvb)
    total = _roundup(kv_off + kvb, 512)
    return q_off, kv_off, kvb, hg_off, total


def kernel(x, meta_tokens, norm_mix, norm_ffn, norm_final, w_in, w_out, rwkv_mu, rwkv_w0, rwkv_w2, rwkv_a0,
           rwkv_a2, rwkv_g2, rwkv_k_k, rwkv_k_a, rwkv_r_k, rwkv_lnx_w, rwkv_lnx_b, swa_sinks, hgrn_lb, hgrn_norm,
           peer_wq, peer_keys, peer_u, peer_v):
    batch, seq, d = x.shape
    depth = w_in.shape[0]
    lp = seq + SWA_BLOCK
    rw = rwkv_w0.shape[1]
    rc = rwkv_mu.shape[1]
    n_q = swa_sinks.shape[1]
    sw = n_q * HEAD_DIM
    hw = hgrn_lb.shape[1]
    kvw = (w_in.shape[2] - rc - sw - 4 * hw) // 2
    n_kv = kvw // HEAD_DIM
    q_off, kv_off, kvb, hg_off, n_tot = _layout(d, rc, sw, kvw, hw)

    pad = jnp.zeros((batch, PAD_FRONT, d), x.dtype)
    meta = jnp.broadcast_to(meta_tokens.astype(x.dtype)[None], (batch, N_META, d))
    h = jnp.concatenate([pad, meta, x], axis=1).reshape(batch * lp, d)

    q_tabs = _rope_tables(lp, sw // n_kv, sw // n_kv // HEAD_DIM)
    kv_tabs = _rope_tables(lp, kvb, n_kv)
    lb_all = jnp.cumsum(jax.nn.softmax(hgrn_lb.astype(F32), axis=0), axis=0)
    lb_all = lb_all - lb_all[0:1]

    for l in range(depth):
        w = w_in[l]
        zc = lambda n: jnp.zeros((d, n), w.dtype)
        w_re = jnp.concatenate([
            w[:, :rc], zc(hg_off - rc),
            w[:, rc + sw + 2 * kvw:], zc(q_off - hg_off - 4 * hw),
            w[:, rc:rc + sw], zc(kv_off - q_off - sw),
            w[:, rc + sw:rc + sw + 2 * kvw], zc(n_tot - kv_off - 2 * kvw)], axis=1).astype(BF16)
        proj = norm_matmul(h, norm_mix[l], w_re, lp)

        rp = dict(mu=rwkv_mu[l], w0=rwkv_w0[l], w2=rwkv_w2[l], a0=rwkv_a0[l], a2=rwkv_a2[l], g2=rwkv_g2[l],
                  k_k=rwkv_k_k[l], k_a=rwkv_k_a[l], r_k=rwkv_r_k[l], lnx_w=rwkv_lnx_w[l], lnx_b=rwkv_lnx_b[l])
        y_a = rwkv_scan(*rwkv_prep(proj, lp, rp), batch, lp)

        kv = swa_kv_rope(proj, kv_off, kvb, kv_tabs, batch, lp)
        kvh = jnp.stack([jnp.concatenate([kv[:, i * HEAD_DIM:(i + 1) * HEAD_DIM],
                                          kv[:, kvw + i * HEAD_DIM:kvw + (i + 1) * HEAD_DIM]], axis=1)
                         for i in range(n_kv)])
        y_b = swa_attention(proj, q_off, kvh, swa_sinks[l], q_tabs, batch, lp, n_q, n_kv)

        y_c = hgrn_mix(proj, hg_off, lb_all[l], hgrn_norm[l], batch, lp)

        h = out_proj(h, y_a, y_b, y_c, w_out[l].astype(BF16))

        znt = norm_transposed(h, norm_ffn[l])
        s1, s2, e1, e2, thr = peer_route(znt, peer_wq[l].T.astype(BF16), peer_keys[l])
        yt = peer_experts(znt, peer_u[l].astype(BF16), peer_v[l].astype(BF16), s1, s2, e1, e2, thr)
        h = residual_add_transposed(h, yt)

    return final_norm(h, norm_final, batch, lp)
```

```python
import functools

import jax
import jax.numpy as jnp
from jax import lax
from jax.experimental import pallas as pl
from jax.experimental.pallas import tpu as pltpu

F32 = jnp.float32
BF16 = jnp.bfloat16
HI = lax.Precision.HIGHEST

N_META = 16
HEAD_DIM = 64
NORM_EPS = 1e-5
NEG_INF = -1e30
RWKV_GN_EPS = 64e-5
SWA_BLOCK = 128
PAD_FRONT = SWA_BLOCK - N_META
ROPE_THETA = 500000.0
ROPE_DIMS = HEAD_DIM // 4
HGRN_HEAD_DIM = 128
CHUNK = 64
PEER_TOPK = 16
LANES = 128
MXU_WIDTH = 256
VMEM_LIMIT_MB = 56
RWKV_HEADS_PER_STEP = 24
SWA_MODE = "bf16"
PEER_SPLIT = 4
HGRN_HEADS_PER_STEP = 8
HGRN_MODE = "bf16"
HGRN_SUB = 16
RWKV_SCAN_MODE = "bf16"


def _cparams(sem, vmem_mb=VMEM_LIMIT_MB):
    return pltpu.CompilerParams(dimension_semantics=sem, vmem_limit_bytes=vmem_mb * 1024 * 1024)


def _pick(n, cands):
    for c in cands:
        if n % c == 0:
            return c
    raise ValueError(f"no tile in {cands} divides {n}")


def _roundup(n, m):
    return -(-n // m) * m


def _dot(a, b, prec=None):
    return jnp.dot(a, b, preferred_element_type=F32, precision=prec)


def _dot_nt(a, b, prec=None):
    return lax.dot_general(a, b, (((1,), (1,)), ((), ())), preferred_element_type=F32, precision=prec)


def _dot_tn(a, b, prec=None):
    return lax.dot_general(a, b, (((0,), (0,)), ((), ())), preferred_element_type=F32, precision=prec)


def _row_in_batch(i, tm, lp):
    pos0 = lax.rem(i * tm, lp)
    rb = pos0 + lax.broadcasted_iota(jnp.int32, (tm, 1), 0)
    return jnp.where(rb >= lp, rb - lp, rb)


def _sigmoid(x):
    return 1.0 / (1.0 + jnp.exp(-x))


def _norm_mm_body(x_ref, g_ref, w_ref, o_ref, z_ref, *, lp, tm):
    i = pl.program_id(0)

    @pl.when(pl.program_id(1) == 0)
    def _():
        x = x_ref[...]
        ms = jnp.mean(x * x, axis=-1, keepdims=True)
        y = x * lax.rsqrt(ms + NORM_EPS) * g_ref[...]
        z_ref[...] = jnp.where(_row_in_batch(i, tm, lp) >= PAD_FRONT, y, 0.0).astype(BF16)

    o_ref[...] = _dot(z_ref[...], w_ref[...])


def norm_matmul(h, gain, w_bf16, lp):
    tp, d = h.shape
    n = w_bf16.shape[1]
    tm = _pick(tp, (512, 256, 128))
    tn = _pick(n, (1024, 512, 256, 128))
    return pl.pallas_call(
        functools.partial(_norm_mm_body, lp=lp, tm=tm),
        grid=(tp // tm, n // tn),
        in_specs=[
            pl.BlockSpec((tm, d), lambda i, j: (i, 0)),
            pl.BlockSpec((1, d), lambda i, j: (0, 0)),
            pl.BlockSpec((d, tn), lambda i, j: (0, j)),
        ],
        out_specs=pl.BlockSpec((tm, tn), lambda i, j: (i, j)),
        out_shape=jax.ShapeDtypeStruct((tp, n), F32),
        scratch_shapes=[pltpu.VMEM((tm, d), BF16)],
        compiler_params=_cparams(("parallel", "arbitrary")),
        name="norm_in_proj",
    )(h, gain.reshape(1, d), w_bf16)


def _mm(a, b, dims, mode):
    if mode == "bf16":
        return lax.dot_general(a.astype(BF16), b.astype(BF16), (dims, ((), ())), preferred_element_type=F32)
    return lax.dot_general(a, b, (dims, ((), ())), preferred_element_type=F32, precision=HI)


_NN = ((1,), (0,))
_NT = ((1,), (1,))
_TN = ((0,), (0,))


def _split3(x):
    hi = x.astype(BF16)
    r1 = x - hi.astype(F32)
    mid = r1.astype(BF16)
    lo = (r1 - mid.astype(F32)).astype(BF16)
    return hi, mid, lo


def _select_sum(sel, x, x_on_left=False):
    sel = sel.astype(BF16)
    terms = [_dot(t, sel) if x_on_left else _dot(sel, t) for t in _split3(x)]
    return (terms[2] + terms[1]) + terms[0]


def _head_sums(x):
    shift = HEAD_DIM.bit_length() - 1
    er = lax.broadcasted_iota(jnp.int32, (LANES, LANES), 0) >> shift
    ec = lax.broadcasted_iota(jnp.int32, (LANES, LANES), 1) >> shift
    ones_bd = jnp.where(er == ec, 1.0, 0.0)
    return jnp.concatenate(
        [_select_sum(ones_bd, x[:, s * LANES:(s + 1) * LANES], x_on_left=True)
         for s in range(x.shape[1] // LANES)], axis=1)


def _rwkv_prep_body(pa_ref, prev_ref, mu_ref, w0_ref, w2_ref, a0_ref, a2_ref, g2_ref, kk_ref, ka_ref, rk_ref,
                    lnw_ref, lnb_ref, out_ref, post_ref, dl_ref, *, lp, tm, rw, rd, ri, rg, c):
    i = pl.program_id(0)
    x = pa_ref[...]
    rowi = lax.broadcasted_iota(jnp.int32, (tm, 1), 0)
    prev_row = jnp.where(lax.rem(i * tm, lp) == 0, 0.0, prev_ref[7:8, :])
    prev = jnp.where(rowi == 0, prev_row, pltpu.roll(x, 1, 0))
    xs = x + (prev - x) * mu_ref[...]
    r = xs[:, 0:rw]
    k = xs[:, rw:2 * rw]
    v = xs[:, 2 * rw:3 * rw]
    wl = xs[:, 3 * rw:3 * rw + rd]
    al = xs[:, 3 * rw + rd:3 * rw + rd + ri]
    gl = xs[:, 3 * rw + rd + ri:3 * rw + rd + ri + rg]
    wv = w0_ref[...] + _dot(jnp.tanh(wl), w2_ref[...], HI)
    nwv = -wv
    softplus = jnp.maximum(nwv, 0.0) + jnp.log(1.0 + jnp.exp(-jnp.abs(nwv)))
    logw = -jnp.exp(-softplus - 0.5)
    alr = _sigmoid(a0_ref[...] + _dot(al, a2_ref[...], HI))
    g = _dot(_sigmoid(gl), g2_ref[...], HI)
    kkr = k * kk_ref[...]
    kk = kkr / jnp.maximum(jnp.sqrt(_head_sums(kkr * kkr)), 1e-12)
    k2 = k * (1.0 + (alr - 1.0) * ka_ref[...])
    shift = c.bit_length() - 1
    tr = lax.broadcasted_iota(jnp.int32, (tm, tm), 0)
    tc = lax.broadcasted_iota(jnp.int32, (tm, tm), 1)
    ltri = jnp.where(tc <= tr, jnp.where((tr >> shift) == (tc >> shift), 1.0, 0.0), 0.0)
    cum = _select_sum(ltri, logw)
    tot = jnp.concatenate(
        [jnp.broadcast_to(cum[(q + 1) * c - 1:(q + 1) * c, :], (c, rw)) for q in range(tm // c)], axis=0)
    p_inv = jnp.exp(-cum)
    p_rem = jnp.exp(tot - cum)
    kb = kk * alr
    od = out_ref.dtype
    out_ref[0] = (-kk * jnp.exp(cum - logw)).astype(od)
    out_ref[1] = (r * jnp.exp(cum)).astype(od)
    out_ref[2] = (kb * p_inv).astype(od)
    out_ref[3] = (k2 * p_inv).astype(od)
    out_ref[4] = v.astype(od)
    out_ref[5] = (kb * p_rem).astype(od)
    out_ref[6] = (k2 * p_rem).astype(od)
    post_ref[0] = lnw_ref[...] * g
    post_ref[1] = (lnb_ref[...] + _head_sums(r * k2 * rk_ref[...]) * v) * g
    for q in range(tm // c):
        dl_ref[q] = jnp.exp(tot[q * c:q * c + 1, :])


def rwkv_prep(proj, lp, p):
    tp = proj.shape[0]
    rw = p["w0"].shape[0]
    rd, ri, rg = p["w2"].shape[0], p["a2"].shape[0], p["g2"].shape[0]
    rc = 3 * rw + rd + ri + rg
    tm = 128
    c = CHUNK
    row = lambda a: a.reshape(1, -1)
    full = lambda a: pl.BlockSpec(a.shape, lambda i: (0, 0))
    args = [row(p["mu"]), row(p["w0"]), p["w2"], row(p["a0"]), p["a2"], p["g2"], row(p["k_k"]), row(p["k_a"]),
            row(p["r_k"]), row(p["lnx_w"]), row(p["lnx_b"])]
    return pl.pallas_call(
        functools.partial(_rwkv_prep_body, lp=lp, tm=tm, rw=rw, rd=rd, ri=ri, rg=rg, c=c),
        grid=(tp // tm,),
        in_specs=[
            pl.BlockSpec((tm, rc), lambda i: (i, 0)),
            pl.BlockSpec((8, rc), lambda i: (jnp.maximum(i * (tm // 8) - 1, 0), 0)),
        ] + [full(a) for a in args],
        out_specs=[pl.BlockSpec((7, tm, rw), lambda i: (0, i, 0)),
                   pl.BlockSpec((2, tm, rw), lambda i: (0, i, 0)),
                   pl.BlockSpec((tm // c, 1, rw), lambda i: (i, 0, 0))],
        out_shape=[jax.ShapeDtypeStruct((7, tp, rw), BF16 if RWKV_SCAN_MODE == "bf16" else F32),
                   jax.ShapeDtypeStruct((2, tp, rw), F32), jax.ShapeDtypeStruct((tp // c, 1, rw), F32)],
        compiler_params=_cparams(("parallel",)),
        name="rwkv_prep",
    )(proj, proj, *args)


def _rwkv_scan_body(x_ref, post_ref, dl_ref, o_ref, s_ref, *, heads, c, mode):
    @pl.when(pl.program_id(2) == 0)
    def _():
        s_ref[...] = jnp.zeros_like(s_ref)

    ri = lax.broadcasted_iota(jnp.int32, (c, c), 0)
    ci = lax.broadcasted_iota(jnp.int32, (c, c), 1)
    strict = ci < ri
    incl = ci <= ri
    hs = range(heads)
    ld = lambda q, h: x_ref[q, :, h * HEAD_DIM:(h + 1) * HEAD_DIM]
    cat0 = lambda a, b: jnp.concatenate([a, b], axis=0)
    cat1 = lambda a, b: jnp.concatenate([a, b], axis=1)
    ar = [cat0(ld(0, h), ld(1, h)) for h in hs]
    m = [_mm(ar[h], cat0(ld(2, h), ld(3, h)), _NT, mode) for h in hs]
    s0 = [s_ref[h] for h in hs]
    ars = [_mm(ar[h], s0[h], _NT, mode) for h in hs]
    v = [ld(4, h) for h in hs]
    nk = [jnp.where(strict, m[h][:c, :c], 0.0) for h in hs]
    m_ak = [jnp.where(strict, m[h][:c, c:], 0.0) for h in hs]
    m_r = [cat1(jnp.where(incl, m[h][c:, :c], 0.0), jnp.where(incl, m[h][c:, c:], 0.0)) for h in hs]
    u = [ars[h][:c] + _mm(m_ak[h], v[h], _NN, mode) for h in hs]
    steps = c.bit_length() - 1
    for it in range(steps):
        if it + 1 < steps:
            both = [_mm(nk[h], cat1(u[h], nk[h]), _NN, mode) for h in hs]
            u = [u[h] + both[h][:, :c] for h in hs]
            nk = [both[h][:, c:] for h in hs]
        else:
            u = [u[h] + _mm(nk[h], u[h], _NN, mode) for h in hs]
    uv = [cat0(u[h].astype(v[h].dtype), v[h]) for h in hs]
    y = [ars[h][c:] + _mm(m_r[h], uv[h], _NN, mode) for h in hs]
    for h in hs:
        dl = dl_ref[0, :, h * HEAD_DIM:(h + 1) * HEAD_DIM]
        s_ref[h] = s0[h] * dl + _mm(uv[h], cat0(ld(5, h), ld(6, h)), _TN, mode)
    outs = []
    for h in hs:
        mean = jnp.mean(y[h], axis=-1, keepdims=True)
        yc = y[h] - mean
        var = jnp.mean(yc * yc, axis=-1, keepdims=True)
        hl = slice(h * HEAD_DIM, (h + 1) * HEAD_DIM)
        outs.append(yc * lax.rsqrt(var + RWKV_GN_EPS) * post_ref[0, :, hl] + post_ref[1, :, hl])
    o_ref[...] = jnp.concatenate(outs, axis=1).astype(o_ref.dtype)


def rwkv_scan(xs, post, dl, batch, lp):
    _, tp, rw = xs.shape
    c = CHUNK
    nc = lp // c
    heads = _pick(rw // HEAD_DIM, (RWKV_HEADS_PER_STEP, 2))
    gw = heads * HEAD_DIM
    return pl.pallas_call(
        functools.partial(_rwkv_scan_body, heads=heads, c=c, mode=RWKV_SCAN_MODE),
        grid=(batch, rw // gw, nc),
        in_specs=[pl.BlockSpec((7, c, gw), lambda b, hg, ch: (0, b * nc + ch, hg)),
                  pl.BlockSpec((2, c, gw), lambda b, hg, ch: (0, b * nc + ch, hg)),
                  pl.BlockSpec((1, 1, gw), lambda b, hg, ch: (b * nc + ch, 0, hg))],
        out_specs=pl.BlockSpec((c, gw), lambda b, hg, ch: (b * nc + ch, hg)),
        out_shape=jax.ShapeDtypeStruct((tp, rw), BF16),
        scratch_shapes=[pltpu.VMEM((heads, HEAD_DIM, HEAD_DIM), F32)],
        compiler_params=_cparams(("parallel", "parallel", "arbitrary")),
        name="rwkv_scan",
    )(xs, post, dl)


def _rope(x, cos, sa, sb):
    parts = []
    for s in range(x.shape[1] // LANES):
        sl = slice(s * LANES, (s + 1) * LANES)
        xs = x[:, sl]
        parts.append(xs * cos[:, sl] + pltpu.roll(xs, LANES - ROPE_DIMS // 2, 1) * sa[:, sl]
                     + pltpu.roll(xs, ROPE_DIMS // 2, 1) * sb[:, sl])
    return parts[0] if len(parts) == 1 else jnp.concatenate(parts, axis=1)


def _kv_rope_body(kv_ref, cos_ref, sa_ref, sb_ref, o_ref):
    o_ref[...] = _rope(kv_ref[...], cos_ref[...], sa_ref[...], sb_ref[...])


def swa_kv_rope(proj, kv_off, kvb, tabs, batch, lp):
    tp = proj.shape[0]
    nb = lp // SWA_BLOCK
    tab = pl.BlockSpec((SWA_BLOCK, kvb), lambda b, n: (n, 0))
    return pl.pallas_call(
        _kv_rope_body,
        grid=(batch, nb),
        in_specs=[pl.BlockSpec((SWA_BLOCK, kvb), lambda b, n: (b * nb + n, kv_off // kvb)), tab, tab, tab],
        out_specs=pl.BlockSpec((SWA_BLOCK, kvb), lambda b, n: (b * nb + n, 0)),
        out_shape=jax.ShapeDtypeStruct((tp, kvb), F32),
        compiler_params=_cparams(("parallel", "parallel")),
        name="swa_kv_rope",
    )(proj, *tabs)


def _swa_body(sink_ref, q_ref, kc_ref, kp_ref, km_ref, cos_ref, sa_ref, sb_ref, o_ref, *, group, kvw):
    n = pl.program_id(1)
    hk = pl.program_id(2)
    blk = SWA_BLOCK
    mode = SWA_MODE
    q = _rope(q_ref[...], cos_ref[...], sa_ref[...], sb_ref[...]) * (HEAD_DIM ** -0.5)
    qs = jnp.concatenate([q[:, gi * HEAD_DIM:(gi + 1) * HEAD_DIM] for gi in range(group)], axis=0)
    rows = group * blk
    qi = lax.broadcasted_iota(jnp.int32, (rows, blk), 0) & (blk - 1)
    kj = lax.broadcasted_iota(jnp.int32, (rows, blk), 1)
    far = 1 << 20
    cur_ok = kj + jnp.where(n >= 1, 0, far) <= qi
    prev_ok = kj > qi + jnp.where(n >= 2, 0, far)
    mi = lax.broadcasted_iota(jnp.int32, (rows, N_META), 0) & (blk - 1)
    mj = lax.broadcasted_iota(jnp.int32, (rows, N_META), 1)
    meta_ok = mj <= mi - PAD_FRONT + jnp.where(n >= 1, far, 0)
    sink = jnp.concatenate([jnp.full((blk, 1), sink_ref[hk * group + gi], F32) for gi in range(group)], axis=0)
    s_c = jnp.where(cur_ok, _mm(qs, kc_ref[:, 0:HEAD_DIM], _NT, mode), NEG_INF)
    s_p = jnp.where(prev_ok, _mm(qs, kp_ref[:, 0:HEAD_DIM], _NT, mode), NEG_INF)
    s_m = jnp.where(meta_ok, _mm(qs, km_ref[:, 0:HEAD_DIM], _NT, mode), NEG_INF)
    mx = jnp.maximum(jnp.maximum(jnp.max(s_c, axis=-1, keepdims=True), jnp.max(s_p, axis=-1, keepdims=True)),
                     jnp.maximum(jnp.max(s_m, axis=-1, keepdims=True), sink))
    p_c = jnp.exp(s_c - mx)
    p_p = jnp.exp(s_p - mx)
    p_m = jnp.exp(s_m - mx)
    den = (jnp.sum(p_c, axis=-1, keepdims=True) + jnp.sum(p_p, axis=-1, keepdims=True)
           + jnp.sum(p_m, axis=-1, keepdims=True) + jnp.exp(sink - mx))
    acc = (_mm(p_c, kc_ref[:, HEAD_DIM:2 * HEAD_DIM], _NN, mode) + _mm(p_p, kp_ref[:, HEAD_DIM:2 * HEAD_DIM], _NN, mode)
           + _mm(p_m, km_ref[:, HEAD_DIM:2 * HEAD_DIM], _NN, mode))
    out = acc / den
    o_ref[...] = jnp.concatenate([out[gi * blk:(gi + 1) * blk, :] for gi in range(group)], axis=1).astype(o_ref.dtype)


def swa_attention(proj, q_off, kvh, sinks, qtabs, batch, lp, n_q_heads, n_kv_heads):
    tp = proj.shape[0]
    nb = lp // SWA_BLOCK
    group = n_q_heads // n_kv_heads
    qb = group * HEAD_DIM
    sw = n_q_heads * HEAD_DIM
    tab = pl.BlockSpec((SWA_BLOCK, qb), lambda b, n, hk: (n, 0))
    kv_spec = lambda f: pl.BlockSpec((None, SWA_BLOCK, LANES), f)
    return pl.pallas_call(
        functools.partial(_swa_body, group=group, kvw=n_kv_heads * HEAD_DIM),
        grid=(batch, nb, n_kv_heads),
        in_specs=[
            pl.BlockSpec(memory_space=pltpu.SMEM),
            pl.BlockSpec((SWA_BLOCK, qb), lambda b, n, hk: (b * nb + n, q_off // qb + hk)),
            kv_spec(lambda b, n, hk: (hk, b * nb + n, 0)),
            kv_spec(lambda b, n, hk: (hk, b * nb + jnp.maximum(n - 1, 0), 0)),
            pl.BlockSpec((None, N_META, LANES),
                         lambda b, n, hk: (hk, b * (lp // N_META) + PAD_FRONT // N_META, 0)),
            tab, tab, tab,
        ],
        out_specs=pl.BlockSpec((SWA_BLOCK, qb), lambda b, n, hk: (b * nb + n, hk)),
        out_shape=jax.ShapeDtypeStruct((tp, sw), BF16),
        compiler_params=_cparams(("parallel", "parallel", "parallel")),
        name="swa_attention",
    )(sinks, proj, kvh, kvh, kvh, *qtabs)


def _hgrn_body(q_ref, f_ref, i_ref, g_ref, lb_ref, gain_ref, o_ref, st_ref, *, c, heads, sub):
    @pl.when(pl.program_id(2) == 0)
    def _():
        st_ref[...] = jnp.zeros_like(st_ref)

    dk = HGRN_HEAD_DIM
    hs = range(heads)
    nb = c // sub
    ri = lax.broadcasted_iota(jnp.int32, (c, c), 0)
    ci = lax.broadcasted_iota(jnp.int32, (c, c), 1)
    incl = ci <= ri
    ltri = jnp.where(incl, 1.0, 0.0)
    cs = lax.broadcasted_iota(jnp.int32, (sub, c), 1)
    col = lambda ref, h: ref[:, h * dk:(h + 1) * dk]
    q = [col(q_ref, h) for h in hs]
    iv = [col(i_ref, h) for h in hs]
    f = [col(lb_ref, h) + (1.0 - col(lb_ref, h)) * _sigmoid(col(f_ref, h)) for h in hs]
    kx = [1.0 - f[h] for h in hs]
    gc = [_select_sum(ltri, jnp.log(f[h])) for h in hs]
    st = [st_ref[h] for h in hs]
    mode = HGRN_MODE
    o_inter = [_mm(q[h] * jnp.exp(gc[h]), st[h], _NT, mode) for h in hs]
    for h in hs:
        g_last = gc[h][c - 1:c, :]
        st_ref[h] = st[h] * jnp.exp(g_last) + _mm(iv[h], kx[h] * jnp.exp(g_last - gc[h]), _TN, mode)
    attn = []
    for h in hs:
        blocks = []
        for b in range(nb):
            rows = slice(b * sub, (b + 1) * sub)
            qb, gb = q[h][rows, :], gc[h][rows, :]
            if b == 0:
                blk = jnp.zeros((sub, c), F32)
            else:
                ref = gc[h][b * sub:b * sub + 1, :]
                k_sc = kx[h] * jnp.exp(jnp.minimum(ref - gc[h], 0.0))
                blk = _mm(qb * jnp.exp(gb - ref), k_sc, _NT, mode)
            for s in range(b * sub, (b + 1) * sub):
                dec = jnp.exp(jnp.minimum(gb - gc[h][s:s + 1, :], 0.0))
                blk = jnp.where(cs == s, jnp.sum(qb * kx[h][s:s + 1, :] * dec, axis=-1, keepdims=True), blk)
            blocks.append(blk)
        attn.append(jnp.where(incl, jnp.concatenate(blocks, axis=0), 0.0))
    outs = []
    for h in hs:
        o = _mm(attn[h], iv[h], _NN, mode) + o_inter[h]
        on = o * lax.rsqrt(jnp.mean(o * o, axis=-1, keepdims=True) + NORM_EPS) * col(gain_ref, h)
        g = col(g_ref, h)
        outs.append(on * (g * _sigmoid(g)))
    o_ref[...] = jnp.concatenate(outs, axis=1).astype(o_ref.dtype)


def hgrn_mix(proj, hg_off, lb, gain, batch, lp):
    tp = proj.shape[0]
    hw = lb.shape[0]
    dk = HGRN_HEAD_DIM
    nh = hw // dk
    c = CHUNK
    nc = lp // c
    heads = _pick(nh, (HGRN_HEADS_PER_STEP, 2, 1))
    gw = heads * dk
    blk = lambda part: pl.BlockSpec((c, gw), lambda b, h, ch: (b * nc + ch, (hg_off + part * hw) // gw + h))
    vec = pl.BlockSpec((1, gw), lambda b, h, ch: (0, h))
    return pl.pallas_call(
        functools.partial(_hgrn_body, c=c, heads=heads, sub=HGRN_SUB),
        grid=(batch, nh // heads, nc),
        in_specs=[blk(0), blk(1), blk(2), blk(3), vec, vec],
        out_specs=pl.BlockSpec((c, gw), lambda b, h, ch: (b * nc + ch, h)),
        out_shape=jax.ShapeDtypeStruct((tp, hw), BF16),
        scratch_shapes=[pltpu.VMEM((heads, dk, dk), F32)],
        compiler_params=_cparams(("parallel", "parallel", "arbitrary")),
        name="hgrn_mix",
    )(proj, proj, proj, proj, lb.reshape(1, hw), gain.reshape(1, hw))


def _out_proj_body(h_ref, ya_ref, yb_ref, yc_ref, wa_ref, wb_ref, wc_ref, o_ref):
    o_ref[...] = (h_ref[...] + _dot(ya_ref[...], wa_ref[...]) + _dot(yb_ref[...], wb_ref[...])
                  + _dot(yc_ref[...], wc_ref[...]))


def out_proj(h, ya, yb, yc, w_bf16):
    tp, d = h.shape
    wa, wb, wc = ya.shape[1], yb.shape[1], yc.shape[1]
    tm = _pick(tp, (512, 256, 128))
    tn = _pick(d, (1024, 512, 256, 128))
    return pl.pallas_call(
        _out_proj_body,
        grid=(tp // tm, d // tn),
        in_specs=[
            pl.BlockSpec((tm, tn), lambda i, j: (i, j)),
            pl.BlockSpec((tm, wa), lambda i, j: (i, 0)),
            pl.BlockSpec((tm, wb), lambda i, j: (i, 0)),
            pl.BlockSpec((tm, wc), lambda i, j: (i, 0)),
            pl.BlockSpec((wa, tn), lambda i, j: (0, j)),
            pl.BlockSpec((wb, tn), lambda i, j: (0, j)),
            pl.BlockSpec((wc, tn), lambda i, j: (0, j)),
        ],
        out_specs=pl.BlockSpec((tm, tn), lambda i, j: (i, j)),
        out_shape=jax.ShapeDtypeStruct((tp, d), F32),
        compiler_params=_cparams(("parallel", "parallel")),
        name="out_proj",
    )(h, ya, yb, yc, w_bf16[:wa], w_bf16[wa:wa + wb], w_bf16[wa + wb:])


def _norm_t_body(h_ref, g_ref, o_ref):
    x = h_ref[...]
    ms = jnp.mean(x * x, axis=-1, keepdims=True)
    y = x * lax.rsqrt(ms + NORM_EPS) * g_ref[...]
    o_ref[...] = y.T.astype(o_ref.dtype)


def norm_transposed(h, gain):
    tp, d = h.shape
    tm = _pick(tp, (512, 256, 128))
    return pl.pallas_call(
        _norm_t_body,
        grid=(tp // tm,),
        in_specs=[pl.BlockSpec((tm, d), lambda i: (i, 0)), pl.BlockSpec((1, d), lambda i: (0, 0))],
        out_specs=pl.BlockSpec((None, d, tm), lambda i: (i, 0, 0)),
        out_shape=jax.ShapeDtypeStruct((tp // tm, d, tm), BF16),
        compiler_params=_cparams(("parallel",)),
        name="peer_norm_t",
    )(h, gain.reshape(1, d))


def _top_values(s, k):
    rows = lax.broadcasted_iota(jnp.int32, (k, 1), 0)
    out = jnp.zeros((k, s.shape[1]), F32)
    for r in range(k):
        m = jnp.max(s, axis=0, keepdims=True)
        out = jnp.where(rows == r, m, out)
        s = jnp.where(s == m, -jnp.inf, s)
    return out


def _peer_route_body(wq_ref, zn_ref, keys_ref, s1_ref, s2_ref, e1_ref, e2_ref, thr_ref, *, half, topk):
    qt = _dot(wq_ref[...], zn_ref[...])
    s1 = _dot(keys_ref[0], qt[:half], HI)
    s2 = _dot(keys_ref[1], qt[half:], HI)
    v1 = _top_values(s1, topk)
    v2 = _top_values(s2, topk)
    sub8 = lax.broadcasted_iota(jnp.int32, (8, 1), 0)
    parts = [v1[0:1, :] + v2]
    i = 1
    while topk // (i + 1) > 1:
        n_j = topk // (i + 1)
        parts.append(jnp.where(sub8 < n_j, v1[i:i + 1, :] + v2[0:8, :], -jnp.inf))
        i += 1
    parts.append(v1[i:, :] + v2[0:1, :])
    best = _top_values(jnp.concatenate(parts, axis=0), topk)
    mx = best[0:1, :]
    z = jnp.sum(jnp.exp(best - mx), axis=0, keepdims=True)
    s1_ref[...] = s1
    s2_ref[...] = s2
    e1_ref[...] = jnp.exp(s1 - v1[0:1, :]) / z
    e2_ref[...] = jnp.exp(s2 - v2[0:1, :])
    thr_ref[...] = best[topk - 1:topk, :]


def peer_route(znt, wqt_bf16, keys):
    nt, d, tm = znt.shape
    nh, _, nk, half = keys.shape
    sc = pl.BlockSpec((None, None, nk, tm), lambda i, h: (i, h, 0, 0))
    big = jax.ShapeDtypeStruct((nt, nh, nk, tm), F32)
    return pl.pallas_call(
        functools.partial(_peer_route_body, half=half, topk=PEER_TOPK),
        grid=(nt, nh),
        in_specs=[
            pl.BlockSpec((2 * half, d), lambda i, h: (h, 0)),
            pl.BlockSpec((None, d, tm), lambda i, h: (i, 0, 0)),
            pl.BlockSpec((None, 2, nk, half), lambda i, h: (h, 0, 0, 0)),
        ],
        out_specs=[sc, sc, sc, sc, pl.BlockSpec((None, None, 1, tm), lambda i, h: (i, h, 0, 0))],
        out_shape=[big, big, big, big, jax.ShapeDtypeStruct((nt, nh, 1, tm), F32)],
        compiler_params=_cparams(("parallel", "arbitrary")),
        name="peer_route",
    )(wqt_bf16, znt, keys)


def _gelu(x):
    return 0.5 * x * (1.0 + lax.erf(x * (2.0 ** -0.5)))


def _peer_expert_body(zn_ref, u_ref, vt_ref, s1_ref, s2_ref, e1_ref, e2_ref, thr_ref, o_ref, a0_ref, a1_ref,
                      hid0_ref, hid1_ref, *, nk, nh, nj, split):
    j = pl.program_id(1)
    te = u_ref.shape[0]
    d = vt_ref.shape[0]
    nq = te // nk
    sub = nk // split
    n_proj = 2 if nq % 2 == 0 else 1
    pr = te // n_proj
    n_act = nq * split
    n_app = n_act // 2
    rows = d // n_app

    def project(g, hid_w):
        sl = slice(g * pr, (g + 1) * pr)
        hid_w[sl, :] = _dot(u_ref[sl, :], zn_ref[...])

    def activate(q, r, hid_r, a_w):
        a = (j - 1) * nq + q
        rs = slice(r * sub, (r + 1) * sub)
        gate = None
        for h in range(nh):
            cand = s1_ref[h, pl.ds(a, 1), :] + s2_ref[h, rs, :]
            gh = jnp.where(cand >= thr_ref[h], e1_ref[h, pl.ds(a, 1), :] * e2_ref[h, rs, :], 0.0)
            gate = gh if gate is None else gate + gh
        es = slice(q * nk + r * sub, q * nk + (r + 1) * sub)
        a_w[es, :] = (gate * _gelu(hid_r[es, :])).astype(BF16)

    def apply(p, a_r):
        sl = slice(p * rows, (p + 1) * rows)
        o_ref[sl, :] += _dot(vt_ref[sl, :], a_r[...])

    def step(do_project, do_activate, do_apply, par):
        hid_w, hid_r = (hid0_ref, hid1_ref) if par == 0 else (hid1_ref, hid0_ref)
        a_w, a_r = (a1_ref, a0_ref) if par == 0 else (a0_ref, a1_ref)
        for p in range(n_act):
            if do_project and p % (n_act // n_proj) == 0:
                project(p // (n_act // n_proj), hid_w)
            if do_activate:
                activate(p // split, p % split, hid_r, a_w)
            if do_apply and p % 2 == 1:
                apply(p // 2, a_r)

    @pl.when(j == 0)
    def _():
        o_ref[...] = jnp.zeros_like(o_ref)
        step(True, False, False, 0)

    @pl.when(j == 1)
    def _():
        step(True, True, False, 1)

    for par in range(2):
        @pl.when((j >= 2) & (j < nj) & (lax.rem(j, 2) == par))
        def _(par=par):
            step(True, True, True, par)

    @pl.when(j == nj)
    def _():
        step(False, True, True, nj % 2)

    @pl.when(j == nj + 1)
    def _():
        step(False, False, True, (nj + 1) % 2)


def peer_experts(znt, u_bf16, v_bf16, s1, s2, e1, e2, thr):
    nt, d, tm = znt.shape
    ne = u_bf16.shape[0]
    _, nh, nk, _ = s1.shape
    te = _pick(ne, (512, 256, 128))
    nj = ne // te
    assert nj >= 2
    vt_bf16 = v_bf16.reshape(nj, te, d).transpose(0, 2, 1)
    once = dict(pipeline_mode=pl.Buffered(1))
    aux = pl.BlockSpec((None, nh, nk, tm), lambda i, j: (i, 0, 0, 0), **once)
    return pl.pallas_call(
        functools.partial(_peer_expert_body, nk=nk, nh=nh, nj=nj, split=PEER_SPLIT),
        grid=(nt, nj + 2),
        in_specs=[
            pl.BlockSpec((None, d, tm), lambda i, j: (i, 0, 0), **once),
            pl.BlockSpec((te, d), lambda i, j: (jnp.minimum(j, nj - 1), 0)),
            pl.BlockSpec((None, d, te), lambda i, j: (jnp.clip(j - 2, 0, nj - 1), 0, 0)),
            aux, aux, aux, aux,
            pl.BlockSpec((None, nh, 1, tm), lambda i, j: (i, 0, 0, 0), **once),
        ],
        out_specs=pl.BlockSpec((None, d, tm), lambda i, j: (i, 0, 0)),
        out_shape=jax.ShapeDtypeStruct((nt, d, tm), F32),
        scratch_shapes=[pltpu.VMEM((te, tm), BF16), pltpu.VMEM((te, tm), BF16),
                        pltpu.VMEM((te, tm), F32), pltpu.VMEM((te, tm), F32)],
        compiler_params=_cparams(("parallel", "arbitrary")),
        name="peer_experts",
    )(znt, u_bf16, vt_bf16, s1, s2, e1, e2, thr)


def _residual_t_body(h_ref, yt_ref, o_ref):
    o_ref[...] = h_ref[...] + yt_ref[...].T


def residual_add_transposed(h, yt):
    tp, d = h.shape
    tm = yt.shape[2]
    return pl.pallas_call(
        _residual_t_body,
        grid=(tp // tm,),
        in_specs=[pl.BlockSpec((tm, d), lambda i: (i, 0)), pl.BlockSpec((None, d, tm), lambda i: (i, 0, 0))],
        out_specs=pl.BlockSpec((tm, d), lambda i: (i, 0)),
        out_shape=jax.ShapeDtypeStruct((tp, d), F32),
        compiler_params=_cparams(("parallel",)),
        name="peer_residual",
    )(h, yt)


def _final_norm_body(h_ref, g_ref, o_ref):
    x = h_ref[...]
    ms = jnp.mean(x * x, axis=-1, keepdims=True)
    o_ref[...] = x * lax.rsqrt(ms + NORM_EPS) * g_ref[...]


def final_norm(h, gain, batch, lp):
    tp, d = h.shape
    nb = lp // SWA_BLOCK
    seq = lp - SWA_BLOCK
    out = pl.pallas_call(
        _final_norm_body,
        grid=(batch, nb - 1),
        in_specs=[pl.BlockSpec((SWA_BLOCK, d), lambda b, n: (b * nb + n + 1, 0)),
                  pl.BlockSpec((1, d), lambda b, n: (0, 0))],
        out_specs=pl.BlockSpec((SWA_BLOCK, d), lambda b, n: (b * (nb - 1) + n, 0)),
        out_shape=jax.ShapeDtypeStruct((batch * seq, d), F32),
        compiler_params=_cparams(("parallel", "parallel")),
        name="final_norm",
    )(h, gain.reshape(1, d))
    return out.reshape(batch, seq, d)


def _rope_tables(lp, width, rot_heads):
    pos = jnp.maximum(jnp.arange(lp, dtype=F32) - PAD_FRONT, 0.0)
    inv_freq = ROPE_THETA ** (-jnp.arange(0, ROPE_DIMS, 2, dtype=F32) / ROPE_DIMS)
    ang = pos[:, None] * inv_freq[None, :]
    cos, sin = jnp.cos(ang), jnp.sin(ang)
    half = ROPE_DIMS // 2
    rest = HEAD_DIM - ROPE_DIMS
    one = jnp.ones((lp, rest), F32)
    zero = jnp.zeros((lp, rest), F32)
    zh = jnp.zeros((lp, half), F32)
    c_head = jnp.concatenate([cos, cos, one], axis=1)
    a_head = jnp.concatenate([-sin, zh, zero], axis=1)
    b_head = jnp.concatenate([zh, sin, zero], axis=1)
    n_heads = width // HEAD_DIM
    idn = jnp.ones((lp, HEAD_DIM), F32)
    zer = jnp.zeros((lp, HEAD_DIM), F32)
    build = lambda rot, flat: jnp.concatenate([rot if i < rot_heads else flat for i in range(n_heads)], axis=1)
    return build(c_head, idn), build(a_head, zer), build(b_head, zer)


def _layout(d, rc, sw, kvw, hw):
    qb = sw * HEAD_DIM // kvw
    kvb = _roundup(2 * kvw, LANES)
    hg_off = _roundup(rc, hw)
    q_off = _roundup(hg_off + 4 * hw, qb)
    kv_off = _roundup(q_off + sw, kvb)
    total = _roundup(kv_off + kvb, 512)
    return q_off, kv_off, kvb, hg_off, total


def kernel(x, meta_tokens, norm_mix, norm_ffn, norm_final, w_in, w_out, rwkv_mu, rwkv_w0, rwkv_w2, rwkv_a0,
           rwkv_a2, rwkv_g2, rwkv_k_k, rwkv_k_a, rwkv_r_k, rwkv_lnx_w, rwkv_lnx_b, swa_sinks, hgrn_lb, hgrn_norm,
           peer_wq, peer_keys, peer_u, peer_v):
    batch, seq, d = x.shape
    depth = w_in.shape[0]
    lp = seq + SWA_BLOCK
    rw = rwkv_w0.shape[1]
    rc = rwkv_mu.shape[1]
    n_q = swa_sinks.shape[1]
    sw = n_q * HEAD_DIM
    hw = hgrn_lb.shape[1]
    kvw = (w_in.shape[2] - rc - sw - 4 * hw) // 2
    n_kv = kvw // HEAD_DIM
    q_off, kv_off, kvb, hg_off, n_tot = _layout(d, rc, sw, kvw, hw)

    pad = jnp.zeros((batch, PAD_FRONT, d), x.dtype)
    meta = jnp.broadcast_to(meta_tokens.astype(x.dtype)[None], (batch, N_META, d))
    h = jnp.concatenate([pad, meta, x], axis=1).reshape(batch * lp, d)

    q_tabs = _rope_tables(lp, sw // n_kv, sw // n_kv // HEAD_DIM)
    kv_tabs = _rope_tables(lp, kvb, n_kv)
    lb_all = jnp.cumsum(jax.nn.softmax(hgrn_lb.astype(F32), axis=0), axis=0)
    lb_all = lb_all - lb_all[0:1]

    for l in range(depth):
        w = w_in[l]
        zc = lambda n: jnp.zeros((d, n), w.dtype)
        w_re = jnp.concatenate([
            w[:, :rc], zc(hg_off - rc),
            w[:, rc + sw + 2 * kvw:], zc(q_off - hg_off - 4 * hw),
            w[:, rc:rc + sw], zc(kv_off - q_off - sw),
            w[:, rc + sw:rc + sw + 2 * kvw], zc(n_tot - kv_off - 2 * kvw)], axis=1).astype(BF16)
        proj = norm_matmul(h, norm_mix[l], w_re, lp)

        rp = dict(mu=rwkv_mu[l], w0=rwkv_w0[l], w2=rwkv_w2[l], a0=rwkv_a0[l], a2=rwkv_a2[l], g2=rwkv_g2[l],
                  k_k=rwkv_k_k[l], k_a=rwkv_k_a[l], r_k=rwkv_r_k[l], lnx_w=rwkv_lnx_w[l], lnx_b=rwkv_lnx_b[l])
        y_a = rwkv_scan(*rwkv_prep(proj, lp, rp), batch, lp)

        kv = swa_kv_rope(proj, kv_off, kvb, kv_tabs, batch, lp)
        kvh = jnp.stack([jnp.concatenate([kv[:, i * HEAD_DIM:(i + 1) * HEAD_DIM],
                                          kv[:, kvw + i * HEAD_DIM:kvw + (i + 1) * HEAD_DIM]], axis=1)
                         for i in range(n_kv)])
        y_b = swa_attention(proj, q_off, kvh, swa_sinks[l], q_tabs, batch, lp, n_q, n_kv)

        y_c = hgrn_mix(proj, hg_off, lb_all[l], hgrn_norm[l], batch, lp)

        h = out_proj(h, y_a, y_b, y_c, w_out[l].astype(BF16))

        znt = norm_transposed(h, norm_ffn[l])
        s1, s2, e1, e2, thr = peer_route(znt, peer_wq[l].T.astype(BF16), peer_keys[l])
        yt = peer_experts(znt, peer_u[l].astype(BF16), peer_v[l].astype(BF16), s1, s2, e1, e2, thr)
        h = residual_add_transposed(h, yt)

    return final_norm(h, norm_final, batch, lp)
```

```python
import functools

import jax
import jax.numpy as jnp
from jax import lax
from jax.experimental import pallas as pl
from jax.experimental.pallas import tpu as pltpu

F32 = jnp.float32
BF16 = jnp.bfloat16
HI = lax.Precision.HIGHEST

N_META = 16
HEAD_DIM = 64
NORM_EPS = 1e-5
NEG_INF = -1e30
RWKV_GN_EPS = 64e-5
SWA_BLOCK = 128
PAD_FRONT = SWA_BLOCK - N_META
ROPE_THETA = 500000.0
ROPE_DIMS = HEAD_DIM // 4
HGRN_HEAD_DIM = 128
CHUNK = 64
PEER_TOPK = 16
LANES = 128
MXU_WIDTH = 256
VMEM_LIMIT_MB = 56
RWKV_HEADS_PER_STEP = 24
SWA_MODE = "bf16"
PEER_SPLIT = 4
HGRN_HEADS_PER_STEP = 8
HGRN_MODE = "bf16"
HGRN_SUB = 16
RWKV_SCAN_MODE = "bf16"


def _cparams(sem, vmem_mb=VMEM_LIMIT_MB, **kw):
    return pltpu.CompilerParams(dimension_semantics=sem, vmem_limit_bytes=vmem_mb * 1024 * 1024, **kw)


def _pick(n, cands):
    for c in cands:
        if n % c == 0:
            return c
    raise ValueError(f"no tile in {cands} divides {n}")


def _roundup(n, m):
    return -(-n // m) * m


def _dot(a, b, prec=None):
    return jnp.dot(a, b, preferred_element_type=F32, precision=prec)


def _dot_nt(a, b, prec=None):
    return lax.dot_general(a, b, (((1,), (1,)), ((), ())), preferred_element_type=F32, precision=prec)


def _dot_tn(a, b, prec=None):
    return lax.dot_general(a, b, (((0,), (0,)), ((), ())), preferred_element_type=F32, precision=prec)


def _row_in_batch(i, tm, lp):
    pos0 = lax.rem(i * tm, lp)
    rb = pos0 + lax.broadcasted_iota(jnp.int32, (tm, 1), 0)
    return jnp.where(rb >= lp, rb - lp, rb)


def _sigmoid(x):
    return 1.0 / (1.0 + jnp.exp(-x))


def _norm_mm_body(x_ref, g_ref, w_ref, o_ref, z_ref, *, lp, tm):
    i = pl.program_id(0)

    @pl.when(pl.program_id(1) == 0)
    def _():
        x = x_ref[...]
        ms = jnp.mean(x * x, axis=-1, keepdims=True)
        y = x * lax.rsqrt(ms + NORM_EPS) * g_ref[...]
        z_ref[...] = jnp.where(_row_in_batch(i, tm, lp) >= PAD_FRONT, y, 0.0).astype(BF16)

    o_ref[...] = _dot(z_ref[...], w_ref[...])


def norm_matmul(h, gain, w_bf16, lp):
    tp, d = h.shape
    n = w_bf16.shape[1]
    tm = _pick(tp, (512, 256, 128))
    tn = _pick(n, (1024, 512, 256, 128))
    return pl.pallas_call(
        functools.partial(_norm_mm_body, lp=lp, tm=tm),
        grid=(tp // tm, n // tn),
        in_specs=[
            pl.BlockSpec((tm, d), lambda i, j: (i, 0)),
            pl.BlockSpec((1, d), lambda i, j: (0, 0)),
            pl.BlockSpec((d, tn), lambda i, j: (0, j)),
        ],
        out_specs=pl.BlockSpec((tm, tn), lambda i, j: (i, j)),
        out_shape=jax.ShapeDtypeStruct((tp, n), F32),
        scratch_shapes=[pltpu.VMEM((tm, d), BF16)],
        compiler_params=_cparams(("parallel", "arbitrary")),
        name="norm_in_proj",
    )(h, gain.reshape(1, d), w_bf16)


def _mm(a, b, dims, mode):
    if mode == "bf16":
        return lax.dot_general(a.astype(BF16), b.astype(BF16), (dims, ((), ())), preferred_element_type=F32)
    return lax.dot_general(a, b, (dims, ((), ())), preferred_element_type=F32, precision=HI)


_NN = ((1,), (0,))
_NT = ((1,), (1,))
_TN = ((0,), (0,))


def _split3(x):
    hi = x.astype(BF16)
    r1 = x - hi.astype(F32)
    mid = r1.astype(BF16)
    lo = (r1 - mid.astype(F32)).astype(BF16)
    return hi, mid, lo


def _dot_x3(a, b):
    a_hi = a.astype(BF16)
    a_lo = (a - a_hi.astype(F32)).astype(BF16)
    b_hi = b.astype(BF16)
    b_lo = (b - b_hi.astype(F32)).astype(BF16)
    return (_dot(a_hi, b_lo) + _dot(a_lo, b_hi)) + _dot(a_hi, b_hi)


def _select_sum(sel, x, x_on_left=False):
    sel = sel.astype(BF16)
    terms = [_dot(t, sel) if x_on_left else _dot(sel, t) for t in _split3(x)]
    return (terms[2] + terms[1]) + terms[0]


def _head_sums(x):
    shift = HEAD_DIM.bit_length() - 1
    er = lax.broadcasted_iota(jnp.int32, (LANES, LANES), 0) >> shift
    ec = lax.broadcasted_iota(jnp.int32, (LANES, LANES), 1) >> shift
    ones_bd = jnp.where(er == ec, 1.0, 0.0)
    return jnp.concatenate(
        [_select_sum(ones_bd, x[:, s * LANES:(s + 1) * LANES], x_on_left=True)
         for s in range(x.shape[1] // LANES)], axis=1)


def _rwkv_prep_body(pa_ref, prev_ref, mu_ref, w0_ref, w2_ref, a0_ref, a2_ref, g2_ref, kk_ref, ka_ref, rk_ref,
                    lnw_ref, lnb_ref, out_ref, post_ref, dl_ref, *, lp, tm, rw, rd, ri, rg, c):
    i = pl.program_id(0)
    x = pa_ref[...]
    rowi = lax.broadcasted_iota(jnp.int32, (tm, 1), 0)
    prev_row = jnp.where(lax.rem(i * tm, lp) == 0, 0.0, prev_ref[7:8, :])
    prev = jnp.where(rowi == 0, prev_row, pltpu.roll(x, 1, 0))
    xs = x + (prev - x) * mu_ref[...]
    r = xs[:, 0:rw]
    k = xs[:, rw:2 * rw]
    v = xs[:, 2 * rw:3 * rw]
    wl = xs[:, 3 * rw:3 * rw + rd]
    al = xs[:, 3 * rw + rd:3 * rw + rd + ri]
    gl = xs[:, 3 * rw + rd + ri:3 * rw + rd + ri + rg]
    wv = w0_ref[...] + _dot_x3(jnp.tanh(wl), w2_ref[...])
    nwv = -wv
    softplus = jnp.maximum(nwv, 0.0) + jnp.log(1.0 + jnp.exp(-jnp.abs(nwv)))
    logw = -jnp.exp(-softplus - 0.5)
    alr = _sigmoid(a0_ref[...] + _dot_x3(al, a2_ref[...]))
    g = _dot_x3(_sigmoid(gl), g2_ref[...])
    kkr = k * kk_ref[...]
    kk = kkr / jnp.maximum(jnp.sqrt(_head_sums(kkr * kkr)), 1e-12)
    k2 = k * (1.0 + (alr - 1.0) * ka_ref[...])
    shift = c.bit_length() - 1
    tr = lax.broadcasted_iota(jnp.int32, (tm, tm), 0)
    tc = lax.broadcasted_iota(jnp.int32, (tm, tm), 1)
    ltri = jnp.where(tc <= tr, jnp.where((tr >> shift) == (tc >> shift), 1.0, 0.0), 0.0)
    cum = _select_sum(ltri, logw)
    tot = jnp.concatenate(
        [jnp.broadcast_to(cum[(q + 1) * c - 1:(q + 1) * c, :], (c, rw)) for q in range(tm // c)], axis=0)
    p_inv = jnp.exp(-cum)
    p_rem = jnp.exp(tot - cum)
    kb = kk * alr
    od = out_ref.dtype
    out_ref[0] = (-kk * jnp.exp(cum - logw)).astype(od)
    out_ref[1] = (r * jnp.exp(cum)).astype(od)
    out_ref[2] = (kb * p_inv).astype(od)
    out_ref[3] = (k2 * p_inv).astype(od)
    out_ref[4] = v.astype(od)
    out_ref[5] = (kb * p_rem).astype(od)
    out_ref[6] = (k2 * p_rem).astype(od)
    post_ref[0] = lnw_ref[...] * g
    post_ref[1] = (lnb_ref[...] + _head_sums(r * k2 * rk_ref[...]) * v) * g
    for q in range(tm // c):
        dl_ref[q] = jnp.exp(tot[q * c:q * c + 1, :])


def rwkv_prep(proj, lp, p):
    tp = proj.shape[0]
    rw = p["w0"].shape[0]
    rd, ri, rg = p["w2"].shape[0], p["a2"].shape[0], p["g2"].shape[0]
    rc = 3 * rw + rd + ri + rg
    tm = 128
    c = CHUNK
    row = lambda a: a.reshape(1, -1)
    full = lambda a: pl.BlockSpec(a.shape, lambda i: (0, 0))
    args = [row(p["mu"]), row(p["w0"]), p["w2"], row(p["a0"]), p["a2"], p["g2"], row(p["k_k"]), row(p["k_a"]),
            row(p["r_k"]), row(p["lnx_w"]), row(p["lnx_b"])]
    return pl.pallas_call(
        functools.partial(_rwkv_prep_body, lp=lp, tm=tm, rw=rw, rd=rd, ri=ri, rg=rg, c=c),
        grid=(tp // tm,),
        in_specs=[
            pl.BlockSpec((tm, rc), lambda i: (i, 0)),
            pl.BlockSpec((8, rc), lambda i: (jnp.maximum(i * (tm // 8) - 1, 0), 0)),
        ] + [full(a) for a in args],
        out_specs=[pl.BlockSpec((7, tm, rw), lambda i: (0, i, 0)),
                   pl.BlockSpec((2, tm, rw), lambda i: (0, i, 0)),
                   pl.BlockSpec((tm // c, 1, rw), lambda i: (i, 0, 0))],
        out_shape=[jax.ShapeDtypeStruct((7, tp, rw), BF16 if RWKV_SCAN_MODE == "bf16" else F32),
                   jax.ShapeDtypeStruct((2, tp, rw), F32), jax.ShapeDtypeStruct((tp // c, 1, rw), F32)],
        compiler_params=_cparams(("parallel",)),
        name="rwkv_prep",
    )(proj, proj, *args)


def _rwkv_scan_body(x_ref, post_ref, dl_ref, o_ref, s_ref, *, heads, c, mode):
    @pl.when(pl.program_id(2) == 0)
    def _():
        s_ref[...] = jnp.zeros_like(s_ref)

    ri = lax.broadcasted_iota(jnp.int32, (c, c), 0)
    ci = lax.broadcasted_iota(jnp.int32, (c, c), 1)
    strict = ci < ri
    incl = ci <= ri
    hs = range(heads)
    ld = lambda q, h: x_ref[q, :, h * HEAD_DIM:(h + 1) * HEAD_DIM]
    cat0 = lambda a, b: jnp.concatenate([a, b], axis=0)
    cat1 = lambda a, b: jnp.concatenate([a, b], axis=1)
    ar = [cat0(ld(0, h), ld(1, h)) for h in hs]
    m = [_mm(ar[h], cat0(ld(2, h), ld(3, h)), _NT, mode) for h in hs]
    s0 = [s_ref[h] for h in hs]
    ars = [_mm(ar[h], s0[h], _NT, mode) for h in hs]
    v = [ld(4, h) for h in hs]
    nk = [jnp.where(strict, m[h][:c, :c], 0.0) for h in hs]
    m_ak = [jnp.where(strict, m[h][:c, c:], 0.0) for h in hs]
    m_r = [cat1(jnp.where(incl, m[h][c:, :c], 0.0), jnp.where(incl, m[h][c:, c:], 0.0)) for h in hs]
    u = [ars[h][:c] + _mm(m_ak[h], v[h], _NN, mode) for h in hs]
    steps = c.bit_length() - 1
    for it in range(steps):
        if it + 1 < steps:
            both = [_mm(nk[h], cat1(u[h], nk[h]), _NN, mode) for h in hs]
            u = [u[h] + both[h][:, :c] for h in hs]
            nk = [both[h][:, c:] for h in hs]
        else:
            u = [u[h] + _mm(nk[h], u[h], _NN, mode) for h in hs]
    uv = [cat0(u[h].astype(v[h].dtype), v[h]) for h in hs]
    y = [ars[h][c:] + _mm(m_r[h], uv[h], _NN, mode) for h in hs]
    for h in hs:
        dl = dl_ref[0, :, h * HEAD_DIM:(h + 1) * HEAD_DIM]
        s_ref[h] = s0[h] * dl + _mm(uv[h], cat0(ld(5, h), ld(6, h)), _TN, mode)
    outs = []
    for h in hs:
        mean = jnp.mean(y[h], axis=-1, keepdims=True)
        yc = y[h] - mean
        var = jnp.mean(yc * yc, axis=-1, keepdims=True)
        hl = slice(h * HEAD_DIM, (h + 1) * HEAD_DIM)
        outs.append(yc * lax.rsqrt(var + RWKV_GN_EPS) * post_ref[0, :, hl] + post_ref[1, :, hl])
    o_ref[...] = jnp.concatenate(outs, axis=1).astype(o_ref.dtype)


def rwkv_scan(xs, post, dl, batch, lp):
    _, tp, rw = xs.shape
    c = CHUNK
    nc = lp // c
    heads = _pick(rw // HEAD_DIM, (RWKV_HEADS_PER_STEP, 2))
    gw = heads * HEAD_DIM
    return pl.pallas_call(
        functools.partial(_rwkv_scan_body, heads=heads, c=c, mode=RWKV_SCAN_MODE),
        grid=(batch, rw // gw, nc),
        in_specs=[pl.BlockSpec((7, c, gw), lambda b, hg, ch: (0, b * nc + ch, hg)),
                  pl.BlockSpec((2, c, gw), lambda b, hg, ch: (0, b * nc + ch, hg)),
                  pl.BlockSpec((1, 1, gw), lambda b, hg, ch: (b * nc + ch, 0, hg))],
        out_specs=pl.BlockSpec((c, gw), lambda b, hg, ch: (b * nc + ch, hg)),
        out_shape=jax.ShapeDtypeStruct((tp, rw), BF16),
        scratch_shapes=[pltpu.VMEM((heads, HEAD_DIM, HEAD_DIM), F32)],
        compiler_params=_cparams(("parallel", "parallel", "arbitrary")),
        name="rwkv_scan",
    )(xs, post, dl)


def _rope(x, cos, sa, sb):
    parts = []
    for s in range(x.shape[1] // LANES):
        sl = slice(s * LANES, (s + 1) * LANES)
        xs = x[:, sl]
        parts.append(xs * cos[:, sl] + pltpu.roll(xs, LANES - ROPE_DIMS // 2, 1) * sa[:, sl]
                     + pltpu.roll(xs, ROPE_DIMS // 2, 1) * sb[:, sl])
    return parts[0] if len(parts) == 1 else jnp.concatenate(parts, axis=1)


def _kv_rope_body(kv_ref, cos_ref, sa_ref, sb_ref, o_ref):
    o_ref[...] = _rope(kv_ref[...], cos_ref[...], sa_ref[...], sb_ref[...])


def swa_kv_rope(proj, kv_off, kvb, tabs, batch, lp):
    tp = proj.shape[0]
    nb = lp // SWA_BLOCK
    tab = pl.BlockSpec((SWA_BLOCK, kvb), lambda b, n: (n, 0))
    return pl.pallas_call(
        _kv_rope_body,
        grid=(batch, nb),
        in_specs=[pl.BlockSpec((SWA_BLOCK, kvb), lambda b, n: (b * nb + n, kv_off // kvb)), tab, tab, tab],
        out_specs=pl.BlockSpec((SWA_BLOCK, kvb), lambda b, n: (b * nb + n, 0)),
        out_shape=jax.ShapeDtypeStruct((tp, kvb), F32),
        compiler_params=_cparams(("parallel", "parallel")),
        name="swa_kv_rope",
    )(proj, *tabs)


def _swa_body(sink_ref, q_ref, kc_ref, kp_ref, km_ref, cos_ref, sa_ref, sb_ref, o_ref, *, group, n_kv):
    n = pl.program_id(1)
    blk = SWA_BLOCK
    mode = SWA_MODE
    qb = group * HEAD_DIM
    rows = group * blk
    hks = range(n_kv)
    qs = []
    for hk in hks:
        q = _rope(q_ref[:, hk * qb:(hk + 1) * qb], cos_ref[...], sa_ref[...], sb_ref[...]) * (HEAD_DIM ** -0.5)
        qs.append(jnp.concatenate([q[:, gi * HEAD_DIM:(gi + 1) * HEAD_DIM] for gi in range(group)], axis=0))
    qi = lax.broadcasted_iota(jnp.int32, (rows, blk), 0) & (blk - 1)
    kj = lax.broadcasted_iota(jnp.int32, (rows, blk), 1)
    far = 1 << 20
    cur_ok = kj + jnp.where(n >= 1, 0, far) <= qi
    prev_ok = kj > qi + jnp.where(n >= 2, 0, far)
    mi = lax.broadcasted_iota(jnp.int32, (rows, N_META), 0) & (blk - 1)
    mj = lax.broadcasted_iota(jnp.int32, (rows, N_META), 1)
    meta_ok = mj <= mi - PAD_FRONT + jnp.where(n >= 1, far, 0)
    kk, vv = slice(0, HEAD_DIM), slice(HEAD_DIM, 2 * HEAD_DIM)
    sink = [jnp.concatenate([jnp.full((blk, 1), sink_ref[hk * group + gi], F32) for gi in range(group)], axis=0)
            for hk in hks]
    s_c = [jnp.where(cur_ok, _mm(qs[hk], kc_ref[hk, :, kk], _NT, mode), NEG_INF) for hk in hks]
    s_p = [jnp.where(prev_ok, _mm(qs[hk], kp_ref[hk, :, kk], _NT, mode), NEG_INF) for hk in hks]
    s_m = [jnp.where(meta_ok, _mm(qs[hk], km_ref[hk, :, kk], _NT, mode), NEG_INF) for hk in hks]
    rmax = lambda s: jnp.max(s, axis=-1, keepdims=True)
    rsum = lambda s: jnp.sum(s, axis=-1, keepdims=True)
    mx = [jnp.maximum(jnp.maximum(rmax(s_c[hk]), rmax(s_p[hk])), jnp.maximum(rmax(s_m[hk]), sink[hk])) for hk in hks]
    p_c = [jnp.exp(s_c[hk] - mx[hk]) for hk in hks]
    p_p = [jnp.exp(s_p[hk] - mx[hk]) for hk in hks]
    p_m = [jnp.exp(s_m[hk] - mx[hk]) for hk in hks]
    den = [rsum(p_c[hk]) + rsum(p_p[hk]) + rsum(p_m[hk]) + jnp.exp(sink[hk] - mx[hk]) for hk in hks]
    acc = [_mm(p_c[hk], kc_ref[hk, :, vv], _NN, mode) + _mm(p_p[hk], kp_ref[hk, :, vv], _NN, mode)
           + _mm(p_m[hk], km_ref[hk, :, vv], _NN, mode) for hk in hks]
    out = [acc[hk] / den[hk] for hk in hks]
    o_ref[...] = jnp.concatenate([out[hk][gi * blk:(gi + 1) * blk, :] for hk in hks for gi in range(group)],
                                 axis=1).astype(o_ref.dtype)


def swa_attention(proj, q_off, kvh, sinks, qtabs, batch, lp, n_q_heads, n_kv_heads):
    tp = proj.shape[0]
    nb = lp // SWA_BLOCK
    group = n_q_heads // n_kv_heads
    qb = group * HEAD_DIM
    sw = n_q_heads * HEAD_DIM
    assert q_off % sw == 0
    tab = pl.BlockSpec((SWA_BLOCK, qb), lambda b, n: (n, 0))
    kv_spec = lambda f: pl.BlockSpec((n_kv_heads, SWA_BLOCK, LANES), f)
    return pl.pallas_call(
        functools.partial(_swa_body, group=group, n_kv=n_kv_heads),
        grid=(batch, nb),
        in_specs=[
            pl.BlockSpec(memory_space=pltpu.SMEM),
            pl.BlockSpec((SWA_BLOCK, sw), lambda b, n: (b * nb + n, q_off // sw)),
            kv_spec(lambda b, n: (0, b * nb + n, 0)),
            kv_spec(lambda b, n: (0, b * nb + jnp.maximum(n - 1, 0), 0)),
            pl.BlockSpec((n_kv_heads, N_META, LANES),
                         lambda b, n: (0, b * (lp // N_META) + PAD_FRONT // N_META, 0)),
            tab, tab, tab,
        ],
        out_specs=pl.BlockSpec((SWA_BLOCK, sw), lambda b, n: (b * nb + n, 0)),
        out_shape=jax.ShapeDtypeStruct((tp, sw), BF16),
        compiler_params=_cparams(("parallel", "parallel")),
        name="swa_attention",
    )(sinks, proj, kvh, kvh, kvh, *qtabs)


def _hgrn_body(q_ref, f_ref, i_ref, g_ref, lb_ref, gain_ref, o_ref, st_ref, *, c, heads, sub):
    @pl.when(pl.program_id(2) == 0)
    def _():
        st_ref[...] = jnp.zeros_like(st_ref)

    dk = HGRN_HEAD_DIM
    hs = range(heads)
    nb = c // sub
    ri = lax.broadcasted_iota(jnp.int32, (c, c), 0)
    ci = lax.broadcasted_iota(jnp.int32, (c, c), 1)
    incl = ci <= ri
    ltri = jnp.where(incl, 1.0, 0.0)
    cs = lax.broadcasted_iota(jnp.int32, (sub, c), 1)
    col = lambda ref, h: ref[:, h * dk:(h + 1) * dk]
    q = [col(q_ref, h) for h in hs]
    iv = [col(i_ref, h) for h in hs]
    f = [col(lb_ref, h) + (1.0 - col(lb_ref, h)) * _sigmoid(col(f_ref, h)) for h in hs]
    kx = [1.0 - f[h] for h in hs]
    gc = [_select_sum(ltri, jnp.log(f[h])) for h in hs]
    st = [st_ref[h] for h in hs]
    mode = HGRN_MODE
    o_inter = [_mm(q[h] * jnp.exp(gc[h]), st[h], _NT, mode) for h in hs]
    for h in hs:
        g_last = gc[h][c - 1:c, :]
        st_ref[h] = st[h] * jnp.exp(g_last) + _mm(iv[h], kx[h] * jnp.exp(g_last - gc[h]), _TN, mode)
    attn = []
    for h in hs:
        blocks = []
        for b in range(nb):
            rows = slice(b * sub, (b + 1) * sub)
            qb, gb = q[h][rows, :], gc[h][rows, :]
            if b == 0:
                blk = jnp.zeros((sub, c), F32)
            else:
                ref = gc[h][b * sub:b * sub + 1, :]
                k_sc = kx[h] * jnp.exp(jnp.minimum(ref - gc[h], 0.0))
                blk = _mm(qb * jnp.exp(gb - ref), k_sc, _NT, mode)
            for s in range(b * sub, (b + 1) * sub):
                dec = jnp.exp(jnp.minimum(gb - gc[h][s:s + 1, :], 0.0))
                blk = jnp.where(cs == s, jnp.sum(qb * kx[h][s:s + 1, :] * dec, axis=-1, keepdims=True), blk)
            blocks.append(blk)
        attn.append(jnp.where(incl, jnp.concatenate(blocks, axis=0), 0.0))
    outs = []
    for h in hs:
        o = _mm(attn[h], iv[h], _NN, mode) + o_inter[h]
        on = o * lax.rsqrt(jnp.mean(o * o, axis=-1, keepdims=True) + NORM_EPS) * col(gain_ref, h)
        g = col(g_ref, h)
        outs.append(on * (g * _sigmoid(g)))
    o_ref[...] = jnp.concatenate(outs, axis=1).astype(o_ref.dtype)


def hgrn_mix(proj, hg_off, lb, gain, batch, lp):
    tp = proj.shape[0]
    hw = lb.shape[0]
    dk = HGRN_HEAD_DIM
    nh = hw // dk
    c = CHUNK
    nc = lp // c
    heads = _pick(nh, (HGRN_HEADS_PER_STEP, 2, 1))
    gw = heads * dk
    blk = lambda part: pl.BlockSpec((c, gw), lambda b, h, ch: (b * nc + ch, (hg_off + part * hw) // gw + h))
    vec = pl.BlockSpec((1, gw), lambda b, h, ch: (0, h))
    return pl.pallas_call(
        functools.partial(_hgrn_body, c=c, heads=heads, sub=HGRN_SUB),
        grid=(batch, nh // heads, nc),
        in_specs=[blk(0), blk(1), blk(2), blk(3), vec, vec],
        out_specs=pl.BlockSpec((c, gw), lambda b, h, ch: (b * nc + ch, h)),
        out_shape=jax.ShapeDtypeStruct((tp, hw), BF16),
        scratch_shapes=[pltpu.VMEM((heads, dk, dk), F32)],
        compiler_params=_cparams(("parallel", "parallel", "arbitrary")),
        name="hgrn_mix",
    )(proj, proj, proj, proj, lb.reshape(1, hw), gain.reshape(1, hw))


def _out_proj_body(h_ref, ya_ref, yb_ref, yc_ref, wa_ref, wb_ref, wc_ref, o_ref):
    o_ref[...] = (h_ref[...] + _dot(ya_ref[...], wa_ref[...]) + _dot(yb_ref[...], wb_ref[...])
                  + _dot(yc_ref[...], wc_ref[...]))


def out_proj(h, ya, yb, yc, w_bf16):
    tp, d = h.shape
    wa, wb, wc = ya.shape[1], yb.shape[1], yc.shape[1]
    tm = _pick(tp, (512, 256, 128))
    tn = _pick(d, (1024, 512, 256, 128))
    return pl.pallas_call(
        _out_proj_body,
        grid=(tp // tm, d // tn),
        in_specs=[
            pl.BlockSpec((tm, tn), lambda i, j: (i, j)),
            pl.BlockSpec((tm, wa), lambda i, j: (i, 0)),
            pl.BlockSpec((tm, wb), lambda i, j: (i, 0)),
            pl.BlockSpec((tm, wc), lambda i, j: (i, 0)),
            pl.BlockSpec((wa, tn), lambda i, j: (0, j)),
            pl.BlockSpec((wb, tn), lambda i, j: (0, j)),
            pl.BlockSpec((wc, tn), lambda i, j: (0, j)),
        ],
        out_specs=pl.BlockSpec((tm, tn), lambda i, j: (i, j)),
        out_shape=jax.ShapeDtypeStruct((tp, d), F32),
        compiler_params=_cparams(("parallel", "parallel")),
        name="out_proj",
    )(h, ya, yb, yc, w_bf16[:wa], w_bf16[wa:wa + wb], w_bf16[wa + wb:])


def _norm_t_body(h_ref, g_ref, o_ref):
    x = h_ref[...]
    ms = jnp.mean(x * x, axis=-1, keepdims=True)
    y = x * lax.rsqrt(ms + NORM_EPS) * g_ref[...]
    o_ref[...] = y.T.astype(o_ref.dtype)


def norm_transposed(h, gain):
    tp, d = h.shape
    tm = _pick(tp, (512, 256, 128))
    return pl.pallas_call(
        _norm_t_body,
        grid=(tp // tm,),
        in_specs=[pl.BlockSpec((tm, d), lambda i: (i, 0)), pl.BlockSpec((1, d), lambda i: (0, 0))],
        out_specs=pl.BlockSpec((None, d, tm), lambda i: (i, 0, 0)),
        out_shape=jax.ShapeDtypeStruct((tp // tm, d, tm), BF16),
        compiler_params=_cparams(("parallel",)),
        name="peer_norm_t",
    )(h, gain.reshape(1, d))


def _top_values(s, k):
    rows = lax.broadcasted_iota(jnp.int32, (k, 1), 0)
    out = jnp.zeros((k, s.shape[1]), F32)
    for r in range(k):
        m = jnp.max(s, axis=0, keepdims=True)
        out = jnp.where(rows == r, m, out)
        s = jnp.where(s == m, -jnp.inf, s)
    return out


def _peer_route_body(wq_ref, zn_ref, keys_ref, s1_ref, s2_ref, e1_ref, e2_ref, thr_ref, *, half, topk):
    qt = _dot(wq_ref[...], zn_ref[...])
    s1 = _dot(keys_ref[0], qt[:half], HI)
    s2 = _dot(keys_ref[1], qt[half:], HI)
    v1 = _top_values(s1, topk)
    v2 = _top_values(s2, topk)
    sub8 = lax.broadcasted_iota(jnp.int32, (8, 1), 0)
    parts = [v1[0:1, :] + v2]
    i = 1
    while topk // (i + 1) > 1:
        n_j = topk // (i + 1)
        parts.append(jnp.where(sub8 < n_j, v1[i:i + 1, :] + v2[0:8, :], -jnp.inf))
        i += 1
    parts.append(v1[i:, :] + v2[0:1, :])
    best = _top_values(jnp.concatenate(parts, axis=0), topk)
    mx = best[0:1, :]
    z = jnp.sum(jnp.exp(best - mx), axis=0, keepdims=True)
    s1_ref[...] = s1
    s2_ref[...] = s2
    e1_ref[...] = jnp.exp(s1 - v1[0:1, :]) / z
    e2_ref[...] = jnp.exp(s2 - v2[0:1, :])
    thr_ref[...] = best[topk - 1:topk, :]


def peer_route(znt, wqt_bf16, keys):
    nt, d, tm = znt.shape
    nh, _, nk, half = keys.shape
    sc = pl.BlockSpec((None, None, nk, tm), lambda i, h: (i, h, 0, 0))
    big = jax.ShapeDtypeStruct((nt, nh, nk, tm), F32)
    return pl.pallas_call(
        functools.partial(_peer_route_body, half=half, topk=PEER_TOPK),
        grid=(nt, nh),
        in_specs=[
            pl.BlockSpec((2 * half, d), lambda i, h: (h, 0)),
            pl.BlockSpec((None, d, tm), lambda i, h: (i, 0, 0)),
            pl.BlockSpec((None, 2, nk, half), lambda i, h: (h, 0, 0, 0)),
        ],
        out_specs=[sc, sc, sc, sc, pl.BlockSpec((None, None, 1, tm), lambda i, h: (i, h, 0, 0))],
        out_shape=[big, big, big, big, jax.ShapeDtypeStruct((nt, nh, 1, tm), F32)],
        compiler_params=_cparams(("parallel", "arbitrary")),
        name="peer_route",
    )(wqt_bf16, znt, keys)


def _gelu(x):
    return 0.5 * x * (1.0 + lax.erf(x * (2.0 ** -0.5)))


def _peer_expert_body(zn_ref, u_ref, vt_ref, s1_ref, s2_ref, e1_ref, e2_ref, thr_ref, o_ref, a0_ref, a1_ref,
                      hid0_ref, hid1_ref, *, nk, nh, nj, split):
    j = pl.program_id(1)
    te = u_ref.shape[0]
    d = vt_ref.shape[0]
    nq = te // nk
    sub = nk // split
    rows = d // (nq * split)

    def project(q, hid_w):
        hid_w[q * nk:(q + 1) * nk, :] = _dot(u_ref[q * nk:(q + 1) * nk, :], zn_ref[...])

    def activate(q, r, hid_r, a_w):
        a = (j - 1) * nq + q
        rs = slice(r * sub, (r + 1) * sub)
        gate = None
        for h in range(nh):
            cand = s1_ref[h, pl.ds(a, 1), :] + s2_ref[h, rs, :]
            gh = jnp.where(cand >= thr_ref[h], e1_ref[h, pl.ds(a, 1), :] * e2_ref[h, rs, :], 0.0)
            gate = gh if gate is None else gate + gh
        es = slice(q * nk + r * sub, q * nk + (r + 1) * sub)
        a_w[es, :] = (gate * _gelu(hid_r[es, :])).astype(BF16)

    def apply(p, a_r):
        sl = slice(p * rows, (p + 1) * rows)
        o_ref[sl, :] += _dot(vt_ref[sl, :], a_r[...])

    def step(do_project, do_activate, do_apply, par):
        hid_w, hid_r = (hid0_ref, hid1_ref) if par == 0 else (hid1_ref, hid0_ref)
        a_w, a_r = (a1_ref, a0_ref) if par == 0 else (a0_ref, a1_ref)
        for q in range(nq):
            if do_project:
                project(q, hid_w)
            for r in range(split):
                if do_activate:
                    activate(q, r, hid_r, a_w)
                if do_apply:
                    apply(q * split + r, a_r)

    @pl.when(j == 0)
    def _():
        o_ref[...] = jnp.zeros_like(o_ref)
        step(True, False, False, 0)

    @pl.when(j == 1)
    def _():
        step(True, True, False, 1)

    for par in range(2):
        @pl.when((j >= 2) & (j < nj) & (lax.rem(j, 2) == par))
        def _(par=par):
            step(True, True, True, par)

    @pl.when(j == nj)
    def _():
        step(False, True, True, nj % 2)

    @pl.when(j == nj + 1)
    def _():
        step(False, False, True, (nj + 1) % 2)


def peer_experts(znt, u_bf16, v_bf16, s1, s2, e1, e2, thr):
    nt, d, tm = znt.shape
    ne = u_bf16.shape[0]
    _, nh, nk, _ = s1.shape
    te = _pick(ne, (512, 256, 128))
    nj = ne // te
    assert nj >= 2
    vt_bf16 = v_bf16.reshape(nj, te, d).transpose(0, 2, 1)
    once = dict(pipeline_mode=pl.Buffered(1))
    aux = pl.BlockSpec((None, nh, nk, tm), lambda i, j: (i, 0, 0, 0), **once)
    return pl.pallas_call(
        functools.partial(_peer_expert_body, nk=nk, nh=nh, nj=nj, split=PEER_SPLIT),
        grid=(nt, nj + 2),
        in_specs=[
            pl.BlockSpec((None, d, tm), lambda i, j: (i, 0, 0), **once),
            pl.BlockSpec((te, d), lambda i, j: (jnp.minimum(j, nj - 1), 0)),
            pl.BlockSpec((None, d, te), lambda i, j: (jnp.clip(j - 2, 0, nj - 1), 0, 0)),
            aux, aux, aux, aux,
            pl.BlockSpec((None, nh, 1, tm), lambda i, j: (i, 0, 0, 0), **once),
        ],
        out_specs=pl.BlockSpec((None, d, tm), lambda i, j: (i, 0, 0)),
        out_shape=jax.ShapeDtypeStruct((nt, d, tm), F32),
        scratch_shapes=[pltpu.VMEM((te, tm), BF16), pltpu.VMEM((te, tm), BF16),
                        pltpu.VMEM((te, tm), F32), pltpu.VMEM((te, tm), F32)],
        compiler_params=_cparams(("parallel", "arbitrary")),
        name="peer_experts",
    )(znt, u_bf16, vt_bf16, s1, s2, e1, e2, thr)


def _residual_t_body(h_ref, yt_ref, o_ref):
    o_ref[...] = h_ref[...] + yt_ref[...].T


def residual_add_transposed(h, yt):
    tp, d = h.shape
    tm = yt.shape[2]
    return pl.pallas_call(
        _residual_t_body,
        grid=(tp // tm,),
        in_specs=[pl.BlockSpec((tm, d), lambda i: (i, 0)), pl.BlockSpec((None, d, tm), lambda i: (i, 0, 0))],
        out_specs=pl.BlockSpec((tm, d), lambda i: (i, 0)),
        out_shape=jax.ShapeDtypeStruct((tp, d), F32),
        compiler_params=_cparams(("parallel",)),
        name="peer_residual",
    )(h, yt)


def _final_norm_body(h_ref, g_ref, o_ref):
    x = h_ref[...]
    ms = jnp.mean(x * x, axis=-1, keepdims=True)
    o_ref[...] = x * lax.rsqrt(ms + NORM_EPS) * g_ref[...]


def final_norm(h, gain, batch, lp):
    tp, d = h.shape
    nb = lp // SWA_BLOCK
    seq = lp - SWA_BLOCK
    out = pl.pallas_call(
        _final_norm_body,
        grid=(batch, nb - 1),
        in_specs=[pl.BlockSpec((SWA_BLOCK, d), lambda b, n: (b * nb + n + 1, 0)),
                  pl.BlockSpec((1, d), lambda b, n: (0, 0))],
        out_specs=pl.BlockSpec((SWA_BLOCK, d), lambda b, n: (b * (nb - 1) + n, 0)),
        out_shape=jax.ShapeDtypeStruct((batch * seq, d), F32),
        compiler_params=_cparams(("parallel", "parallel")),
        name="final_norm",
    )(h, gain.reshape(1, d))
    return out.reshape(batch, seq, d)


def _rope_tables(lp, width, rot_heads):
    pos = jnp.maximum(jnp.arange(lp, dtype=F32) - PAD_FRONT, 0.0)
    inv_freq = ROPE_THETA ** (-jnp.arange(0, ROPE_DIMS, 2, dtype=F32) / ROPE_DIMS)
    ang = pos[:, None] * inv_freq[None, :]
    cos, sin = jnp.cos(ang), jnp.sin(ang)
    half = ROPE_DIMS // 2
    rest = HEAD_DIM - ROPE_DIMS
    one = jnp.ones((lp, rest), F32)
    zero = jnp.zeros((lp, rest), F32)
    zh = jnp.zeros((lp, half), F32)
    c_head = jnp.concatenate([cos, cos, one], axis=1)
    a_head = jnp.concatenate([-sin, zh, zero], axis=1)
    b_head = jnp.concatenate([zh, sin, zero], axis=1)
    n_heads = width // HEAD_DIM
    idn = jnp.ones((lp, HEAD_DIM), F32)
    zer = jnp.zeros((lp, HEAD_DIM), F32)
    build = lambda rot, flat: jnp.concatenate([rot if i < rot_heads else flat for i in range(n_heads)], axis=1)
    return build(c_head, idn), build(a_head, zer), build(b_head, zer)


def _layout(d, rc, sw, kvw, hw):
    kvb = _roundup(2 * kvw, LANES)
    hg_off = _roundup(rc, hw)
    q_off = _roundup(hg_off + 4 * hw, sw)
    kv_off = _roundup(q_off + sw, kvb)
    total = _roundup(kv_off + kvb, 512)
    return q_off, kv_off, kvb, hg_off, total


def kernel(x, meta_tokens, norm_mix, norm_ffn, norm_final, w_in, w_out, rwkv_mu, rwkv_w0, rwkv_w2, rwkv_a0,
           rwkv_a2, rwkv_g2, rwkv_k_k, rwkv_k_a, rwkv_r_k, rwkv_lnx_w, rwkv_lnx_b, swa_sinks, hgrn_lb, hgrn_norm,
           peer_wq, peer_keys, peer_u, peer_v):
    batch, seq, d = x.shape
    depth = w_in.shape[0]
    lp = seq + SWA_BLOCK
    rw = rwkv_w0.shape[1]
    rc = rwkv_mu.shape[1]
    n_q = swa_sinks.shape[1]
    sw = n_q * HEAD_DIM
    hw = hgrn_lb.shape[1]
    kvw = (w_in.shape[2] - rc - sw - 4 * hw) // 2
    n_kv = kvw // HEAD_DIM
    q_off, kv_off, kvb, hg_off, n_tot = _layout(d, rc, sw, kvw, hw)

    pad = jnp.zeros((batch, PAD_FRONT, d), x.dtype)
    meta = jnp.broadcast_to(meta_tokens.astype(x.dtype)[None], (batch, N_META, d))
    h = jnp.concatenate([pad, meta, x], axis=1).reshape(batch * lp, d)

    q_tabs = _rope_tables(lp, sw // n_kv, sw // n_kv // HEAD_DIM)
    kv_tabs = _rope_tables(lp, kvb, n_kv)
    lb_all = jnp.cumsum(jax.nn.softmax(hgrn_lb.astype(F32), axis=0), axis=0)
    lb_all = lb_all - lb_all[0:1]

    for l in range(depth):
        w = w_in[l]
        zc = lambda n: jnp.zeros((d, n), w.dtype)
        w_re = jnp.concatenate([
            w[:, :rc], zc(hg_off - rc),
            w[:, rc + sw + 2 * kvw:], zc(q_off - hg_off - 4 * hw),
            w[:, rc:rc + sw], zc(kv_off - q_off - sw),
            w[:, rc + sw:rc + sw + 2 * kvw], zc(n_tot - kv_off - 2 * kvw)], axis=1).astype(BF16)
        proj = norm_matmul(h, norm_mix[l], w_re, lp)

        rp = dict(mu=rwkv_mu[l], w0=rwkv_w0[l], w2=rwkv_w2[l], a0=rwkv_a0[l], a2=rwkv_a2[l], g2=rwkv_g2[l],
                  k_k=rwkv_k_k[l], k_a=rwkv_k_a[l], r_k=rwkv_r_k[l], lnx_w=rwkv_lnx_w[l], lnx_b=rwkv_lnx_b[l])
        y_a = rwkv_scan(*rwkv_prep(proj, lp, rp), batch, lp)

        kv = swa_kv_rope(proj, kv_off, kvb, kv_tabs, batch, lp)
        kvh = jnp.stack([jnp.concatenate([kv[:, i * HEAD_DIM:(i + 1) * HEAD_DIM],
                                          kv[:, kvw + i * HEAD_DIM:kvw + (i + 1) * HEAD_DIM]], axis=1)
                         for i in range(n_kv)])
        y_b = swa_attention(proj, q_off, kvh, swa_sinks[l], q_tabs, batch, lp, n_q, n_kv)

        y_c = hgrn_mix(proj, hg_off, lb_all[l], hgrn_norm[l], batch, lp)

        h = out_proj(h, y_a, y_b, y_c, w_out[l].astype(BF16))

        znt = norm_transposed(h, norm_ffn[l])
        s1, s2, e1, e2, thr = peer_route(znt, peer_wq[l].T.astype(BF16), peer_keys[l])
        yt = peer_experts(znt, peer_u[l].astype(BF16), peer_v[l].astype(BF16), s1, s2, e1, e2, thr)
        h = residual_add_transposed(h, yt)

    return final_norm(h, norm_final, batch, lp)
```

```python
import functools

import jax
import jax.numpy as jnp
from jax import lax
from jax.experimental import pallas as pl
from jax.experimental.pallas import tpu as pltpu

F32 = jnp.float32
BF16 = jnp.bfloat16
HI = lax.Precision.HIGHEST

N_META = 16
HEAD_DIM = 64
NORM_EPS = 1e-5
NEG_INF = -1e30
RWKV_GN_EPS = 64e-5
SWA_BLOCK = 128
PAD_FRONT = SWA_BLOCK - N_META
ROPE_THETA = 500000.0
ROPE_DIMS = HEAD_DIM // 4
HGRN_HEAD_DIM = 128
CHUNK = 64
PEER_TOPK = 16
LANES = 128
MXU_WIDTH = 256
VMEM_LIMIT_MB = 56
RWKV_HEADS_PER_STEP = 24
SWA_MODE = "bf16"
PEER_SPLIT = 4
PROJ_DTYPE = BF16
PEER_ROUTE_HEADS_PER_STEP = 4
HGRN_HEADS_PER_STEP = 8
HGRN_MODE = "bf16"
HGRN_SUB = 16
RWKV_SCAN_MODE = "bf16"


def _cparams(sem, vmem_mb=VMEM_LIMIT_MB, **kw):
    return pltpu.CompilerParams(dimension_semantics=sem, vmem_limit_bytes=vmem_mb * 1024 * 1024, **kw)


def _pick(n, cands):
    for c in cands:
        if n % c == 0:
            return c
    raise ValueError(f"no tile in {cands} divides {n}")


def _roundup(n, m):
    return -(-n // m) * m


def _dot(a, b, prec=None):
    return jnp.dot(a, b, preferred_element_type=F32, precision=prec)


def _dot_nt(a, b, prec=None):
    return lax.dot_general(a, b, (((1,), (1,)), ((), ())), preferred_element_type=F32, precision=prec)


def _dot_tn(a, b, prec=None):
    return lax.dot_general(a, b, (((0,), (0,)), ((), ())), preferred_element_type=F32, precision=prec)


def _row_in_batch(i, tm, lp):
    pos0 = lax.rem(i * tm, lp)
    rb = pos0 + lax.broadcasted_iota(jnp.int32, (tm, 1), 0)
    return jnp.where(rb >= lp, rb - lp, rb)


def _sigmoid(x):
    return 1.0 / (1.0 + jnp.exp(-x))


def _norm_mm_body(x_ref, g_ref, w_ref, o_ref, z_ref, *, lp, tm):
    i = pl.program_id(0)

    @pl.when(pl.program_id(1) == 0)
    def _():
        x = x_ref[...]
        ms = jnp.mean(x * x, axis=-1, keepdims=True)
        y = x * lax.rsqrt(ms + NORM_EPS) * g_ref[...]
        z_ref[...] = jnp.where(_row_in_batch(i, tm, lp) >= PAD_FRONT, y, 0.0).astype(BF16)

    o_ref[...] = _dot(z_ref[...], w_ref[...]).astype(o_ref.dtype)


def norm_matmul(h, gain, w_bf16, lp):
    tp, d = h.shape
    n = w_bf16.shape[1]
    tm = _pick(tp, (512, 256, 128))
    tn = _pick(n, (1024, 512, 256, 128))
    return pl.pallas_call(
        functools.partial(_norm_mm_body, lp=lp, tm=tm),
        grid=(tp // tm, n // tn),
        in_specs=[
            pl.BlockSpec((tm, d), lambda i, j: (i, 0)),
            pl.BlockSpec((1, d), lambda i, j: (0, 0)),
            pl.BlockSpec((d, tn), lambda i, j: (0, j)),
        ],
        out_specs=pl.BlockSpec((tm, tn), lambda i, j: (i, j)),
        out_shape=jax.ShapeDtypeStruct((tp, n), PROJ_DTYPE),
        scratch_shapes=[pltpu.VMEM((tm, d), BF16)],
        compiler_params=_cparams(("parallel", "arbitrary")),
        name="norm_in_proj",
    )(h, gain.reshape(1, d), w_bf16)


def _mm(a, b, dims, mode):
    if mode == "bf16":
        return lax.dot_general(a.astype(BF16), b.astype(BF16), (dims, ((), ())), preferred_element_type=F32)
    return lax.dot_general(a, b, (dims, ((), ())), preferred_element_type=F32, precision=HI)


_NN = ((1,), (0,))
_NT = ((1,), (1,))
_TN = ((0,), (0,))


def _split3(x):
    hi = x.astype(BF16)
    r1 = x - hi.astype(F32)
    mid = r1.astype(BF16)
    lo = (r1 - mid.astype(F32)).astype(BF16)
    return hi, mid, lo


def _dot_x3(a, b):
    a_hi = a.astype(BF16)
    a_lo = (a - a_hi.astype(F32)).astype(BF16)
    b_hi = b.astype(BF16)
    b_lo = (b - b_hi.astype(F32)).astype(BF16)
    return (_dot(a_hi, b_lo) + _dot(a_lo, b_hi)) + _dot(a_hi, b_hi)


def _select_sum(sel, x, x_on_left=False):
    sel = sel.astype(BF16)
    terms = [_dot(t, sel) if x_on_left else _dot(sel, t) for t in _split3(x)]
    return (terms[2] + terms[1]) + terms[0]


def _head_sums(x):
    shift = HEAD_DIM.bit_length() - 1
    er = lax.broadcasted_iota(jnp.int32, (LANES, LANES), 0) >> shift
    ec = lax.broadcasted_iota(jnp.int32, (LANES, LANES), 1) >> shift
    ones_bd = jnp.where(er == ec, 1.0, 0.0)
    return jnp.concatenate(
        [_select_sum(ones_bd, x[:, s * LANES:(s + 1) * LANES], x_on_left=True)
         for s in range(x.shape[1] // LANES)], axis=1)


def _rwkv_prep_body(pa_ref, prev_ref, mu_ref, w0_ref, w2_ref, a0_ref, a2_ref, g2_ref, kk_ref, ka_ref, rk_ref,
                    lnw_ref, lnb_ref, out_ref, post_ref, dl_ref, *, lp, tm, rw, rd, ri, rg, c):
    i = pl.program_id(0)
    x = pa_ref[...].astype(F32)
    rowi = lax.broadcasted_iota(jnp.int32, (tm, 1), 0)
    pr = prev_ref.shape[0]
    prev_row = jnp.where(lax.rem(i * tm, lp) == 0, 0.0, prev_ref[...].astype(F32)[pr - 1:pr, :])
    prev = jnp.where(rowi == 0, prev_row, pltpu.roll(x, 1, 0))
    xs = x + (prev - x) * mu_ref[...]
    r = xs[:, 0:rw]
    k = xs[:, rw:2 * rw]
    v = xs[:, 2 * rw:3 * rw]
    wl = xs[:, 3 * rw:3 * rw + rd]
    al = xs[:, 3 * rw + rd:3 * rw + rd + ri]
    gl = xs[:, 3 * rw + rd + ri:3 * rw + rd + ri + rg]
    wv = w0_ref[...] + _dot_x3(jnp.tanh(wl), w2_ref[...])
    nwv = -wv
    softplus = jnp.maximum(nwv, 0.0) + jnp.log(1.0 + jnp.exp(-jnp.abs(nwv)))
    logw = -jnp.exp(-softplus - 0.5)
    alr = _sigmoid(a0_ref[...] + _dot_x3(al, a2_ref[...]))
    g = _dot_x3(_sigmoid(gl), g2_ref[...])
    kkr = k * kk_ref[...]
    kk = kkr / jnp.maximum(jnp.sqrt(_head_sums(kkr * kkr)), 1e-12)
    k2 = k * (1.0 + (alr - 1.0) * ka_ref[...])
    shift = c.bit_length() - 1
    tr = lax.broadcasted_iota(jnp.int32, (tm, tm), 0)
    tc = lax.broadcasted_iota(jnp.int32, (tm, tm), 1)
    ltri = jnp.where(tc <= tr, jnp.where((tr >> shift) == (tc >> shift), 1.0, 0.0), 0.0)
    cum = _select_sum(ltri, logw)
    tot = jnp.concatenate(
        [jnp.broadcast_to(cum[(q + 1) * c - 1:(q + 1) * c, :], (c, rw)) for q in range(tm // c)], axis=0)
    p_inv = jnp.exp(-cum)
    p_rem = jnp.exp(tot - cum)
    kb = kk * alr
    od = out_ref.dtype
    out_ref[0] = (-kk * jnp.exp(cum - logw)).astype(od)
    out_ref[1] = (r * jnp.exp(cum)).astype(od)
    out_ref[2] = (kb * p_inv).astype(od)
    out_ref[3] = (k2 * p_inv).astype(od)
    out_ref[4] = v.astype(od)
    out_ref[5] = (kb * p_rem).astype(od)
    out_ref[6] = (k2 * p_rem).astype(od)
    post_ref[0] = lnw_ref[...] * g
    post_ref[1] = (lnb_ref[...] + _head_sums(r * k2 * rk_ref[...]) * v) * g
    for q in range(tm // c):
        dl_ref[q] = jnp.exp(tot[q * c:q * c + 1, :])


def rwkv_prep(proj, lp, p):
    tp = proj.shape[0]
    rw = p["w0"].shape[0]
    rd, ri, rg = p["w2"].shape[0], p["a2"].shape[0], p["g2"].shape[0]
    rc = 3 * rw + rd + ri + rg
    tm = 128
    c = CHUNK
    pr = 8 * 4 // proj.dtype.itemsize
    row = lambda a: a.reshape(1, -1)
    full = lambda a: pl.BlockSpec(a.shape, lambda i: (0, 0))
    args = [row(p["mu"]), row(p["w0"]), p["w2"], row(p["a0"]), p["a2"], p["g2"], row(p["k_k"]), row(p["k_a"]),
            row(p["r_k"]), row(p["lnx_w"]), row(p["lnx_b"])]
    return pl.pallas_call(
        functools.partial(_rwkv_prep_body, lp=lp, tm=tm, rw=rw, rd=rd, ri=ri, rg=rg, c=c),
        grid=(tp // tm,),
        in_specs=[
            pl.BlockSpec((tm, rc), lambda i: (i, 0)),
            pl.BlockSpec((pr, rc), lambda i: (jnp.maximum(i * (tm // pr) - 1, 0), 0)),
        ] + [full(a) for a in args],
        out_specs=[pl.BlockSpec((7, tm, rw), lambda i: (0, i, 0)),
                   pl.BlockSpec((2, tm, rw), lambda i: (0, i, 0)),
                   pl.BlockSpec((tm // c, 1, rw), lambda i: (i, 0, 0))],
        out_shape=[jax.ShapeDtypeStruct((7, tp, rw), BF16 if RWKV_SCAN_MODE == "bf16" else F32),
                   jax.ShapeDtypeStruct((2, tp, rw), F32), jax.ShapeDtypeStruct((tp // c, 1, rw), F32)],
        compiler_params=_cparams(("parallel",)),
        name="rwkv_prep",
    )(proj, proj, *args)


def _rwkv_scan_body(x_ref, post_ref, dl_ref, o_ref, s_ref, *, heads, c, mode):
    @pl.when(pl.program_id(2) == 0)
    def _():
        s_ref[...] = jnp.zeros_like(s_ref)

    ri = lax.broadcasted_iota(jnp.int32, (c, c), 0)
    ci = lax.broadcasted_iota(jnp.int32, (c, c), 1)
    strict = ci < ri
    incl = ci <= ri
    hs = range(heads)
    ld = lambda q, h: x_ref[q, :, h * HEAD_DIM:(h + 1) * HEAD_DIM]
    cat0 = lambda a, b: jnp.concatenate([a, b], axis=0)
    cat1 = lambda a, b: jnp.concatenate([a, b], axis=1)
    ar = [cat0(ld(0, h), ld(1, h)) for h in hs]
    m = [_mm(ar[h], cat0(ld(2, h), ld(3, h)), _NT, mode) for h in hs]
    s0 = [s_ref[h] for h in hs]
    ars = [_mm(ar[h], s0[h], _NT, mode) for h in hs]
    v = [ld(4, h) for h in hs]
    nk = [jnp.where(strict, m[h][:c, :c], 0.0) for h in hs]
    m_ak = [jnp.where(strict, m[h][:c, c:], 0.0) for h in hs]
    m_r = [cat1(jnp.where(incl, m[h][c:, :c], 0.0), jnp.where(incl, m[h][c:, c:], 0.0)) for h in hs]
    u = [ars[h][:c] + _mm(m_ak[h], v[h], _NN, mode) for h in hs]
    steps = c.bit_length() - 1
    for it in range(steps):
        if it + 1 < steps:
            both = [_mm(nk[h], cat1(u[h], nk[h]), _NN, mode) for h in hs]
            u = [u[h] + both[h][:, :c] for h in hs]
            nk = [both[h][:, c:] for h in hs]
        else:
            u = [u[h] + _mm(nk[h], u[h], _NN, mode) for h in hs]
    uv = [cat0(u[h].astype(v[h].dtype), v[h]) for h in hs]
    y = [ars[h][c:] + _mm(m_r[h], uv[h], _NN, mode) for h in hs]
    for h in hs:
        dl = dl_ref[0, :, h * HEAD_DIM:(h + 1) * HEAD_DIM]
        s_ref[h] = s0[h] * dl + _mm(uv[h], cat0(ld(5, h), ld(6, h)), _TN, mode)
    outs = []
    for h in hs:
        mean = jnp.mean(y[h], axis=-1, keepdims=True)
        yc = y[h] - mean
        var = jnp.mean(yc * yc, axis=-1, keepdims=True)
        hl = slice(h * HEAD_DIM, (h + 1) * HEAD_DIM)
        outs.append(yc * lax.rsqrt(var + RWKV_GN_EPS) * post_ref[0, :, hl] + post_ref[1, :, hl])
    o_ref[...] = jnp.concatenate(outs, axis=1).astype(o_ref.dtype)


def rwkv_scan(xs, post, dl, batch, lp):
    _, tp, rw = xs.shape
    c = CHUNK
    nc = lp // c
    heads = _pick(rw // HEAD_DIM, (RWKV_HEADS_PER_STEP, 2))
    gw = heads * HEAD_DIM
    return pl.pallas_call(
        functools.partial(_rwkv_scan_body, heads=heads, c=c, mode=RWKV_SCAN_MODE),
        grid=(batch, rw // gw, nc),
        in_specs=[pl.BlockSpec((7, c, gw), lambda b, hg, ch: (0, b * nc + ch, hg)),
                  pl.BlockSpec((2, c, gw), lambda b, hg, ch: (0, b * nc + ch, hg)),
                  pl.BlockSpec((1, 1, gw), lambda b, hg, ch: (b * nc + ch, 0, hg))],
        out_specs=pl.BlockSpec((c, gw), lambda b, hg, ch: (b * nc + ch, hg)),
        out_shape=jax.ShapeDtypeStruct((tp, rw), BF16),
        scratch_shapes=[pltpu.VMEM((heads, HEAD_DIM, HEAD_DIM), F32)],
        compiler_params=_cparams(("parallel", "parallel", "arbitrary")),
        name="rwkv_scan",
    )(xs, post, dl)


def _rope(x, cos, sa, sb):
    parts = []
    for s in range(x.shape[1] // LANES):
        sl = slice(s * LANES, (s + 1) * LANES)
        xs = x[:, sl]
        parts.append(xs * cos[:, sl] + pltpu.roll(xs, LANES - ROPE_DIMS // 2, 1) * sa[:, sl]
                     + pltpu.roll(xs, ROPE_DIMS // 2, 1) * sb[:, sl])
    return parts[0] if len(parts) == 1 else jnp.concatenate(parts, axis=1)


def _kv_rope_body(kv_ref, cos_ref, sa_ref, sb_ref, o_ref):
    o_ref[...] = _rope(kv_ref[...].astype(F32), cos_ref[...], sa_ref[...], sb_ref[...])


def swa_kv_rope(proj, kv_off, kvb, tabs, batch, lp):
    tp = proj.shape[0]
    nb = lp // SWA_BLOCK
    tab = pl.BlockSpec((SWA_BLOCK, kvb), lambda b, n: (n, 0))
    return pl.pallas_call(
        _kv_rope_body,
        grid=(batch, nb),
        in_specs=[pl.BlockSpec((SWA_BLOCK, kvb), lambda b, n: (b * nb + n, kv_off // kvb)), tab, tab, tab],
        out_specs=pl.BlockSpec((SWA_BLOCK, kvb), lambda b, n: (b * nb + n, 0)),
        out_shape=jax.ShapeDtypeStruct((tp, kvb), F32),
        compiler_params=_cparams(("parallel", "parallel")),
        name="swa_kv_rope",
    )(proj, *tabs)


def _swa_body(sink_ref, q_ref, kc_ref, kp_ref, km_ref, cos_ref, sa_ref, sb_ref, o_ref, *, group, n_kv):
    n = pl.program_id(1)
    blk = SWA_BLOCK
    mode = SWA_MODE
    qb = group * HEAD_DIM
    rows = group * blk
    hks = range(n_kv)
    qs = []
    for hk in hks:
        q = _rope(q_ref[:, hk * qb:(hk + 1) * qb].astype(F32), cos_ref[...], sa_ref[...], sb_ref[...])
        q = q * (HEAD_DIM ** -0.5)
        qs.append(jnp.concatenate([q[:, gi * HEAD_DIM:(gi + 1) * HEAD_DIM] for gi in range(group)], axis=0))
    qi = lax.broadcasted_iota(jnp.int32, (rows, blk), 0) & (blk - 1)
    kj = lax.broadcasted_iota(jnp.int32, (rows, blk), 1)
    far = 1 << 20
    cur_ok = kj + jnp.where(n >= 1, 0, far) <= qi
    prev_ok = kj > qi + jnp.where(n >= 2, 0, far)
    mi = lax.broadcasted_iota(jnp.int32, (rows, N_META), 0) & (blk - 1)
    mj = lax.broadcasted_iota(jnp.int32, (rows, N_META), 1)
    meta_ok = mj <= mi - PAD_FRONT + jnp.where(n >= 1, far, 0)
    kk, vv = slice(0, HEAD_DIM), slice(HEAD_DIM, 2 * HEAD_DIM)
    sink = [jnp.concatenate([jnp.full((blk, 1), sink_ref[hk * group + gi], F32) for gi in range(group)], axis=0)
            for hk in hks]
    s_c = [jnp.where(cur_ok, _mm(qs[hk], kc_ref[hk, :, kk], _NT, mode), NEG_INF) for hk in hks]
    s_p = [jnp.where(prev_ok, _mm(qs[hk], kp_ref[hk, :, kk], _NT, mode), NEG_INF) for hk in hks]
    s_m = [jnp.where(meta_ok, _mm(qs[hk], km_ref[hk, :, kk], _NT, mode), NEG_INF) for hk in hks]
    rmax = lambda s: jnp.max(s, axis=-1, keepdims=True)
    rsum = lambda s: jnp.sum(s, axis=-1, keepdims=True)
    mx = [jnp.maximum(jnp.maximum(rmax(s_c[hk]), rmax(s_p[hk])), jnp.maximum(rmax(s_m[hk]), sink[hk])) for hk in hks]
    p_c = [jnp.exp(s_c[hk] - mx[hk]) for hk in hks]
    p_p = [jnp.exp(s_p[hk] - mx[hk]) for hk in hks]
    p_m = [jnp.exp(s_m[hk] - mx[hk]) for hk in hks]
    den = [rsum(p_c[hk]) + rsum(p_p[hk]) + rsum(p_m[hk]) + jnp.exp(sink[hk] - mx[hk]) for hk in hks]
    acc = [_mm(p_c[hk], kc_ref[hk, :, vv], _NN, mode) + _mm(p_p[hk], kp_ref[hk, :, vv], _NN, mode)
           + _mm(p_m[hk], km_ref[hk, :, vv], _NN, mode) for hk in hks]
    out = [acc[hk] / den[hk] for hk in hks]
    o_ref[...] = jnp.concatenate([out[hk][gi * blk:(gi + 1) * blk, :] for hk in hks for gi in range(group)],
                                 axis=1).astype(o_ref.dtype)


def swa_attention(proj, q_off, kvh, sinks, qtabs, batch, lp, n_q_heads, n_kv_heads):
    tp = proj.shape[0]
    nb = lp // SWA_BLOCK
    group = n_q_heads // n_kv_heads
    qb = group * HEAD_DIM
    sw = n_q_heads * HEAD_DIM
    assert q_off % sw == 0
    tab = pl.BlockSpec((SWA_BLOCK, qb), lambda b, n: (n, 0))
    kv_spec = lambda f: pl.BlockSpec((n_kv_heads, SWA_BLOCK, LANES), f)
    return pl.pallas_call(
        functools.partial(_swa_body, group=group, n_kv=n_kv_heads),
        grid=(batch, nb),
        in_specs=[
            pl.BlockSpec(memory_space=pltpu.SMEM),
            pl.BlockSpec((SWA_BLOCK, sw), lambda b, n: (b * nb + n, q_off // sw)),
            kv_spec(lambda b, n: (0, b * nb + n, 0)),
            kv_spec(lambda b, n: (0, b * nb + jnp.maximum(n - 1, 0), 0)),
            pl.BlockSpec((n_kv_heads, N_META, LANES),
                         lambda b, n: (0, b * (lp // N_META) + PAD_FRONT // N_META, 0)),
            tab, tab, tab,
        ],
        out_specs=pl.BlockSpec((SWA_BLOCK, sw), lambda b, n: (b * nb + n, 0)),
        out_shape=jax.ShapeDtypeStruct((tp, sw), BF16),
        compiler_params=_cparams(("parallel", "parallel")),
        name="swa_attention",
    )(sinks, proj, kvh, kvh, kvh, *qtabs)


def _hgrn_body(q_ref, f_ref, i_ref, g_ref, lb_ref, gain_ref, o_ref, st_ref, *, c, heads, sub):
    @pl.when(pl.program_id(2) == 0)
    def _():
        st_ref[...] = jnp.zeros_like(st_ref)

    dk = HGRN_HEAD_DIM
    hs = range(heads)
    nb = c // sub
    ri = lax.broadcasted_iota(jnp.int32, (c, c), 0)
    ci = lax.broadcasted_iota(jnp.int32, (c, c), 1)
    incl = ci <= ri
    ltri = jnp.where(incl, 1.0, 0.0)
    cs = lax.broadcasted_iota(jnp.int32, (sub, c), 1)
    col = lambda ref, h: ref[:, h * dk:(h + 1) * dk].astype(F32)
    q = [col(q_ref, h) for h in hs]
    iv = [col(i_ref, h) for h in hs]
    f = [col(lb_ref, h) + (1.0 - col(lb_ref, h)) * _sigmoid(col(f_ref, h)) for h in hs]
    kx = [1.0 - f[h] for h in hs]
    gc = [_select_sum(ltri, jnp.log(f[h])) for h in hs]
    st = [st_ref[h] for h in hs]
    mode = HGRN_MODE
    o_inter = [_mm(q[h] * jnp.exp(gc[h]), st[h], _NT, mode) for h in hs]
    for h in hs:
        g_last = gc[h][c - 1:c, :]
        st_ref[h] = st[h] * jnp.exp(g_last) + _mm(iv[h], kx[h] * jnp.exp(g_last - gc[h]), _TN, mode)
    attn = []
    for h in hs:
        blocks = []
        for b in range(nb):
            rows = slice(b * sub, (b + 1) * sub)
            qb, gb = q[h][rows, :], gc[h][rows, :]
            if b == 0:
                blk = jnp.zeros((sub, c), F32)
            else:
                ref = gc[h][b * sub:b * sub + 1, :]
                k_sc = kx[h] * jnp.exp(jnp.minimum(ref - gc[h], 0.0))
                blk = _mm(qb * jnp.exp(gb - ref), k_sc, _NT, mode)
            for s in range(b * sub, (b + 1) * sub):
                dec = jnp.exp(jnp.minimum(gb - gc[h][s:s + 1, :], 0.0))
                blk = jnp.where(cs == s, jnp.sum(qb * kx[h][s:s + 1, :] * dec, axis=-1, keepdims=True), blk)
            blocks.append(blk)
        attn.append(jnp.where(incl, jnp.concatenate(blocks, axis=0), 0.0))
    outs = []
    for h in hs:
        o = _mm(attn[h], iv[h], _NN, mode) + o_inter[h]
        on = o * lax.rsqrt(jnp.mean(o * o, axis=-1, keepdims=True) + NORM_EPS) * col(gain_ref, h)
        g = col(g_ref, h)
        outs.append(on * (g * _sigmoid(g)))
    o_ref[...] = jnp.concatenate(outs, axis=1).astype(o_ref.dtype)


def hgrn_mix(proj, hg_off, lb, gain, batch, lp):
    tp = proj.shape[0]
    hw = lb.shape[0]
    dk = HGRN_HEAD_DIM
    nh = hw // dk
    c = CHUNK
    nc = lp // c
    heads = _pick(nh, (HGRN_HEADS_PER_STEP, 2, 1))
    gw = heads * dk
    blk = lambda part: pl.BlockSpec((c, gw), lambda b, h, ch: (b * nc + ch, (hg_off + part * hw) // gw + h))
    vec = pl.BlockSpec((1, gw), lambda b, h, ch: (0, h))
    return pl.pallas_call(
        functools.partial(_hgrn_body, c=c, heads=heads, sub=HGRN_SUB),
        grid=(batch, nh // heads, nc),
        in_specs=[blk(0), blk(1), blk(2), blk(3), vec, vec],
        out_specs=pl.BlockSpec((c, gw), lambda b, h, ch: (b * nc + ch, h)),
        out_shape=jax.ShapeDtypeStruct((tp, hw), BF16),
        scratch_shapes=[pltpu.VMEM((heads, dk, dk), F32)],
        compiler_params=_cparams(("parallel", "parallel", "arbitrary")),
        name="hgrn_mix",
    )(proj, proj, proj, proj, lb.reshape(1, hw), gain.reshape(1, hw))


def _out_proj_body(h_ref, ya_ref, yb_ref, yc_ref, wa_ref, wb_ref, wc_ref, o_ref):
    o_ref[...] = (h_ref[...] + _dot(ya_ref[...], wa_ref[...]) + _dot(yb_ref[...], wb_ref[...])
                  + _dot(yc_ref[...], wc_ref[...]))


def out_proj(h, ya, yb, yc, w_bf16):
    tp, d = h.shape
    wa, wb, wc = ya.shape[1], yb.shape[1], yc.shape[1]
    tm = _pick(tp, (512, 256, 128))
    tn = _pick(d, (1024, 512, 256, 128))
    return pl.pallas_call(
        _out_proj_body,
        grid=(tp // tm, d // tn),
        in_specs=[
            pl.BlockSpec((tm, tn), lambda i, j: (i, j)),
            pl.BlockSpec((tm, wa), lambda i, j: (i, 0)),
            pl.BlockSpec((tm, wb), lambda i, j: (i, 0)),
            pl.BlockSpec((tm, wc), lambda i, j: (i, 0)),
            pl.BlockSpec((wa, tn), lambda i, j: (0, j)),
            pl.BlockSpec((wb, tn), lambda i, j: (0, j)),
            pl.BlockSpec((wc, tn), lambda i, j: (0, j)),
        ],
        out_specs=pl.BlockSpec((tm, tn), lambda i, j: (i, j)),
        out_shape=jax.ShapeDtypeStruct((tp, d), F32),
        compiler_params=_cparams(("parallel", "parallel")),
        name="out_proj",
    )(h, ya, yb, yc, w_bf16[:wa], w_bf16[wa:wa + wb], w_bf16[wa + wb:])


def _norm_t_body(h_ref, g_ref, o_ref):
    x = h_ref[...]
    ms = jnp.mean(x * x, axis=-1, keepdims=True)
    y = x * lax.rsqrt(ms + NORM_EPS) * g_ref[...]
    o_ref[...] = y.T.astype(o_ref.dtype)


def norm_transposed(h, gain):
    tp, d = h.shape
    tm = _pick(tp, (512, 256, 128))
    return pl.pallas_call(
        _norm_t_body,
        grid=(tp // tm,),
        in_specs=[pl.BlockSpec((tm, d), lambda i: (i, 0)), pl.BlockSpec((1, d), lambda i: (0, 0))],
        out_specs=pl.BlockSpec((None, d, tm), lambda i: (i, 0, 0)),
        out_shape=jax.ShapeDtypeStruct((tp // tm, d, tm), BF16),
        compiler_params=_cparams(("parallel",)),
        name="peer_norm_t",
    )(h, gain.reshape(1, d))


def _top_values(arrays, k):
    rows = lax.broadcasted_iota(jnp.int32, (k, 1), 0)
    arrays = list(arrays)
    outs = [jnp.zeros((k, s.shape[1]), F32) for s in arrays]
    for r in range(k):
        ms = [jnp.max(s, axis=0, keepdims=True) for s in arrays]
        outs = [jnp.where(rows == r, m, o) for m, o in zip(ms, outs)]
        arrays = [jnp.where(s == m, -jnp.inf, s) for s, m in zip(arrays, ms)]
    return outs


def _pair_candidates(v1, v2, topk):
    sub8 = lax.broadcasted_iota(jnp.int32, (8, 1), 0)
    parts = [v1[0:1, :] + v2]
    i = 1
    while topk // (i + 1) > 1:
        n_j = topk // (i + 1)
        parts.append(jnp.where(sub8 < n_j, v1[i:i + 1, :] + v2[0:8, :], -jnp.inf))
        i += 1
    parts.append(v1[i:, :] + v2[0:1, :])
    return jnp.concatenate(parts, axis=0)


def _peer_route_body(wq_ref, zn_ref, keys_ref, s1_ref, s2_ref, e1_ref, e2_ref, thr_ref, *, half, topk, hp):
    qt = _dot(wq_ref[...], zn_ref[...])
    hs = range(hp)
    s1 = [_dot(keys_ref[h, 0], qt[2 * h * half:(2 * h + 1) * half], HI) for h in hs]
    s2 = [_dot(keys_ref[h, 1], qt[(2 * h + 1) * half:(2 * h + 2) * half], HI) for h in hs]
    tops = _top_values(s1 + s2, topk)
    v1, v2 = tops[:hp], tops[hp:]
    best = _top_values([_pair_candidates(v1[h], v2[h], topk) for h in hs], topk)
    for h in hs:
        mx = best[h][0:1, :]
        z = jnp.sum(jnp.exp(best[h] - mx), axis=0, keepdims=True)
        s1_ref[h] = s1[h]
        s2_ref[h] = s2[h]
        e1_ref[h] = jnp.exp(s1[h] - v1[h][0:1, :]) / z
        e2_ref[h] = jnp.exp(s2[h] - v2[h][0:1, :])
        thr_ref[h] = best[h][topk - 1:topk, :]


def peer_route(znt, wqt_bf16, keys):
    nt, d, tm = znt.shape
    nh, _, nk, half = keys.shape
    hp = _pick(nh, (PEER_ROUTE_HEADS_PER_STEP, 2, 1))
    sc = pl.BlockSpec((None, hp, nk, tm), lambda i, h: (i, h, 0, 0))
    big = jax.ShapeDtypeStruct((nt, nh, nk, tm), F32)
    return pl.pallas_call(
        functools.partial(_peer_route_body, half=half, topk=PEER_TOPK, hp=hp),
        grid=(nt, nh // hp),
        in_specs=[
            pl.BlockSpec((hp * 2 * half, d), lambda i, h: (h, 0)),
            pl.BlockSpec((None, d, tm), lambda i, h: (i, 0, 0)),
            pl.BlockSpec((hp, 2, nk, half), lambda i, h: (h, 0, 0, 0)),
        ],
        out_specs=[sc, sc, sc, sc, pl.BlockSpec((None, hp, 1, tm), lambda i, h: (i, h, 0, 0))],
        out_shape=[big, big, big, big, jax.ShapeDtypeStruct((nt, nh, 1, tm), F32)],
        compiler_params=_cparams(("parallel", "arbitrary")),
        name="peer_route",
    )(wqt_bf16, znt, keys)


def _gelu(x):
    return 0.5 * x * (1.0 + lax.erf(x * (2.0 ** -0.5)))


def _peer_expert_body(zn_ref, u_ref, vt_ref, s1_ref, s2_ref, e1_ref, e2_ref, thr_ref, o_ref, a0_ref, a1_ref,
                      hid0_ref, hid1_ref, *, nk, nh, nj, split):
    j = pl.program_id(1)
    te = u_ref.shape[0]
    d = vt_ref.shape[0]
    nq = te // nk
    sub = nk // split
    rows = d // (nq * split)

    def project(q, hid_w):
        hid_w[q * nk:(q + 1) * nk, :] = _dot(u_ref[q * nk:(q + 1) * nk, :], zn_ref[...])

    def activate(q, r, hid_r, a_w):
        a = (j - 1) * nq + q
        rs = slice(r * sub, (r + 1) * sub)
        gate = None
        for h in range(nh):
            cand = s1_ref[h, pl.ds(a, 1), :] + s2_ref[h, rs, :]
            gh = jnp.where(cand >= thr_ref[h], e1_ref[h, pl.ds(a, 1), :] * e2_ref[h, rs, :], 0.0)
            gate = gh if gate is None else gate + gh
        es = slice(q * nk + r * sub, q * nk + (r + 1) * sub)
        a_w[es, :] = (gate * _gelu(hid_r[es, :])).astype(BF16)

    def apply(p, a_r):
        sl = slice(p * rows, (p + 1) * rows)
        o_ref[sl, :] += _dot(vt_ref[sl, :], a_r[...])

    def step(do_project, do_activate, do_apply, par):
        hid_w, hid_r = (hid0_ref, hid1_ref) if par == 0 else (hid1_ref, hid0_ref)
        a_w, a_r = (a1_ref, a0_ref) if par == 0 else (a0_ref, a1_ref)
        for q in range(nq):
            if do_project:
                project(q, hid_w)
            for r in range(split):
                if do_activate:
                    activate(q, r, hid_r, a_w)
                if do_apply:
                    apply(q * split + r, a_r)

    @pl.when(j == 0)
    def _():
        o_ref[...] = jnp.zeros_like(o_ref)
        step(True, False, False, 0)

    @pl.when(j == 1)
    def _():
        step(True, True, False, 1)

    for par in range(2):
        @pl.when((j >= 2) & (j < nj) & (lax.rem(j, 2) == par))
        def _(par=par):
            step(True, True, True, par)

    @pl.when(j == nj)
    def _():
        step(False, True, True, nj % 2)

    @pl.when(j == nj + 1)
    def _():
        step(False, False, True, (nj + 1) % 2)


def peer_experts(znt, u_bf16, v_bf16, s1, s2, e1, e2, thr):
    nt, d, tm = znt.shape
    ne = u_bf16.shape[0]
    _, nh, nk, _ = s1.shape
    te = _pick(ne, (512, 256, 128))
    nj = ne // te
    assert nj >= 2
    vt_bf16 = v_bf16.reshape(nj, te, d).transpose(0, 2, 1)
    once = dict(pipeline_mode=pl.Buffered(1))
    aux = pl.BlockSpec((None, nh, nk, tm), lambda i, j: (i, 0, 0, 0), **once)
    return pl.pallas_call(
        functools.partial(_peer_expert_body, nk=nk, nh=nh, nj=nj, split=PEER_SPLIT),
        grid=(nt, nj + 2),
        in_specs=[
            pl.BlockSpec((None, d, tm), lambda i, j: (i, 0, 0), **once),
            pl.BlockSpec((te, d), lambda i, j: (jnp.minimum(j, nj - 1), 0)),
            pl.BlockSpec((None, d, te), lambda i, j: (jnp.clip(j - 2, 0, nj - 1), 0, 0)),
            aux, aux, aux, aux,
            pl.BlockSpec((None, nh, 1, tm), lambda i, j: (i, 0, 0, 0), **once),
        ],
        out_specs=pl.BlockSpec((None, d, tm), lambda i, j: (i, 0, 0)),
        out_shape=jax.ShapeDtypeStruct((nt, d, tm), F32),
        scratch_shapes=[pltpu.VMEM((te, tm), BF16), pltpu.VMEM((te, tm), BF16),
                        pltpu.VMEM((te, tm), F32), pltpu.VMEM((te, tm), F32)],
        compiler_params=_cparams(("parallel", "arbitrary")),
        name="peer_experts",
    )(znt, u_bf16, vt_bf16, s1, s2, e1, e2, thr)


def _residual_t_body(h_ref, yt_ref, o_ref):
    o_ref[...] = h_ref[...] + yt_ref[...].T


def residual_add_transposed(h, yt):
    tp, d = h.shape
    tm = yt.shape[2]
    return pl.pallas_call(
        _residual_t_body,
        grid=(tp // tm,),
        in_specs=[pl.BlockSpec((tm, d), lambda i: (i, 0)), pl.BlockSpec((None, d, tm), lambda i: (i, 0, 0))],
        out_specs=pl.BlockSpec((tm, d), lambda i: (i, 0)),
        out_shape=jax.ShapeDtypeStruct((tp, d), F32),
        compiler_params=_cparams(("parallel",)),
        name="peer_residual",
    )(h, yt)


def _final_norm_body(h_ref, g_ref, o_ref):
    x = h_ref[...]
    ms = jnp.mean(x * x, axis=-1, keepdims=True)
    o_ref[...] = x * lax.rsqrt(ms + NORM_EPS) * g_ref[...]


def final_norm(h, gain, batch, lp):
    tp, d = h.shape
    nb = lp // SWA_BLOCK
    seq = lp - SWA_BLOCK
    out = pl.pallas_call(
        _final_norm_body,
        grid=(batch, nb - 1),
        in_specs=[pl.BlockSpec((SWA_BLOCK, d), lambda b, n: (b * nb + n + 1, 0)),
                  pl.BlockSpec((1, d), lambda b, n: (0, 0))],
        out_specs=pl.BlockSpec((SWA_BLOCK, d), lambda b, n: (b * (nb - 1) + n, 0)),
        out_shape=jax.ShapeDtypeStruct((batch * seq, d), F32),
        compiler_params=_cparams(("parallel", "parallel")),
        name="final_norm",
    )(h, gain.reshape(1, d))
    return out.reshape(batch, seq, d)


def _rope_tables(lp, width, rot_heads):
    pos = jnp.maximum(jnp.arange(lp, dtype=F32) - PAD_FRONT, 0.0)
    inv_freq = ROPE_THETA ** (-jnp.arange(0, ROPE_DIMS, 2, dtype=F32) / ROPE_DIMS)
    ang = pos[:, None] * inv_freq[None, :]
    cos, sin = jnp.cos(ang), jnp.sin(ang)
    half = ROPE_DIMS // 2
    rest = HEAD_DIM - ROPE_DIMS
    one = jnp.ones((lp, rest), F32)
    zero = jnp.zeros((lp, rest), F32)
    zh = jnp.zeros((lp, half), F32)
    c_head = jnp.concatenate([cos, cos, one], axis=1)
    a_head = jnp.concatenate([-sin, zh, zero], axis=1)
    b_head = jnp.concatenate([zh, sin, zero], axis=1)
    n_heads = width // HEAD_DIM
    idn = jnp.ones((lp, HEAD_DIM), F32)
    zer = jnp.zeros((lp, HEAD_DIM), F32)
    build = lambda rot, flat: jnp.concatenate([rot if i < rot_heads else flat for i in range(n_heads)], axis=1)
    return build(c_head, idn), build(a_head, zer), build(b_head, zer)


def _layout(d, rc, sw, kvw, hw):
    kvb = _roundup(2 * kvw, LANES)
    hg_off = _roundup(rc, hw)
    q_off = _roundup(hg_off + 4 * hw, sw)
    kv_off = _roundup(q_off + sw, kvb)
    total = _roundup(kv_off + kvb, 512)
    return q_off, kv_off, kvb, hg_off, total


def kernel(x, meta_tokens, norm_mix, norm_ffn, norm_final, w_in, w_out, rwkv_mu, rwkv_w0, rwkv_w2, rwkv_a0,
           rwkv_a2, rwkv_g2, rwkv_k_k, rwkv_k_a, rwkv_r_k, rwkv_lnx_w, rwkv_lnx_b, swa_sinks, hgrn_lb, hgrn_norm,
           peer_wq, peer_keys, peer_u, peer_v):
    batch, seq, d = x.shape
    depth = w_in.shape[0]
    lp = seq + SWA_BLOCK
    rw = rwkv_w0.shape[1]
    rc = rwkv_mu.shape[1]
    n_q = swa_sinks.shape[1]
    sw = n_q * HEAD_DIM
    hw = hgrn_lb.shape[1]
    kvw = (w_in.shape[2] - rc - sw - 4 * hw) // 2
    n_kv = kvw // HEAD_DIM
    q_off, kv_off, kvb, hg_off, n_tot = _layout(d, rc, sw, kvw, hw)

    pad = jnp.zeros((batch, PAD_FRONT, d), x.dtype)
    meta = jnp.broadcast_to(meta_tokens.astype(x.dtype)[None], (batch, N_META, d))
    h = jnp.concatenate([pad, meta, x], axis=1).reshape(batch * lp, d)

    q_tabs = _rope_tables(lp, sw // n_kv, sw // n_kv // HEAD_DIM)
    kv_tabs = _rope_tables(lp, kvb, n_kv)
    lb_all = jnp.cumsum(jax.nn.softmax(hgrn_lb.astype(F32), axis=0), axis=0)
    lb_all = lb_all - lb_all[0:1]

    for l in range(depth):
        w = w_in[l]
        zc = lambda n: jnp.zeros((d, n), w.dtype)
        w_re = jnp.concatenate([
            w[:, :rc], zc(hg_off - rc),
            w[:, rc + sw + 2 * kvw:], zc(q_off - hg_off - 4 * hw),
            w[:, rc:rc + sw], zc(kv_off - q_off - sw),
            w[:, rc + sw:rc + sw + 2 * kvw], zc(n_tot - kv_off - 2 * kvw)], axis=1).astype(BF16)
        proj = norm_matmul(h, norm_mix[l], w_re, lp)

        rp = dict(mu=rwkv_mu[l], w0=rwkv_w0[l], w2=rwkv_w2[l], a0=rwkv_a0[l], a2=rwkv_a2[l], g2=rwkv_g2[l],
                  k_k=rwkv_k_k[l], k_a=rwkv_k_a[l], r_k=rwkv_r_k[l], lnx_w=rwkv_lnx_w[l], lnx_b=rwkv_lnx_b[l])
        y_a = rwkv_scan(*rwkv_prep(proj, lp, rp), batch, lp)

        kv = swa_kv_rope(proj, kv_off, kvb, kv_tabs, batch, lp)
        kvh = jnp.stack([jnp.concatenate([kv[:, i * HEAD_DIM:(i + 1) * HEAD_DIM],
                                          kv[:, kvw + i * HEAD_DIM:kvw + (i + 1) * HEAD_DIM]], axis=1)
                         for i in range(n_kv)])
        y_b = swa_attention(proj, q_off, kvh, swa_sinks[l], q_tabs, batch, lp, n_q, n_kv)

        y_c = hgrn_mix(proj, hg_off, lb_all[l], hgrn_norm[l], batch, lp)

        h = out_proj(h, y_a, y_b, y_c, w_out[l].astype(BF16))

        znt = norm_transposed(h, norm_ffn[l])
        s1, s2, e1, e2, thr = peer_route(znt, peer_wq[l].T.astype(BF16), peer_keys[l])
        yt = peer_experts(znt, peer_u[l].astype(BF16), peer_v[l].astype(BF16), s1, s2, e1, e2, thr)
        h = residual_add_transposed(h, yt)

    return final_norm(h, norm_final, batch, lp)
```

```python
import functools

import jax
import jax.numpy as jnp
from jax import lax
from jax.experimental import pallas as pl
from jax.experimental.pallas import tpu as pltpu

F32 = jnp.float32
BF16 = jnp.bfloat16
HI = lax.Precision.HIGHEST

N_META = 16
HEAD_DIM = 64
NORM_EPS = 1e-5
NEG_INF = -1e30
RWKV_GN_EPS = 64e-5
SWA_BLOCK = 128
PAD_FRONT = SWA_BLOCK - N_META
ROPE_THETA = 500000.0
ROPE_DIMS = HEAD_DIM // 4
HGRN_HEAD_DIM = 128
CHUNK = 64
RWKV_CHUNK = 64
PEER_TOPK = 16
LANES = 128
MXU_WIDTH = 256
VMEM_LIMIT_MB = 56
RWKV_HEADS_PER_STEP = 24
SWA_MODE = "bf16"
PEER_SPLIT = 4
PROJ_DTYPE = BF16
PEER_ROUTE_HEADS_PER_STEP = 4
HGRN_HEADS_PER_STEP = 8
HGRN_MODE = "bf16"
HGRN_SUB = 8
RWKV_SCAN_MODE = "bf16"


def _cparams(sem, vmem_mb=VMEM_LIMIT_MB, **kw):
    return pltpu.CompilerParams(dimension_semantics=sem, vmem_limit_bytes=vmem_mb * 1024 * 1024, **kw)


def _pick(n, cands):
    for c in cands:
        if n % c == 0:
            return c
    raise ValueError(f"no tile in {cands} divides {n}")


def _roundup(n, m):
    return -(-n // m) * m


def _dot(a, b, prec=None):
    return jnp.dot(a, b, preferred_element_type=F32, precision=prec)


def _dot_nt(a, b, prec=None):
    return lax.dot_general(a, b, (((1,), (1,)), ((), ())), preferred_element_type=F32, precision=prec)


def _dot_tn(a, b, prec=None):
    return lax.dot_general(a, b, (((0,), (0,)), ((), ())), preferred_element_type=F32, precision=prec)


def _row_in_batch(i, tm, lp):
    pos0 = lax.rem(i * tm, lp)
    rb = pos0 + lax.broadcasted_iota(jnp.int32, (tm, 1), 0)
    return jnp.where(rb >= lp, rb - lp, rb)


def _sigmoid(x):
    return 1.0 / (1.0 + jnp.exp(-x))


def _norm_mm_body(x_ref, g_ref, w_ref, o_ref, z_ref, *, lp, tm):
    i = pl.program_id(0)

    @pl.when(pl.program_id(1) == 0)
    def _():
        x = x_ref[...]
        ms = jnp.mean(x * x, axis=-1, keepdims=True)
        y = x * lax.rsqrt(ms + NORM_EPS) * g_ref[...]
        z_ref[...] = jnp.where(_row_in_batch(i, tm, lp) >= PAD_FRONT, y, 0.0).astype(BF16)

    o_ref[...] = _dot(z_ref[...], w_ref[...]).astype(o_ref.dtype)


def norm_matmul(h, gain, w_bf16, lp):
    tp, d = h.shape
    n = w_bf16.shape[1]
    tm = _pick(tp, (512, 256, 128))
    tn = _pick(n, (1024, 512, 256, 128))
    return pl.pallas_call(
        functools.partial(_norm_mm_body, lp=lp, tm=tm),
        grid=(tp // tm, n // tn),
        in_specs=[
            pl.BlockSpec((tm, d), lambda i, j: (i, 0)),
            pl.BlockSpec((1, d), lambda i, j: (0, 0)),
            pl.BlockSpec((d, tn), lambda i, j: (0, j)),
        ],
        out_specs=pl.BlockSpec((tm, tn), lambda i, j: (i, j)),
        out_shape=jax.ShapeDtypeStruct((tp, n), PROJ_DTYPE),
        scratch_shapes=[pltpu.VMEM((tm, d), BF16)],
        compiler_params=_cparams(("parallel", "arbitrary")),
        name="norm_in_proj",
    )(h, gain.reshape(1, d), w_bf16)


def _mm(a, b, dims, mode):
    if mode == "bf16":
        return lax.dot_general(a.astype(BF16), b.astype(BF16), (dims, ((), ())), preferred_element_type=F32)
    return lax.dot_general(a, b, (dims, ((), ())), preferred_element_type=F32, precision=HI)


_NN = ((1,), (0,))
_NT = ((1,), (1,))
_TN = ((0,), (0,))


def _split3(x):
    hi = x.astype(BF16)
    r1 = x - hi.astype(F32)
    mid = r1.astype(BF16)
    lo = (r1 - mid.astype(F32)).astype(BF16)
    return hi, mid, lo


def _dot_x3(a, b):
    a_hi = a.astype(BF16)
    a_lo = (a - a_hi.astype(F32)).astype(BF16)
    b_hi = b.astype(BF16)
    b_lo = (b - b_hi.astype(F32)).astype(BF16)
    return (_dot(a_hi, b_lo) + _dot(a_lo, b_hi)) + _dot(a_hi, b_hi)


def _select_sum(sel, x, x_on_left=False):
    sel = sel.astype(BF16)
    terms = [_dot(t, sel) if x_on_left else _dot(sel, t) for t in _split3(x)]
    return (terms[2] + terms[1]) + terms[0]


def _head_sums(x):
    shift = HEAD_DIM.bit_length() - 1
    er = lax.broadcasted_iota(jnp.int32, (LANES, LANES), 0) >> shift
    ec = lax.broadcasted_iota(jnp.int32, (LANES, LANES), 1) >> shift
    ones_bd = jnp.where(er == ec, 1.0, 0.0)
    return jnp.concatenate(
        [_select_sum(ones_bd, x[:, s * LANES:(s + 1) * LANES], x_on_left=True)
         for s in range(x.shape[1] // LANES)], axis=1)


def _rwkv_prep_body(pa_ref, prev_ref, mu_ref, w0_ref, w2_ref, a0_ref, a2_ref, g2_ref, kk_ref, ka_ref, rk_ref,
                    lnw_ref, lnb_ref, out_ref, post_ref, dl_ref, *, lp, tm, rw, rd, ri, rg, c):
    i = pl.program_id(0)
    x = pa_ref[...].astype(F32)
    rowi = lax.broadcasted_iota(jnp.int32, (tm, 1), 0)
    pr = prev_ref.shape[0]
    prev_row = jnp.where(lax.rem(i * tm, lp) == 0, 0.0, prev_ref[...].astype(F32)[pr - 1:pr, :])
    prev = jnp.where(rowi == 0, prev_row, pltpu.roll(x, 1, 0))
    xs = x + (prev - x) * mu_ref[...]
    r = xs[:, 0:rw]
    k = xs[:, rw:2 * rw]
    v = xs[:, 2 * rw:3 * rw]
    wl = xs[:, 3 * rw:3 * rw + rd]
    al = xs[:, 3 * rw + rd:3 * rw + rd + ri]
    gl = xs[:, 3 * rw + rd + ri:3 * rw + rd + ri + rg]
    wv = w0_ref[...] + _dot_x3(jnp.tanh(wl), w2_ref[...])
    nwv = -wv
    softplus = jnp.maximum(nwv, 0.0) + jnp.log(1.0 + jnp.exp(-jnp.abs(nwv)))
    logw = -jnp.exp(-softplus - 0.5)
    alr = _sigmoid(a0_ref[...] + _dot_x3(al, a2_ref[...]))
    g = _dot_x3(_sigmoid(gl), g2_ref[...])
    kkr = k * kk_ref[...]
    kk = kkr / jnp.maximum(jnp.sqrt(_head_sums(kkr * kkr)), 1e-12)
    k2 = k * (1.0 + (alr - 1.0) * ka_ref[...])
    shift = c.bit_length() - 1
    tr = lax.broadcasted_iota(jnp.int32, (tm, tm), 0)
    tc = lax.broadcasted_iota(jnp.int32, (tm, tm), 1)
    ltri = jnp.where(tc <= tr, jnp.where((tr >> shift) == (tc >> shift), 1.0, 0.0), 0.0)
    cum = _select_sum(ltri, logw)
    tot = jnp.concatenate(
        [jnp.broadcast_to(cum[(q + 1) * c - 1:(q + 1) * c, :], (c, rw)) for q in range(tm // c)], axis=0)
    p_inv = jnp.exp(-cum)
    p_rem = jnp.exp(tot - cum)
    kb = kk * alr
    od = out_ref.dtype
    out_ref[0] = (-kk * jnp.exp(cum - logw)).astype(od)
    out_ref[1] = (r * jnp.exp(cum)).astype(od)
    out_ref[2] = (kb * p_inv).astype(od)
    out_ref[3] = (k2 * p_inv).astype(od)
    out_ref[4] = v.astype(od)
    out_ref[5] = (kb * p_rem).astype(od)
    out_ref[6] = (k2 * p_rem).astype(od)
    post_ref[0] = lnw_ref[...] * g
    post_ref[1] = (lnb_ref[...] + _head_sums(r * k2 * rk_ref[...]) * v) * g
    for q in range(tm // c):
        dl_ref[q] = jnp.exp(tot[q * c:q * c + 1, :])


def rwkv_prep(proj, lp, p):
    tp = proj.shape[0]
    rw = p["w0"].shape[0]
    rd, ri, rg = p["w2"].shape[0], p["a2"].shape[0], p["g2"].shape[0]
    rc = 3 * rw + rd + ri + rg
    tm = 128
    c = RWKV_CHUNK
    pr = 8 * 4 // proj.dtype.itemsize
    row = lambda a: a.reshape(1, -1)
    full = lambda a: pl.BlockSpec(a.shape, lambda i: (0, 0))
    args = [row(p["mu"]), row(p["w0"]), p["w2"], row(p["a0"]), p["a2"], p["g2"], row(p["k_k"]), row(p["k_a"]),
            row(p["r_k"]), row(p["lnx_w"]), row(p["lnx_b"])]
    return pl.pallas_call(
        functools.partial(_rwkv_prep_body, lp=lp, tm=tm, rw=rw, rd=rd, ri=ri, rg=rg, c=c),
        grid=(tp // tm,),
        in_specs=[
            pl.BlockSpec((tm, rc), lambda i: (i, 0)),
            pl.BlockSpec((pr, rc), lambda i: (jnp.maximum(i * (tm // pr) - 1, 0), 0)),
        ] + [full(a) for a in args],
        out_specs=[pl.BlockSpec((7, tm, rw), lambda i: (0, i, 0)),
                   pl.BlockSpec((2, tm, rw), lambda i: (0, i, 0)),
                   pl.BlockSpec((tm // c, 1, rw), lambda i: (i, 0, 0))],
        out_shape=[jax.ShapeDtypeStruct((7, tp, rw), BF16 if RWKV_SCAN_MODE == "bf16" else F32),
                   jax.ShapeDtypeStruct((2, tp, rw), F32), jax.ShapeDtypeStruct((tp // c, 1, rw), F32)],
        compiler_params=_cparams(("parallel",)),
        name="rwkv_prep",
    )(proj, proj, *args)


def _rwkv_scan_body(x_ref, post_ref, dl_ref, o_ref, s_ref, *, heads, c, mode):
    @pl.when(pl.program_id(2) == 0)
    def _():
        s_ref[...] = jnp.zeros_like(s_ref)

    ri = lax.broadcasted_iota(jnp.int32, (c, c), 0)
    ci = lax.broadcasted_iota(jnp.int32, (c, c), 1)
    strict = ci < ri
    incl = ci <= ri
    hs = range(heads)
    ld = lambda q, h: x_ref[q, :, h * HEAD_DIM:(h + 1) * HEAD_DIM]
    cat0 = lambda a, b: jnp.concatenate([a, b], axis=0)
    cat1 = lambda a, b: jnp.concatenate([a, b], axis=1)
    ar = [cat0(ld(0, h), ld(1, h)) for h in hs]
    m = [_mm(ar[h], cat0(ld(2, h), ld(3, h)), _NT, mode) for h in hs]
    s0 = [s_ref[h] for h in hs]
    ars = [_mm(ar[h], s0[h], _NT, mode) for h in hs]
    v = [ld(4, h) for h in hs]
    nk = [jnp.where(strict, m[h][:c, :c], 0.0) for h in hs]
    m_ak = [jnp.where(strict, m[h][:c, c:], 0.0) for h in hs]
    m_r = [cat1(jnp.where(incl, m[h][c:, :c], 0.0), jnp.where(incl, m[h][c:, c:], 0.0)) for h in hs]
    u = [ars[h][:c] + _mm(m_ak[h], v[h], _NN, mode) for h in hs]
    steps = c.bit_length() - 1
    for it in range(steps):
        if it + 1 < steps:
            both = [_mm(nk[h], cat1(u[h], nk[h]), _NN, mode) for h in hs]
            u = [u[h] + both[h][:, :HEAD_DIM] for h in hs]
            nk = [both[h][:, HEAD_DIM:] for h in hs]
        else:
            u = [u[h] + _mm(nk[h], u[h], _NN, mode) for h in hs]
    uv = [cat0(u[h].astype(v[h].dtype), v[h]) for h in hs]
    y = [ars[h][c:] + _mm(m_r[h], uv[h], _NN, mode) for h in hs]
    for h in hs:
        dl = dl_ref[0, :, h * HEAD_DIM:(h + 1) * HEAD_DIM]
        s_ref[h] = s0[h] * dl + _mm(uv[h], cat0(ld(5, h), ld(6, h)), _TN, mode)
    outs = []
    for h in hs:
        mean = jnp.mean(y[h], axis=-1, keepdims=True)
        yc = y[h] - mean
        var = jnp.mean(yc * yc, axis=-1, keepdims=True)
        hl = slice(h * HEAD_DIM, (h + 1) * HEAD_DIM)
        outs.append(yc * lax.rsqrt(var + RWKV_GN_EPS) * post_ref[0, :, hl] + post_ref[1, :, hl])
    o_ref[...] = jnp.concatenate(outs, axis=1).astype(o_ref.dtype)


def rwkv_scan(xs, post, dl, batch, lp):
    _, tp, rw = xs.shape
    c = RWKV_CHUNK
    nc = lp // c
    heads = _pick(rw // HEAD_DIM, (RWKV_HEADS_PER_STEP, 2))
    gw = heads * HEAD_DIM
    return pl.pallas_call(
        functools.partial(_rwkv_scan_body, heads=heads, c=c, mode=RWKV_SCAN_MODE),
        grid=(batch, rw // gw, nc),
        in_specs=[pl.BlockSpec((7, c, gw), lambda b, hg, ch: (0, b * nc + ch, hg)),
                  pl.BlockSpec((2, c, gw), lambda b, hg, ch: (0, b * nc + ch, hg)),
                  pl.BlockSpec((1, 1, gw), lambda b, hg, ch: (b * nc + ch, 0, hg))],
        out_specs=pl.BlockSpec((c, gw), lambda b, hg, ch: (b * nc + ch, hg)),
        out_shape=jax.ShapeDtypeStruct((tp, rw), BF16),
        scratch_shapes=[pltpu.VMEM((heads, HEAD_DIM, HEAD_DIM), F32)],
        compiler_params=_cparams(("parallel", "parallel", "arbitrary")),
        name="rwkv_scan",
    )(xs, post, dl)


def _rope(x, cos, sa, sb):
    parts = []
    for s in range(x.shape[1] // LANES):
        sl = slice(s * LANES, (s + 1) * LANES)
        xs = x[:, sl]
        parts.append(xs * cos[:, sl] + pltpu.roll(xs, LANES - ROPE_DIMS // 2, 1) * sa[:, sl]
                     + pltpu.roll(xs, ROPE_DIMS // 2, 1) * sb[:, sl])
    return parts[0] if len(parts) == 1 else jnp.concatenate(parts, axis=1)


def _kv_rope_body(kv_ref, cos_ref, sa_ref, sb_ref, o_ref):
    o_ref[...] = _rope(kv_ref[...].astype(F32), cos_ref[...], sa_ref[...], sb_ref[...])


def swa_kv_rope(proj, kv_off, kvb, tabs, batch, lp):
    tp = proj.shape[0]
    nb = lp // SWA_BLOCK
    tab = pl.BlockSpec((SWA_BLOCK, kvb), lambda b, n: (n, 0))
    return pl.pallas_call(
        _kv_rope_body,
        grid=(batch, nb),
        in_specs=[pl.BlockSpec((SWA_BLOCK, kvb), lambda b, n: (b * nb + n, kv_off // kvb)), tab, tab, tab],
        out_specs=pl.BlockSpec((SWA_BLOCK, kvb), lambda b, n: (b * nb + n, 0)),
        out_shape=jax.ShapeDtypeStruct((tp, kvb), F32),
        compiler_params=_cparams(("parallel", "parallel")),
        name="swa_kv_rope",
    )(proj, *tabs)


def _swa_body(sink_ref, q_ref, kc_ref, kp_ref, km_ref, cos_ref, sa_ref, sb_ref, o_ref, *, group, n_kv):
    n = pl.program_id(1)
    blk = SWA_BLOCK
    mode = SWA_MODE
    qb = group * HEAD_DIM
    rows = group * blk
    hks = range(n_kv)
    qs = []
    for hk in hks:
        q = _rope(q_ref[:, hk * qb:(hk + 1) * qb].astype(F32), cos_ref[...], sa_ref[...], sb_ref[...])
        q = q * (HEAD_DIM ** -0.5)
        qs.append(jnp.concatenate([q[:, gi * HEAD_DIM:(gi + 1) * HEAD_DIM] for gi in range(group)], axis=0))
    qi = lax.broadcasted_iota(jnp.int32, (rows, blk), 0) & (blk - 1)
    kj = lax.broadcasted_iota(jnp.int32, (rows, blk), 1)
    far = 1 << 20
    cur_ok = kj + jnp.where(n >= 1, 0, far) <= qi
    prev_ok = kj > qi + jnp.where(n >= 2, 0, far)
    mi = lax.broadcasted_iota(jnp.int32, (rows, N_META), 0) & (blk - 1)
    mj = lax.broadcasted_iota(jnp.int32, (rows, N_META), 1)
    meta_ok = mj <= mi - PAD_FRONT + jnp.where(n >= 1, far, 0)
    kk, vv = slice(0, HEAD_DIM), slice(HEAD_DIM, 2 * HEAD_DIM)
    sink = [jnp.concatenate([jnp.full((blk, 1), sink_ref[hk * group + gi], F32) for gi in range(group)], axis=0)
            for hk in hks]
    s_c = [jnp.where(cur_ok, _mm(qs[hk], kc_ref[hk, :, kk], _NT, mode), NEG_INF) for hk in hks]
    s_p = [jnp.where(prev_ok, _mm(qs[hk], kp_ref[hk, :, kk], _NT, mode), NEG_INF) for hk in hks]
    s_m = [jnp.where(meta_ok, _mm(qs[hk], km_ref[hk, :, kk], _NT, mode), NEG_INF) for hk in hks]
    rmax = lambda s: jnp.max(s, axis=-1, keepdims=True)
    rsum = lambda s: jnp.sum(s, axis=-1, keepdims=True)
    mx = [jnp.maximum(jnp.maximum(rmax(s_c[hk]), rmax(s_p[hk])), jnp.maximum(rmax(s_m[hk]), sink[hk])) for hk in hks]
    p_c = [jnp.exp(s_c[hk] - mx[hk]) for hk in hks]
    p_p = [jnp.exp(s_p[hk] - mx[hk]) for hk in hks]
    p_m = [jnp.exp(s_m[hk] - mx[hk]) for hk in hks]
    den = [rsum(p_c[hk]) + rsum(p_p[hk]) + rsum(p_m[hk]) + jnp.exp(sink[hk] - mx[hk]) for hk in hks]
    acc = [_mm(p_c[hk], kc_ref[hk, :, vv], _NN, mode) + _mm(p_p[hk], kp_ref[hk, :, vv], _NN, mode)
           + _mm(p_m[hk], km_ref[hk, :, vv], _NN, mode) for hk in hks]
    out = [acc[hk] / den[hk] for hk in hks]
    o_ref[...] = jnp.concatenate([out[hk][gi * blk:(gi + 1) * blk, :] for hk in hks for gi in range(group)],
                                 axis=1).astype(o_ref.dtype)


def swa_attention(proj, q_off, kvh, sinks, qtabs, batch, lp, n_q_heads, n_kv_heads):
    tp = proj.shape[0]
    nb = lp // SWA_BLOCK
    group = n_q_heads // n_kv_heads
    qb = group * HEAD_DIM
    sw = n_q_heads * HEAD_DIM
    assert q_off % sw == 0
    tab = pl.BlockSpec((SWA_BLOCK, qb), lambda b, n: (n, 0))
    kv_spec = lambda f: pl.BlockSpec((n_kv_heads, SWA_BLOCK, LANES), f)
    return pl.pallas_call(
        functools.partial(_swa_body, group=group, n_kv=n_kv_heads),
        grid=(batch, nb),
        in_specs=[
            pl.BlockSpec(memory_space=pltpu.SMEM),
            pl.BlockSpec((SWA_BLOCK, sw), lambda b, n: (b * nb + n, q_off // sw)),
            kv_spec(lambda b, n: (0, b * nb + n, 0)),
            kv_spec(lambda b, n: (0, b * nb + jnp.maximum(n - 1, 0), 0)),
            pl.BlockSpec((n_kv_heads, N_META, LANES),
                         lambda b, n: (0, b * (lp // N_META) + PAD_FRONT // N_META, 0)),
            tab, tab, tab,
        ],
        out_specs=pl.BlockSpec((SWA_BLOCK, sw), lambda b, n: (b * nb + n, 0)),
        out_shape=jax.ShapeDtypeStruct((tp, sw), BF16),
        compiler_params=_cparams(("parallel", "parallel")),
        name="swa_attention",
    )(sinks, proj, kvh, kvh, kvh, *qtabs)


def _hgrn_body(q_ref, f_ref, i_ref, g_ref, lb_ref, gain_ref, o_ref, st_ref, *, c, heads, sub):
    @pl.when(pl.program_id(2) == 0)
    def _():
        st_ref[...] = jnp.zeros_like(st_ref)

    dk = HGRN_HEAD_DIM
    hs = range(heads)
    nb = c // sub
    ri = lax.broadcasted_iota(jnp.int32, (c, c), 0)
    ci = lax.broadcasted_iota(jnp.int32, (c, c), 1)
    incl = ci <= ri
    ltri = jnp.where(incl, 1.0, 0.0)
    cs = lax.broadcasted_iota(jnp.int32, (sub, c), 1)
    col = lambda ref, h: ref[:, h * dk:(h + 1) * dk].astype(F32)
    q = [col(q_ref, h) for h in hs]
    iv = [col(i_ref, h) for h in hs]
    f = [col(lb_ref, h) + (1.0 - col(lb_ref, h)) * _sigmoid(col(f_ref, h)) for h in hs]
    kx = [1.0 - f[h] for h in hs]
    gc = [_select_sum(ltri, jnp.log(f[h])) for h in hs]
    st = [st_ref[h] for h in hs]
    mode = HGRN_MODE
    o_inter = [_mm(q[h] * jnp.exp(gc[h]), st[h], _NT, mode) for h in hs]
    for h in hs:
        g_last = gc[h][c - 1:c, :]
        st_ref[h] = st[h] * jnp.exp(g_last) + _mm(iv[h], kx[h] * jnp.exp(g_last - gc[h]), _TN, mode)
    attn = []
    for h in hs:
        blocks = []
        for b in range(nb):
            rows = slice(b * sub, (b + 1) * sub)
            qb, gb = q[h][rows, :], gc[h][rows, :]
            if b == 0:
                blk = jnp.zeros((sub, c), F32)
            else:
                ref = gc[h][b * sub:b * sub + 1, :]
                k_sc = kx[h] * jnp.exp(jnp.minimum(ref - gc[h], 0.0))
                blk = _mm(qb * jnp.exp(gb - ref), k_sc, _NT, mode)
            for s in range(b * sub, (b + 1) * sub):
                dec = jnp.exp(jnp.minimum(gb - gc[h][s:s + 1, :], 0.0))
                blk = jnp.where(cs == s, jnp.sum(qb * kx[h][s:s + 1, :] * dec, axis=-1, keepdims=True), blk)
            blocks.append(blk)
        attn.append(jnp.where(incl, jnp.concatenate(blocks, axis=0), 0.0))
    outs = []
    for h in hs:
        o = _mm(attn[h], iv[h], _NN, mode) + o_inter[h]
        on = o * lax.rsqrt(jnp.mean(o * o, axis=-1, keepdims=True) + NORM_EPS) * col(gain_ref, h)
        g = col(g_ref, h)
        outs.append(on * (g * _sigmoid(g)))
    o_ref[...] = jnp.concatenate(outs, axis=1).astype(o_ref.dtype)


def hgrn_mix(proj, hg_off, lb, gain, batch, lp):
    tp = proj.shape[0]
    hw = lb.shape[0]
    dk = HGRN_HEAD_DIM
    nh = hw // dk
    c = CHUNK
    nc = lp // c
    heads = _pick(nh, (HGRN_HEADS_PER_STEP, 2, 1))
    gw = heads * dk
    blk = lambda part: pl.BlockSpec((c, gw), lambda b, h, ch: (b * nc + ch, (hg_off + part * hw) // gw + h))
    vec = pl.BlockSpec((1, gw), lambda b, h, ch: (0, h))
    return pl.pallas_call(
        functools.partial(_hgrn_body, c=c, heads=heads, sub=HGRN_SUB),
        grid=(batch, nh // heads, nc),
        in_specs=[blk(0), blk(1), blk(2), blk(3), vec, vec],
        out_specs=pl.BlockSpec((c, gw), lambda b, h, ch: (b * nc + ch, h)),
        out_shape=jax.ShapeDtypeStruct((tp, hw), BF16),
        scratch_shapes=[pltpu.VMEM((heads, dk, dk), F32)],
        compiler_params=_cparams(("parallel", "parallel", "arbitrary")),
        name="hgrn_mix",
    )(proj, proj, proj, proj, lb.reshape(1, hw), gain.reshape(1, hw))


def _out_proj_body(h_ref, ya_ref, yb_ref, yc_ref, wa_ref, wb_ref, wc_ref, o_ref):
    o_ref[...] = (h_ref[...] + _dot(ya_ref[...], wa_ref[...]) + _dot(yb_ref[...], wb_ref[...])
                  + _dot(yc_ref[...], wc_ref[...]))


def out_proj(h, ya, yb, yc, w_bf16):
    tp, d = h.shape
    wa, wb, wc = ya.shape[1], yb.shape[1], yc.shape[1]
    tm = _pick(tp, (512, 256, 128))
    tn = _pick(d, (1024, 512, 256, 128))
    return pl.pallas_call(
        _out_proj_body,
        grid=(tp // tm, d // tn),
        in_specs=[
            pl.BlockSpec((tm, tn), lambda i, j: (i, j)),
            pl.BlockSpec((tm, wa), lambda i, j: (i, 0)),
            pl.BlockSpec((tm, wb), lambda i, j: (i, 0)),
            pl.BlockSpec((tm, wc), lambda i, j: (i, 0)),
            pl.BlockSpec((wa, tn), lambda i, j: (0, j)),
            pl.BlockSpec((wb, tn), lambda i, j: (0, j)),
            pl.BlockSpec((wc, tn), lambda i, j: (0, j)),
        ],
        out_specs=pl.BlockSpec((tm, tn), lambda i, j: (i, j)),
        out_shape=jax.ShapeDtypeStruct((tp, d), F32),
        compiler_params=_cparams(("parallel", "parallel")),
        name="out_proj",
    )(h, ya, yb, yc, w_bf16[:wa], w_bf16[wa:wa + wb], w_bf16[wa + wb:])


def _norm_t_body(h_ref, g_ref, o_ref):
    x = h_ref[...]
    ms = jnp.mean(x * x, axis=-1, keepdims=True)
    y = x * lax.rsqrt(ms + NORM_EPS) * g_ref[...]
    o_ref[...] = y.T.astype(o_ref.dtype)


def norm_transposed(h, gain):
    tp, d = h.shape
    tm = _pick(tp, (512, 256, 128))
    return pl.pallas_call(
        _norm_t_body,
        grid=(tp // tm,),
        in_specs=[pl.BlockSpec((tm, d), lambda i: (i, 0)), pl.BlockSpec((1, d), lambda i: (0, 0))],
        out_specs=pl.BlockSpec((None, d, tm), lambda i: (i, 0, 0)),
        out_shape=jax.ShapeDtypeStruct((tp // tm, d, tm), BF16),
        compiler_params=_cparams(("parallel",)),
        name="peer_norm_t",
    )(h, gain.reshape(1, d))


def _top_values(arrays, k):
    rows = lax.broadcasted_iota(jnp.int32, (k, 1), 0)
    arrays = list(arrays)
    outs = [jnp.zeros((k, s.shape[1]), F32) for s in arrays]
    for r in range(k):
        ms = [jnp.max(s, axis=0, keepdims=True) for s in arrays]
        outs = [jnp.where(rows == r, m, o) for m, o in zip(ms, outs)]
        arrays = [jnp.where(s == m, -jnp.inf, s) for s, m in zip(arrays, ms)]
    return outs


def _pair_candidates(v1, v2, topk):
    sub8 = lax.broadcasted_iota(jnp.int32, (8, 1), 0)
    parts = [v1[0:1, :] + v2]
    i = 1
    while topk // (i + 1) > 1:
        n_j = topk // (i + 1)
        parts.append(jnp.where(sub8 < n_j, v1[i:i + 1, :] + v2[0:8, :], -jnp.inf))
        i += 1
    parts.append(v1[i:, :] + v2[0:1, :])
    return jnp.concatenate(parts, axis=0)


def _peer_route_body(wq_ref, zn_ref, keys_ref, s1_ref, s2_ref, e1_ref, e2_ref, thr_ref, *, half, topk, hp):
    qt = _dot(wq_ref[...], zn_ref[...])
    hs = range(hp)
    s1 = [_dot(keys_ref[h, 0], qt[2 * h * half:(2 * h + 1) * half], HI) for h in hs]
    s2 = [_dot(keys_ref[h, 1], qt[(2 * h + 1) * half:(2 * h + 2) * half], HI) for h in hs]
    tops = _top_values(s1 + s2, topk)
    v1, v2 = tops[:hp], tops[hp:]
    best = _top_values([_pair_candidates(v1[h], v2[h], topk) for h in hs], topk)
    for h in hs:
        mx = best[h][0:1, :]
        z = jnp.sum(jnp.exp(best[h] - mx), axis=0, keepdims=True)
        s1_ref[h] = s1[h]
        s2_ref[h] = s2[h]
        e1_ref[h] = jnp.exp(s1[h] - v1[h][0:1, :]) / z
        e2_ref[h] = jnp.exp(s2[h] - v2[h][0:1, :])
        thr_ref[h] = best[h][topk - 1:topk, :]


def peer_route(znt, wqt_bf16, keys):
    nt, d, tm = znt.shape
    nh, _, nk, half = keys.shape
    hp = _pick(nh, (PEER_ROUTE_HEADS_PER_STEP, 2, 1))
    sc = pl.BlockSpec((None, hp, nk, tm), lambda i, h: (i, h, 0, 0))
    big = jax.ShapeDtypeStruct((nt, nh, nk, tm), F32)
    return pl.pallas_call(
        functools.partial(_peer_route_body, half=half, topk=PEER_TOPK, hp=hp),
        grid=(nt, nh // hp),
        in_specs=[
            pl.BlockSpec((hp * 2 * half, d), lambda i, h: (h, 0)),
            pl.BlockSpec((None, d, tm), lambda i, h: (i, 0, 0)),
            pl.BlockSpec((hp, 2, nk, half), lambda i, h: (h, 0, 0, 0)),
        ],
        out_specs=[sc, sc, sc, sc, pl.BlockSpec((None, hp, 1, tm), lambda i, h: (i, h, 0, 0))],
        out_shape=[big, big, big, big, jax.ShapeDtypeStruct((nt, nh, 1, tm), F32)],
        compiler_params=_cparams(("parallel", "arbitrary")),
        name="peer_route",
    )(wqt_bf16, znt, keys)


def _gelu(x):
    return 0.5 * x * (1.0 + lax.erf(x * (2.0 ** -0.5)))


def _peer_expert_body(zn_ref, u_ref, vt_ref, s1_ref, s2_ref, e1_ref, e2_ref, thr_ref, o_ref, a0_ref, a1_ref,
                      hid0_ref, hid1_ref, *, nk, nh, nj, split):
    j = pl.program_id(1)
    te = u_ref.shape[0]
    d = vt_ref.shape[0]
    nq = te // nk
    sub = nk // split
    rows = d // (nq * split)

    def project(q, hid_w):
        hid_w[q * nk:(q + 1) * nk, :] = _dot(u_ref[q * nk:(q + 1) * nk, :], zn_ref[...])

    def activate(q, r, hid_r, a_w):
        a = (j - 1) * nq + q
        rs = slice(r * sub, (r + 1) * sub)
        gate = None
        for h in range(nh):
            cand = s1_ref[h, pl.ds(a, 1), :] + s2_ref[h, rs, :]
            gh = jnp.where(cand >= thr_ref[h], e1_ref[h, pl.ds(a, 1), :] * e2_ref[h, rs, :], 0.0)
            gate = gh if gate is None else gate + gh
        es = slice(q * nk + r * sub, q * nk + (r + 1) * sub)
        a_w[es, :] = (gate * _gelu(hid_r[es, :])).astype(BF16)

    def apply(p, a_r):
        sl = slice(p * rows, (p + 1) * rows)
        o_ref[sl, :] += _dot(vt_ref[sl, :], a_r[...])

    def step(do_project, do_activate, do_apply, par):
        hid_w, hid_r = (hid0_ref, hid1_ref) if par == 0 else (hid1_ref, hid0_ref)
        a_w, a_r = (a1_ref, a0_ref) if par == 0 else (a0_ref, a1_ref)
        for q in range(nq):
            if do_project:
                project(q, hid_w)
            for r in range(split):
                if do_activate:
                    activate(q, r, hid_r, a_w)
                if do_apply:
                    apply(q * split + r, a_r)

    @pl.when(j == 0)
    def _():
        o_ref[...] = jnp.zeros_like(o_ref)
        step(True, False, False, 0)

    @pl.when(j == 1)
    def _():
        step(True, True, False, 1)

    for par in range(2):
        @pl.when((j >= 2) & (j < nj) & (lax.rem(j, 2) == par))
        def _(par=par):
            step(True, True, True, par)

    @pl.when(j == nj)
    def _():
        step(False, True, True, nj % 2)

    @pl.when(j == nj + 1)
    def _():
        step(False, False, True, (nj + 1) % 2)


def peer_experts(znt, u_bf16, v_bf16, s1, s2, e1, e2, thr):
    nt, d, tm = znt.shape
    ne = u_bf16.shape[0]
    _, nh, nk, _ = s1.shape
    te = _pick(ne, (512, 256, 128))
    nj = ne // te
    assert nj >= 2
    vt_bf16 = v_bf16.reshape(nj, te, d).transpose(0, 2, 1)
    once = dict(pipeline_mode=pl.Buffered(1))
    aux = pl.BlockSpec((None, nh, nk, tm), lambda i, j: (i, 0, 0, 0), **once)
    return pl.pallas_call(
        functools.partial(_peer_expert_body, nk=nk, nh=nh, nj=nj, split=PEER_SPLIT),
        grid=(nt, nj + 2),
        in_specs=[
            pl.BlockSpec((None, d, tm), lambda i, j: (i, 0, 0), **once),
            pl.BlockSpec((te, d), lambda i, j: (jnp.minimum(j, nj - 1), 0)),
            pl.BlockSpec((None, d, te), lambda i, j: (jnp.clip(j - 2, 0, nj - 1), 0, 0)),
            aux, aux, aux, aux,
            pl.BlockSpec((None, nh, 1, tm), lambda i, j: (i, 0, 0, 0), **once),
        ],
        out_specs=pl.BlockSpec((None, d, tm), lambda i, j: (i, 0, 0)),
        out_shape=jax.ShapeDtypeStruct((nt, d, tm), F32),
        scratch_shapes=[pltpu.VMEM((te, tm), BF16), pltpu.VMEM((te, tm), BF16),
                        pltpu.VMEM((te, tm), F32), pltpu.VMEM((te, tm), F32)],
        compiler_params=_cparams(("parallel", "arbitrary")),
        name="peer_experts",
    )(znt, u_bf16, vt_bf16, s1, s2, e1, e2, thr)


def _residual_t_body(h_ref, yt_ref, o_ref):
    o_ref[...] = h_ref[...] + yt_ref[...].T


def residual_add_transposed(h, yt):
    tp, d = h.shape
    tm = yt.shape[2]
    return pl.pallas_call(
        _residual_t_body,
        grid=(tp // tm,),
        in_specs=[pl.BlockSpec((tm, d), lambda i: (i, 0)), pl.BlockSpec((None, d, tm), lambda i: (i, 0, 0))],
        out_specs=pl.BlockSpec((tm, d), lambda i: (i, 0)),
        out_shape=jax.ShapeDtypeStruct((tp, d), F32),
        compiler_params=_cparams(("parallel",)),
        name="peer_residual",
    )(h, yt)


def _final_norm_body(h_ref, g_ref, o_ref):
    x = h_ref[...]
    ms = jnp.mean(x * x, axis=-1, keepdims=True)
    o_ref[...] = x * lax.rsqrt(ms + NORM_EPS) * g_ref[...]


def final_norm(h, gain, batch, lp):
    tp, d = h.shape
    nb = lp // SWA_BLOCK
    seq = lp - SWA_BLOCK
    out = pl.pallas_call(
        _final_norm_body,
        grid=(batch, nb - 1),
        in_specs=[pl.BlockSpec((SWA_BLOCK, d), lambda b, n: (b * nb + n + 1, 0)),
                  pl.BlockSpec((1, d), lambda b, n: (0, 0))],
        out_specs=pl.BlockSpec((SWA_BLOCK, d), lambda b, n: (b * (nb - 1) + n, 0)),
        out_shape=jax.ShapeDtypeStruct((batch * seq, d), F32),
        compiler_params=_cparams(("parallel", "parallel")),
        name="final_norm",
    )(h, gain.reshape(1, d))
    return out.reshape(batch, seq, d)


def _rope_tables(lp, width, rot_heads):
    pos = jnp.maximum(jnp.arange(lp, dtype=F32) - PAD_FRONT, 0.0)
    inv_freq = ROPE_THETA ** (-jnp.arange(0, ROPE_DIMS, 2, dtype=F32) / ROPE_DIMS)
    ang = pos[:, None] * inv_freq[None, :]
    cos, sin = jnp.cos(ang), jnp.sin(ang)
    half = ROPE_DIMS // 2
    rest = HEAD_DIM - ROPE_DIMS
    one = jnp.ones((lp, rest), F32)
    zero = jnp.zeros((lp, rest), F32)
    zh = jnp.zeros((lp, half), F32)
    c_head = jnp.concatenate([cos, cos, one], axis=1)
    a_head = jnp.concatenate([-sin, zh, zero], axis=1)
    b_head = jnp.concatenate([zh, sin, zero], axis=1)
    n_heads = width // HEAD_DIM
    idn = jnp.ones((lp, HEAD_DIM), F32)
    zer = jnp.zeros((lp, HEAD_DIM), F32)
    build = lambda rot, flat: jnp.concatenate([rot if i < rot_heads else flat for i in range(n_heads)], axis=1)
    return build(c_head, idn), build(a_head, zer), build(b_head, zer)


def _layout(d, rc, sw, kvw, hw):
    kvb = _roundup(2 * kvw, LANES)
    hg_off = _roundup(rc, hw)
    q_off = _roundup(hg_off + 4 * hw, sw)
    kv_off = _roundup(q_off + sw, kvb)
    total = _roundup(kv_off + kvb, 512)
    return q_off, kv_off, kvb, hg_off, total


def kernel(x, meta_tokens, norm_mix, norm_ffn, norm_final, w_in, w_out, rwkv_mu, rwkv_w0, rwkv_w2, rwkv_a0,
           rwkv_a2, rwkv_g2, rwkv_k_k, rwkv_k_a, rwkv_r_k, rwkv_lnx_w, rwkv_lnx_b, swa_sinks, hgrn_lb, hgrn_norm,
           peer_wq, peer_keys, peer_u, peer_v):
    batch, seq, d = x.shape
    depth = w_in.shape[0]
    lp = seq + SWA_BLOCK
    rw = rwkv_w0.shape[1]
    rc = rwkv_mu.shape[1]
    n_q = swa_sinks.shape[1]
    sw = n_q * HEAD_DIM
    hw = hgrn_lb.shape[1]
    kvw = (w_in.shape[2] - rc - sw - 4 * hw) // 2
    n_kv = kvw // HEAD_DIM
    q_off, kv_off, kvb, hg_off, n_tot = _layout(d, rc, sw, kvw, hw)

    pad = jnp.zeros((batch, PAD_FRONT, d), x.dtype)
    meta = jnp.broadcast_to(meta_tokens.astype(x.dtype)[None], (batch, N_META, d))
    h = jnp.concatenate([pad, meta, x], axis=1).reshape(batch * lp, d)

    q_tabs = _rope_tables(lp, sw // n_kv, sw // n_kv // HEAD_DIM)
    kv_tabs = _rope_tables(lp, kvb, n_kv)
    lb_all = jnp.cumsum(jax.nn.softmax(hgrn_lb.astype(F32), axis=0), axis=0)
    lb_all = lb_all - lb_all[0:1]

    for l in range(depth):
        w = w_in[l]
        zc = lambda n: jnp.zeros((d, n), w.dtype)
        w_re = jnp.concatenate([
            w[:, :rc], zc(hg_off - rc),
            w[:, rc + sw + 2 * kvw:], zc(q_off - hg_off - 4 * hw),
            w[:, rc:rc + sw], zc(kv_off - q_off - sw),
            w[:, rc + sw:rc + sw + 2 * kvw], zc(n_tot - kv_off - 2 * kvw)], axis=1).astype(BF16)
        proj = norm_matmul(h, norm_mix[l], w_re, lp)

        rp = dict(mu=rwkv_mu[l], w0=rwkv_w0[l], w2=rwkv_w2[l], a0=rwkv_a0[l], a2=rwkv_a2[l], g2=rwkv_g2[l],
                  k_k=rwkv_k_k[l], k_a=rwkv_k_a[l], r_k=rwkv_r_k[l], lnx_w=rwkv_lnx_w[l], lnx_b=rwkv_lnx_b[l])
        y_a = rwkv_scan(*rwkv_prep(proj, lp, rp), batch, lp)

        kv = swa_kv_rope(proj, kv_off, kvb, kv_tabs, batch, lp)
        kvh = jnp.stack([jnp.concatenate([kv[:, i * HEAD_DIM:(i + 1) * HEAD_DIM],
                                          kv[:, kvw + i * HEAD_DIM:kvw + (i + 1) * HEAD_DIM]], axis=1)
                         for i in range(n_kv)])
        y_b = swa_attention(proj, q_off, kvh, swa_sinks[l], q_tabs, batch, lp, n_q, n_kv)

        y_c = hgrn_mix(proj, hg_off, lb_all[l], hgrn_norm[l], batch, lp)

        h = out_proj(h, y_a, y_b, y_c, w_out[l].astype(BF16))

        znt = norm_transposed(h, norm_ffn[l])
        s1, s2, e1, e2, thr = peer_route(znt, peer_wq[l].T.astype(BF16), peer_keys[l])
        yt = peer_experts(znt, peer_u[l].astype(BF16), peer_v[l].astype(BF16), s1, s2, e1, e2, thr)
        h = residual_add_transposed(h, yt)

    return final_norm(h, norm_final, batch, lp)
```

```python
import functools

import jax
import jax.numpy as jnp
from jax import lax
from jax.experimental import pallas as pl
from jax.experimental.pallas import tpu as pltpu

F32 = jnp.float32
BF16 = jnp.bfloat16
HI = lax.Precision.HIGHEST

N_META = 16
HEAD_DIM = 64
NORM_EPS = 1e-5
NEG_INF = -1e30
RWKV_GN_EPS = 64e-5
SWA_BLOCK = 128
PAD_FRONT = SWA_BLOCK - N_META
ROPE_THETA = 500000.0
ROPE_DIMS = HEAD_DIM // 4
HGRN_HEAD_DIM = 128
CHUNK = 64
RWKV_CHUNK = 64
PEER_TOPK = 16
LANES = 128
MXU_WIDTH = 256
VMEM_LIMIT_MB = 56
RWKV_HEADS_PER_STEP = 24
SWA_MODE = "bf16"
PEER_SPLIT = 4
PROJ_DTYPE = BF16
PEER_ROUTE_HEADS_PER_STEP = 4
HGRN_HEADS_PER_STEP = 8
HGRN_MODE = "bf16"
HGRN_SUB = 8
RWKV_SCAN_MODE = "bf16"


def _cparams(sem, vmem_mb=VMEM_LIMIT_MB, **kw):
    return pltpu.CompilerParams(dimension_semantics=sem, vmem_limit_bytes=vmem_mb * 1024 * 1024, **kw)


def _pick(n, cands):
    for c in cands:
        if n % c == 0:
            return c
    raise ValueError(f"no tile in {cands} divides {n}")


def _roundup(n, m):
    return -(-n // m) * m


def _dot(a, b, prec=None):
    return jnp.dot(a, b, preferred_element_type=F32, precision=prec)


def _dot_nt(a, b, prec=None):
    return lax.dot_general(a, b, (((1,), (1,)), ((), ())), preferred_element_type=F32, precision=prec)


def _dot_tn(a, b, prec=None):
    return lax.dot_general(a, b, (((0,), (0,)), ((), ())), preferred_element_type=F32, precision=prec)


def _row_in_batch(i, tm, lp):
    pos0 = lax.rem(i * tm, lp)
    rb = pos0 + lax.broadcasted_iota(jnp.int32, (tm, 1), 0)
    return jnp.where(rb >= lp, rb - lp, rb)


def _sigmoid(x):
    return 1.0 / (1.0 + jnp.exp(-x))


def _norm_mm_body(x_ref, g_ref, w_ref, o_ref, z_ref, *, lp, tm):
    i = pl.program_id(0)

    @pl.when(pl.program_id(1) == 0)
    def _():
        x = x_ref[...]
        ms = jnp.mean(x * x, axis=-1, keepdims=True)
        y = x * lax.rsqrt(ms + NORM_EPS) * g_ref[...]
        z_ref[...] = jnp.where(_row_in_batch(i, tm, lp) >= PAD_FRONT, y, 0.0).astype(BF16)

    o_ref[...] = _dot(z_ref[...], w_ref[...]).astype(o_ref.dtype)


def norm_matmul(h, gain, w_bf16, lp):
    tp, d = h.shape
    n = w_bf16.shape[1]
    tm = _pick(tp, (512, 256, 128))
    tn = _pick(n, (1024, 512, 256, 128))
    return pl.pallas_call(
        functools.partial(_norm_mm_body, lp=lp, tm=tm),
        grid=(tp // tm, n // tn),
        in_specs=[
            pl.BlockSpec((tm, d), lambda i, j: (i, 0)),
            pl.BlockSpec((1, d), lambda i, j: (0, 0)),
            pl.BlockSpec((d, tn), lambda i, j: (0, j)),
        ],
        out_specs=pl.BlockSpec((tm, tn), lambda i, j: (i, j)),
        out_shape=jax.ShapeDtypeStruct((tp, n), PROJ_DTYPE),
        scratch_shapes=[pltpu.VMEM((tm, d), BF16)],
        compiler_params=_cparams(("parallel", "arbitrary")),
        name="norm_in_proj",
    )(h, gain.reshape(1, d), w_bf16)


def _mm(a, b, dims, mode):
    if mode == "bf16":
        return lax.dot_general(a.astype(BF16), b.astype(BF16), (dims, ((), ())), preferred_element_type=F32)
    return lax.dot_general(a, b, (dims, ((), ())), preferred_element_type=F32, precision=HI)


_NN = ((1,), (0,))
_NT = ((1,), (1,))
_TN = ((0,), (0,))


def _split3(x):
    hi = x.astype(BF16)
    r1 = x - hi.astype(F32)
    mid = r1.astype(BF16)
    lo = (r1 - mid.astype(F32)).astype(BF16)
    return hi, mid, lo


def _dot_x3(a, b):
    a_hi = a.astype(BF16)
    a_lo = (a - a_hi.astype(F32)).astype(BF16)
    b_hi = b.astype(BF16)
    b_lo = (b - b_hi.astype(F32)).astype(BF16)
    return (_dot(a_hi, b_lo) + _dot(a_lo, b_hi)) + _dot(a_hi, b_hi)


def _select_sum(sel, x, x_on_left=False):
    sel = sel.astype(BF16)
    terms = [_dot(t, sel) if x_on_left else _dot(sel, t) for t in _split3(x)]
    return (terms[2] + terms[1]) + terms[0]


def _head_sums(x):
    shift = HEAD_DIM.bit_length() - 1
    er = lax.broadcasted_iota(jnp.int32, (LANES, LANES), 0) >> shift
    ec = lax.broadcasted_iota(jnp.int32, (LANES, LANES), 1) >> shift
    ones_bd = jnp.where(er == ec, 1.0, 0.0)
    return jnp.concatenate(
        [_select_sum(ones_bd, x[:, s * LANES:(s + 1) * LANES], x_on_left=True)
         for s in range(x.shape[1] // LANES)], axis=1)


def _rwkv_prep_body(pa_ref, prev_ref, mu_ref, w0_ref, w2_ref, a0_ref, a2_ref, g2_ref, kk_ref, ka_ref, rk_ref,
                    lnw_ref, lnb_ref, out_ref, post_ref, dl_ref, *, lp, tm, rw, rd, ri, rg, c):
    i = pl.program_id(0)
    x = pa_ref[...].astype(F32)
    rowi = lax.broadcasted_iota(jnp.int32, (tm, 1), 0)
    pr = prev_ref.shape[0]
    prev_row = jnp.where(lax.rem(i * tm, lp) == 0, 0.0, prev_ref[...].astype(F32)[pr - 1:pr, :])
    prev = jnp.where(rowi == 0, prev_row, pltpu.roll(x, 1, 0))
    xs = x + (prev - x) * mu_ref[...]
    r = xs[:, 0:rw]
    k = xs[:, rw:2 * rw]
    v = xs[:, 2 * rw:3 * rw]
    wl = xs[:, 3 * rw:3 * rw + rd]
    al = xs[:, 3 * rw + rd:3 * rw + rd + ri]
    gl = xs[:, 3 * rw + rd + ri:3 * rw + rd + ri + rg]
    wv = w0_ref[...] + _dot_x3(jnp.tanh(wl), w2_ref[...])
    nwv = -wv
    softplus = jnp.maximum(nwv, 0.0) + jnp.log(1.0 + jnp.exp(-jnp.abs(nwv)))
    logw = -jnp.exp(-softplus - 0.5)
    alr = _sigmoid(a0_ref[...] + _dot_x3(al, a2_ref[...]))
    g = _dot_x3(_sigmoid(gl), g2_ref[...])
    kkr = k * kk_ref[...]
    kk = kkr / jnp.maximum(jnp.sqrt(_head_sums(kkr * kkr)), 1e-12)
    k2 = k * (1.0 + (alr - 1.0) * ka_ref[...])
    shift = c.bit_length() - 1
    tr = lax.broadcasted_iota(jnp.int32, (tm, tm), 0)
    tc = lax.broadcasted_iota(jnp.int32, (tm, tm), 1)
    ltri = jnp.where(tc <= tr, jnp.where((tr >> shift) == (tc >> shift), 1.0, 0.0), 0.0)
    cum = _select_sum(ltri, logw)
    tot = jnp.concatenate(
        [jnp.broadcast_to(cum[(q + 1) * c - 1:(q + 1) * c, :], (c, rw)) for q in range(tm // c)], axis=0)
    p_inv = jnp.exp(-cum)
    p_rem = jnp.exp(tot - cum)
    kb = kk * alr
    od = out_ref.dtype
    out_ref[0] = (-kk * jnp.exp(cum - logw)).astype(od)
    out_ref[1] = (r * jnp.exp(cum)).astype(od)
    out_ref[2] = (kb * p_inv).astype(od)
    out_ref[3] = (k2 * p_inv).astype(od)
    out_ref[4] = v.astype(od)
    out_ref[5] = (kb * p_rem).astype(od)
    out_ref[6] = (k2 * p_rem).astype(od)
    post_ref[0] = lnw_ref[...] * g
    post_ref[1] = (lnb_ref[...] + _head_sums(r * k2 * rk_ref[...]) * v) * g
    for q in range(tm // c):
        dl_ref[q] = jnp.exp(tot[q * c:q * c + 1, :])


def rwkv_prep(proj, lp, p):
    tp = proj.shape[0]
    rw = p["w0"].shape[0]
    rd, ri, rg = p["w2"].shape[0], p["a2"].shape[0], p["g2"].shape[0]
    rc = 3 * rw + rd + ri + rg
    tm = 128
    c = RWKV_CHUNK
    pr = 8 * 4 // proj.dtype.itemsize
    row = lambda a: a.reshape(1, -1)
    full = lambda a: pl.BlockSpec(a.shape, lambda i: (0, 0))
    args = [row(p["mu"]), row(p["w0"]), p["w2"], row(p["a0"]), p["a2"], p["g2"], row(p["k_k"]), row(p["k_a"]),
            row(p["r_k"]), row(p["lnx_w"]), row(p["lnx_b"])]
    return pl.pallas_call(
        functools.partial(_rwkv_prep_body, lp=lp, tm=tm, rw=rw, rd=rd, ri=ri, rg=rg, c=c),
        grid=(tp // tm,),
        in_specs=[
            pl.BlockSpec((tm, rc), lambda i: (i, 0)),
            pl.BlockSpec((pr, rc), lambda i: (jnp.maximum(i * (tm // pr) - 1, 0), 0)),
        ] + [full(a) for a in args],
        out_specs=[pl.BlockSpec((7, tm, rw), lambda i: (0, i, 0)),
                   pl.BlockSpec((2, tm, rw), lambda i: (0, i, 0)),
                   pl.BlockSpec((tm // c, 1, rw), lambda i: (i, 0, 0))],
        out_shape=[jax.ShapeDtypeStruct((7, tp, rw), BF16 if RWKV_SCAN_MODE == "bf16" else F32),
                   jax.ShapeDtypeStruct((2, tp, rw), F32), jax.ShapeDtypeStruct((tp // c, 1, rw), F32)],
        compiler_params=_cparams(("parallel",)),
        name="rwkv_prep",
    )(proj, proj, *args)


def _rwkv_scan_body(x_ref, post_ref, dl_ref, o_ref, s_ref, *, heads, c, mode):
    @pl.when(pl.program_id(2) == 0)
    def _():
        s_ref[...] = jnp.zeros_like(s_ref)

    ri = lax.broadcasted_iota(jnp.int32, (c, c), 0)
    ci = lax.broadcasted_iota(jnp.int32, (c, c), 1)
    strict = ci < ri
    incl = ci <= ri
    hs = range(heads)
    ld = lambda q, h: x_ref[q, :, h * HEAD_DIM:(h + 1) * HEAD_DIM]
    cat0 = lambda a, b: jnp.concatenate([a, b], axis=0)
    cat1 = lambda a, b: jnp.concatenate([a, b], axis=1)
    ar = [cat0(ld(0, h), ld(1, h)) for h in hs]
    m = [_mm(ar[h], cat0(ld(2, h), ld(3, h)), _NT, mode) for h in hs]
    s0 = [s_ref[h] for h in hs]
    ars = [_mm(ar[h], s0[h], _NT, mode) for h in hs]
    v = [ld(4, h) for h in hs]
    nk = [jnp.where(strict, m[h][:c, :c], 0.0) for h in hs]
    m_ak = [jnp.where(strict, m[h][:c, c:], 0.0) for h in hs]
    m_r = [cat1(jnp.where(incl, m[h][c:, :c], 0.0), jnp.where(incl, m[h][c:, c:], 0.0)) for h in hs]
    u = [ars[h][:c] + _mm(m_ak[h], v[h], _NN, mode) for h in hs]
    steps = c.bit_length() - 1
    for it in range(steps):
        if it + 1 < steps:
            both = [_mm(nk[h], cat1(u[h], nk[h]), _NN, mode) for h in hs]
            u = [u[h] + both[h][:, :HEAD_DIM] for h in hs]
            nk = [both[h][:, HEAD_DIM:] for h in hs]
        else:
            u = [u[h] + _mm(nk[h], u[h], _NN, mode) for h in hs]
    uv = [cat0(u[h].astype(v[h].dtype), v[h]) for h in hs]
    y = [ars[h][c:] + _mm(m_r[h], uv[h], _NN, mode) for h in hs]
    for h in hs:
        dl = dl_ref[0, :, h * HEAD_DIM:(h + 1) * HEAD_DIM]
        s_ref[h] = s0[h] * dl + _mm(uv[h], cat0(ld(5, h), ld(6, h)), _TN, mode)
    outs = []
    for h in hs:
        mean = jnp.mean(y[h], axis=-1, keepdims=True)
        yc = y[h] - mean
        var = jnp.mean(yc * yc, axis=-1, keepdims=True)
        hl = slice(h * HEAD_DIM, (h + 1) * HEAD_DIM)
        outs.append(yc * lax.rsqrt(var + RWKV_GN_EPS) * post_ref[0, :, hl] + post_ref[1, :, hl])
    o_ref[...] = jnp.concatenate(outs, axis=1).astype(o_ref.dtype)


def rwkv_scan(xs, post, dl, batch, lp):
    _, tp, rw = xs.shape
    c = RWKV_CHUNK
    nc = lp // c
    heads = _pick(rw // HEAD_DIM, (RWKV_HEADS_PER_STEP, 2))
    gw = heads * HEAD_DIM
    return pl.pallas_call(
        functools.partial(_rwkv_scan_body, heads=heads, c=c, mode=RWKV_SCAN_MODE),
        grid=(batch, rw // gw, nc),
        in_specs=[pl.BlockSpec((7, c, gw), lambda b, hg, ch: (0, b * nc + ch, hg)),
                  pl.BlockSpec((2, c, gw), lambda b, hg, ch: (0, b * nc + ch, hg)),
                  pl.BlockSpec((1, 1, gw), lambda b, hg, ch: (b * nc + ch, 0, hg))],
        out_specs=pl.BlockSpec((c, gw), lambda b, hg, ch: (b * nc + ch, hg)),
        out_shape=jax.ShapeDtypeStruct((tp, rw), BF16),
        scratch_shapes=[pltpu.VMEM((heads, HEAD_DIM, HEAD_DIM), F32)],
        compiler_params=_cparams(("parallel", "parallel", "arbitrary")),
        name="rwkv_scan",
    )(xs, post, dl)


def _rope(x, cos, sa, sb):
    parts = []
    for s in range(x.shape[1] // LANES):
        sl = slice(s * LANES, (s + 1) * LANES)
        xs = x[:, sl]
        parts.append(xs * cos[:, sl] + pltpu.roll(xs, LANES - ROPE_DIMS // 2, 1) * sa[:, sl]
                     + pltpu.roll(xs, ROPE_DIMS // 2, 1) * sb[:, sl])
    return parts[0] if len(parts) == 1 else jnp.concatenate(parts, axis=1)


def _kv_rope_body(kv_ref, cos_ref, sa_ref, sb_ref, o_ref):
    o_ref[...] = _rope(kv_ref[...].astype(F32), cos_ref[...], sa_ref[...], sb_ref[...])


def swa_kv_rope(proj, kv_off, kvb, tabs, batch, lp):
    tp = proj.shape[0]
    nb = lp // SWA_BLOCK
    tab = pl.BlockSpec((SWA_BLOCK, kvb), lambda b, n: (n, 0))
    return pl.pallas_call(
        _kv_rope_body,
        grid=(batch, nb),
        in_specs=[pl.BlockSpec((SWA_BLOCK, kvb), lambda b, n: (b * nb + n, kv_off // kvb)), tab, tab, tab],
        out_specs=pl.BlockSpec((SWA_BLOCK, kvb), lambda b, n: (b * nb + n, 0)),
        out_shape=jax.ShapeDtypeStruct((tp, kvb), F32),
        compiler_params=_cparams(("parallel", "parallel")),
        name="swa_kv_rope",
    )(proj, *tabs)


def _swa_body(sink_ref, q_ref, kc_ref, kp_ref, km_ref, cos_ref, sa_ref, sb_ref, o_ref, *, group, n_kv):
    n = pl.program_id(1)
    blk = SWA_BLOCK
    mode = SWA_MODE
    qb = group * HEAD_DIM
    rows = group * blk
    hks = range(n_kv)
    qs = []
    for hk in hks:
        q = _rope(q_ref[:, hk * qb:(hk + 1) * qb].astype(F32), cos_ref[...], sa_ref[...], sb_ref[...])
        q = q * (HEAD_DIM ** -0.5)
        qs.append(jnp.concatenate([q[:, gi * HEAD_DIM:(gi + 1) * HEAD_DIM] for gi in range(group)], axis=0))
    qi = lax.broadcasted_iota(jnp.int32, (rows, blk), 0) & (blk - 1)
    kj = lax.broadcasted_iota(jnp.int32, (rows, blk), 1)
    far = 1 << 20
    cur_ok = kj + jnp.where(n >= 1, 0, far) <= qi
    prev_ok = kj > qi + jnp.where(n >= 2, 0, far)
    mi = lax.broadcasted_iota(jnp.int32, (rows, N_META), 0) & (blk - 1)
    mj = lax.broadcasted_iota(jnp.int32, (rows, N_META), 1)
    meta_ok = mj <= mi - PAD_FRONT + jnp.where(n >= 1, far, 0)
    kk, vv = slice(0, HEAD_DIM), slice(HEAD_DIM, 2 * HEAD_DIM)
    sink = [jnp.concatenate([jnp.full((blk, 1), sink_ref[hk * group + gi], F32) for gi in range(group)], axis=0)
            for hk in hks]
    s_c = [jnp.where(cur_ok, _mm(qs[hk], kc_ref[hk, :, kk], _NT, mode), NEG_INF) for hk in hks]
    s_p = [jnp.where(prev_ok, _mm(qs[hk], kp_ref[hk, :, kk], _NT, mode), NEG_INF) for hk in hks]
    s_m = [jnp.where(meta_ok, _mm(qs[hk], km_ref[hk, :, kk], _NT, mode), NEG_INF) for hk in hks]
    rmax = lambda s: jnp.max(s, axis=-1, keepdims=True)
    rsum = lambda s: jnp.sum(s, axis=-1, keepdims=True)
    mx = [jnp.maximum(jnp.maximum(rmax(s_c[hk]), rmax(s_p[hk])), jnp.maximum(rmax(s_m[hk]), sink[hk])) for hk in hks]
    p_c = [jnp.exp(s_c[hk] - mx[hk]) for hk in hks]
    p_p = [jnp.exp(s_p[hk] - mx[hk]) for hk in hks]
    p_m = [jnp.exp(s_m[hk] - mx[hk]) for hk in hks]
    den = [rsum(p_c[hk]) + rsum(p_p[hk]) + rsum(p_m[hk]) + jnp.exp(sink[hk] - mx[hk]) for hk in hks]
    acc = [_mm(p_c[hk], kc_ref[hk, :, vv], _NN, mode) + _mm(p_p[hk], kp_ref[hk, :, vv], _NN, mode)
           + _mm(p_m[hk], km_ref[hk, :, vv], _NN, mode) for hk in hks]
    out = [acc[hk] / den[hk] for hk in hks]
    o_ref[...] = jnp.concatenate([out[hk][gi * blk:(gi + 1) * blk, :] for hk in hks for gi in range(group)],
                                 axis=1).astype(o_ref.dtype)


def swa_attention(proj, q_off, kvh, sinks, qtabs, batch, lp, n_q_heads, n_kv_heads):
    tp = proj.shape[0]
    nb = lp // SWA_BLOCK
    group = n_q_heads // n_kv_heads
    qb = group * HEAD_DIM
    sw = n_q_heads * HEAD_DIM
    assert q_off % sw == 0
    tab = pl.BlockSpec((SWA_BLOCK, qb), lambda b, n: (n, 0))
    kv_spec = lambda f: pl.BlockSpec((n_kv_heads, SWA_BLOCK, LANES), f)
    return pl.pallas_call(
        functools.partial(_swa_body, group=group, n_kv=n_kv_heads),
        grid=(batch, nb),
        in_specs=[
            pl.BlockSpec(memory_space=pltpu.SMEM),
            pl.BlockSpec((SWA_BLOCK, sw), lambda b, n: (b * nb + n, q_off // sw)),
            kv_spec(lambda b, n: (0, b * nb + n, 0)),
            kv_spec(lambda b, n: (0, b * nb + jnp.maximum(n - 1, 0), 0)),
            pl.BlockSpec((n_kv_heads, N_META, LANES),
                         lambda b, n: (0, b * (lp // N_META) + PAD_FRONT // N_META, 0)),
            tab, tab, tab,
        ],
        out_specs=pl.BlockSpec((SWA_BLOCK, sw), lambda b, n: (b * nb + n, 0)),
        out_shape=jax.ShapeDtypeStruct((tp, sw), BF16),
        compiler_params=_cparams(("parallel", "parallel")),
        name="swa_attention",
    )(sinks, proj, kvh, kvh, kvh, *qtabs)


def _hgrn_body(q_ref, f_ref, i_ref, g_ref, lb_ref, gain_ref, o_ref, st_ref, *, c, heads, sub):
    @pl.when(pl.program_id(2) == 0)
    def _():
        st_ref[...] = jnp.zeros_like(st_ref)

    dk = HGRN_HEAD_DIM
    hs = range(heads)
    nb = c // sub
    ri = lax.broadcasted_iota(jnp.int32, (c, c), 0)
    ci = lax.broadcasted_iota(jnp.int32, (c, c), 1)
    incl = ci <= ri
    ltri = jnp.where(incl, 1.0, 0.0)
    cs = lax.broadcasted_iota(jnp.int32, (sub, c), 1)
    col = lambda ref, h: ref[:, h * dk:(h + 1) * dk].astype(F32)
    q = [col(q_ref, h) for h in hs]
    iv = [col(i_ref, h) for h in hs]
    f = [col(lb_ref, h) + (1.0 - col(lb_ref, h)) * _sigmoid(col(f_ref, h)) for h in hs]
    kx = [1.0 - f[h] for h in hs]
    gc = [_select_sum(ltri, jnp.log(f[h])) for h in hs]
    st = [st_ref[h] for h in hs]
    mode = HGRN_MODE
    o_inter = [_mm(q[h] * jnp.exp(gc[h]), st[h], _NT, mode) for h in hs]
    for h in hs:
        g_last = gc[h][c - 1:c, :]
        st_ref[h] = st[h] * jnp.exp(g_last) + _mm(iv[h], kx[h] * jnp.exp(g_last - gc[h]), _TN, mode)
    attn = []
    for h in hs:
        blocks = []
        for b in range(nb):
            rows = slice(b * sub, (b + 1) * sub)
            qb, gb = q[h][rows, :], gc[h][rows, :]
            if b == 0:
                blk = jnp.zeros((sub, c), F32)
            else:
                ref = gc[h][b * sub:b * sub + 1, :]
                k_sc = kx[h] * jnp.exp(jnp.minimum(ref - gc[h], 0.0))
                blk = _mm(qb * jnp.exp(gb - ref), k_sc, _NT, mode)
            for s in range(b * sub, (b + 1) * sub):
                dec = jnp.exp(jnp.minimum(gb - gc[h][s:s + 1, :], 0.0))
                blk = jnp.where(cs == s, jnp.sum(qb * kx[h][s:s + 1, :] * dec, axis=-1, keepdims=True), blk)
            blocks.append(blk)
        attn.append(jnp.where(incl, jnp.concatenate(blocks, axis=0), 0.0))
    outs = []
    for h in hs:
        o = _mm(attn[h], iv[h], _NN, mode) + o_inter[h]
        on = o * lax.rsqrt(jnp.mean(o * o, axis=-1, keepdims=True) + NORM_EPS) * col(gain_ref, h)
        g = col(g_ref, h)
        outs.append(on * (g * _sigmoid(g)))
    o_ref[...] = jnp.concatenate(outs, axis=1).astype(o_ref.dtype)


def hgrn_mix(proj, hg_off, lb, gain, batch, lp):
    tp = proj.shape[0]
    hw = lb.shape[0]
    dk = HGRN_HEAD_DIM
    nh = hw // dk
    c = CHUNK
    nc = lp // c
    heads = _pick(nh, (HGRN_HEADS_PER_STEP, 2, 1))
    gw = heads * dk
    blk = lambda part: pl.BlockSpec((c, gw), lambda b, h, ch: (b * nc + ch, (hg_off + part * hw) // gw + h))
    vec = pl.BlockSpec((1, gw), lambda b, h, ch: (0, h))
    return pl.pallas_call(
        functools.partial(_hgrn_body, c=c, heads=heads, sub=HGRN_SUB),
        grid=(batch, nh // heads, nc),
        in_specs=[blk(0), blk(1), blk(2), blk(3), vec, vec],
        out_specs=pl.BlockSpec((c, gw), lambda b, h, ch: (b * nc + ch, h)),
        out_shape=jax.ShapeDtypeStruct((tp, hw), BF16),
        scratch_shapes=[pltpu.VMEM((heads, dk, dk), F32)],
        compiler_params=_cparams(("parallel", "parallel", "arbitrary")),
        name="hgrn_mix",
    )(proj, proj, proj, proj, lb.reshape(1, hw), gain.reshape(1, hw))


def _out_proj_body(h_ref, ya_ref, yb_ref, yc_ref, wa_ref, wb_ref, wc_ref, o_ref):
    o_ref[...] = (h_ref[...] + _dot(ya_ref[...], wa_ref[...]) + _dot(yb_ref[...], wb_ref[...])
                  + _dot(yc_ref[...], wc_ref[...]))


def out_proj(h, ya, yb, yc, w_bf16):
    tp, d = h.shape
    wa, wb, wc = ya.shape[1], yb.shape[1], yc.shape[1]
    tm = _pick(tp, (512, 256, 128))
    tn = _pick(d, (1024, 512, 256, 128))
    return pl.pallas_call(
        _out_proj_body,
        grid=(tp // tm, d // tn),
        in_specs=[
            pl.BlockSpec((tm, tn), lambda i, j: (i, j)),
            pl.BlockSpec((tm, wa), lambda i, j: (i, 0)),
            pl.BlockSpec((tm, wb), lambda i, j: (i, 0)),
            pl.BlockSpec((tm, wc), lambda i, j: (i, 0)),
            pl.BlockSpec((wa, tn), lambda i, j: (0, j)),
            pl.BlockSpec((wb, tn), lambda i, j: (0, j)),
            pl.BlockSpec((wc, tn), lambda i, j: (0, j)),
        ],
        out_specs=pl.BlockSpec((tm, tn), lambda i, j: (i, j)),
        out_shape=jax.ShapeDtypeStruct((tp, d), F32),
        compiler_params=_cparams(("parallel", "parallel")),
        name="out_proj",
    )(h, ya, yb, yc, w_bf16[:wa], w_bf16[wa:wa + wb], w_bf16[wa + wb:])


def _norm_t_body(h_ref, g_ref, o_ref):
    x = h_ref[...]
    ms = jnp.mean(x * x, axis=-1, keepdims=True)
    y = x * lax.rsqrt(ms + NORM_EPS) * g_ref[...]
    o_ref[...] = y.T.astype(o_ref.dtype)


def norm_transposed(h, gain):
    tp, d = h.shape
    tm = _pick(tp, (512, 256, 128))
    return pl.pallas_call(
        _norm_t_body,
        grid=(tp // tm,),
        in_specs=[pl.BlockSpec((tm, d), lambda i: (i, 0)), pl.BlockSpec((1, d), lambda i: (0, 0))],
        out_specs=pl.BlockSpec((None, d, tm), lambda i: (i, 0, 0)),
        out_shape=jax.ShapeDtypeStruct((tp // tm, d, tm), BF16),
        compiler_params=_cparams(("parallel",)),
        name="peer_norm_t",
    )(h, gain.reshape(1, d))


def _top_values(arrays, k):
    rows = lax.broadcasted_iota(jnp.int32, (k, 1), 0)
    arrays = list(arrays)
    outs = [jnp.zeros((k, s.shape[1]), F32) for s in arrays]
    for r in range(k):
        ms = [jnp.max(s, axis=0, keepdims=True) for s in arrays]
        outs = [jnp.where(rows == r, m, o) for m, o in zip(ms, outs)]
        if r + 1 < k:
            arrays = [jnp.where(s == m, -jnp.inf, s) for s, m in zip(arrays, ms)]
    return outs


def _pair_candidates(v1, v2, topk):
    sub8 = lax.broadcasted_iota(jnp.int32, (8, 1), 0)
    parts = [v1[0:1, :] + v2]
    i = 1
    while topk // (i + 1) > 1:
        n_j = topk // (i + 1)
        parts.append(jnp.where(sub8 < n_j, v1[i:i + 1, :] + v2[0:8, :], -jnp.inf))
        i += 1
    parts.append(v1[i:, :] + v2[0:1, :])
    return jnp.concatenate(parts, axis=0)


def _peer_route_body(wq_ref, zn_ref, keys_ref, s1_ref, s2_ref, e1_ref, e2_ref, thr_ref, *, half, topk, hp):
    qt = _dot(wq_ref[...], zn_ref[...])
    hs = range(hp)
    s1 = [_dot(keys_ref[h, 0], qt[2 * h * half:(2 * h + 1) * half], HI) for h in hs]
    s2 = [_dot(keys_ref[h, 1], qt[(2 * h + 1) * half:(2 * h + 2) * half], HI) for h in hs]
    tops = _top_values(s1 + s2, topk)
    v1, v2 = tops[:hp], tops[hp:]
    best = _top_values([_pair_candidates(v1[h], v2[h], topk) for h in hs], topk)
    for h in hs:
        mx = best[h][0:1, :]
        z = jnp.sum(jnp.exp(best[h] - mx), axis=0, keepdims=True)
        s1_ref[h] = s1[h]
        s2_ref[h] = s2[h]
        e1_ref[h] = jnp.exp(s1[h] - v1[h][0:1, :]) / z
        e2_ref[h] = jnp.exp(s2[h] - v2[h][0:1, :])
        thr_ref[h] = best[h][topk - 1:topk, :]


def peer_route(znt, wqt_bf16, keys):
    nt, d, tm = znt.shape
    nh, _, nk, half = keys.shape
    hp = _pick(nh, (PEER_ROUTE_HEADS_PER_STEP, 2, 1))
    sc = pl.BlockSpec((None, hp, nk, tm), lambda i, h: (i, h, 0, 0))
    big = jax.ShapeDtypeStruct((nt, nh, nk, tm), F32)
    return pl.pallas_call(
        functools.partial(_peer_route_body, half=half, topk=PEER_TOPK, hp=hp),
        grid=(nt, nh // hp),
        in_specs=[
            pl.BlockSpec((hp * 2 * half, d), lambda i, h: (h, 0)),
            pl.BlockSpec((None, d, tm), lambda i, h: (i, 0, 0)),
            pl.BlockSpec((hp, 2, nk, half), lambda i, h: (h, 0, 0, 0)),
        ],
        out_specs=[sc, sc, sc, sc, pl.BlockSpec((None, hp, 1, tm), lambda i, h: (i, h, 0, 0))],
        out_shape=[big, big, big, big, jax.ShapeDtypeStruct((nt, nh, 1, tm), F32)],
        compiler_params=_cparams(("parallel", "arbitrary")),
        name="peer_route",
    )(wqt_bf16, znt, keys)


def _gelu(x):
    return 0.5 * x * (1.0 + lax.erf(x * (2.0 ** -0.5)))


def _peer_expert_body(zn_ref, u_ref, vt_ref, s1_ref, s2_ref, e1_ref, e2_ref, thr_ref, o_ref, a0_ref, a1_ref,
                      hid0_ref, hid1_ref, *, nk, nh, nj, split):
    j = pl.program_id(1)
    te = u_ref.shape[0]
    d = vt_ref.shape[0]
    nq = te // nk
    sub = nk // split
    rows = d // (nq * split)

    def project(q, hid_w):
        hid_w[q * nk:(q + 1) * nk, :] = _dot(u_ref[q * nk:(q + 1) * nk, :], zn_ref[...])

    def activate(q, r, hid_r, a_w):
        a = (j - 1) * nq + q
        rs = slice(r * sub, (r + 1) * sub)
        gate = None
        for h in range(nh):
            cand = s1_ref[h, pl.ds(a, 1), :] + s2_ref[h, rs, :]
            gh = jnp.where(cand >= thr_ref[h], e1_ref[h, pl.ds(a, 1), :] * e2_ref[h, rs, :], 0.0)
            gate = gh if gate is None else gate + gh
        es = slice(q * nk + r * sub, q * nk + (r + 1) * sub)
        a_w[es, :] = (gate * _gelu(hid_r[es, :])).astype(BF16)

    def apply(p, a_r):
        sl = slice(p * rows, (p + 1) * rows)
        o_ref[sl, :] += _dot(vt_ref[sl, :], a_r[...])

    def step(do_project, do_activate, do_apply, par):
        hid_w, hid_r = (hid0_ref, hid1_ref) if par == 0 else (hid1_ref, hid0_ref)
        a_w, a_r = (a1_ref, a0_ref) if par == 0 else (a0_ref, a1_ref)
        for q in range(nq):
            if do_project:
                project(q, hid_w)
            for r in range(split):
                if do_activate:
                    activate(q, r, hid_r, a_w)
                if do_apply:
                    apply(q * split + r, a_r)

    @pl.when(j == 0)
    def _():
        o_ref[...] = jnp.zeros_like(o_ref)
        step(True, False, False, 0)

    @pl.when(j == 1)
    def _():
        step(True, True, False, 1)

    for par in range(2):
        @pl.when((j >= 2) & (j < nj) & (lax.rem(j, 2) == par))
        def _(par=par):
            step(True, True, True, par)

    @pl.when(j == nj)
    def _():
        step(False, True, True, nj % 2)

    @pl.when(j == nj + 1)
    def _():
        step(False, False, True, (nj + 1) % 2)


def peer_experts(znt, u_bf16, v_bf16, s1, s2, e1, e2, thr):
    nt, d, tm = znt.shape
    ne = u_bf16.shape[0]
    _, nh, nk, _ = s1.shape
    te = _pick(ne, (512, 256, 128))
    nj = ne // te
    assert nj >= 2
    vt_bf16 = v_bf16.reshape(nj, te, d).transpose(0, 2, 1)
    once = dict(pipeline_mode=pl.Buffered(1))
    aux = pl.BlockSpec((None, nh, nk, tm), lambda i, j: (i, 0, 0, 0), **once)
    return pl.pallas_call(
        functools.partial(_peer_expert_body, nk=nk, nh=nh, nj=nj, split=PEER_SPLIT),
        grid=(nt, nj + 2),
        in_specs=[
            pl.BlockSpec((None, d, tm), lambda i, j: (i, 0, 0), **once),
            pl.BlockSpec((te, d), lambda i, j: (jnp.minimum(j, nj - 1), 0)),
            pl.BlockSpec((None, d, te), lambda i, j: (jnp.clip(j - 2, 0, nj - 1), 0, 0)),
            aux, aux, aux, aux,
            pl.BlockSpec((None, nh, 1, tm), lambda i, j: (i, 0, 0, 0), **once),
        ],
        out_specs=pl.BlockSpec((None, d, tm), lambda i, j: (i, 0, 0)),
        out_shape=jax.ShapeDtypeStruct((nt, d, tm), F32),
        scratch_shapes=[pltpu.VMEM((te, tm), BF16), pltpu.VMEM((te, tm), BF16),
                        pltpu.VMEM((te, tm), F32), pltpu.VMEM((te, tm), F32)],
        compiler_params=_cparams(("parallel", "arbitrary")),
        name="peer_experts",
    )(znt, u_bf16, vt_bf16, s1, s2, e1, e2, thr)


def _residual_t_body(h_ref, yt_ref, o_ref):
    o_ref[...] = h_ref[...] + yt_ref[...].T


def residual_add_transposed(h, yt):
    tp, d = h.shape
    tm = yt.shape[2]
    return pl.pallas_call(
        _residual_t_body,
        grid=(tp // tm,),
        in_specs=[pl.BlockSpec((tm, d), lambda i: (i, 0)), pl.BlockSpec((None, d, tm), lambda i: (i, 0, 0))],
        out_specs=pl.BlockSpec((tm, d), lambda i: (i, 0)),
        out_shape=jax.ShapeDtypeStruct((tp, d), F32),
        compiler_params=_cparams(("parallel",)),
        name="peer_residual",
    )(h, yt)


def _final_norm_body(h_ref, yt_ref, g_ref, o_ref):
    x = h_ref[...] + yt_ref[...].T
    ms = jnp.mean(x * x, axis=-1, keepdims=True)
    o_ref[...] = x * lax.rsqrt(ms + NORM_EPS) * g_ref[...]


def final_norm(h, yt, gain, batch, lp):
    tp, d = h.shape
    tm = yt.shape[2]
    per = tm // SWA_BLOCK
    nb = lp // SWA_BLOCK
    seq = lp - SWA_BLOCK
    rb = lambda b, n: b * nb + n + 1
    out = pl.pallas_call(
        _final_norm_body,
        grid=(batch, nb - 1),
        in_specs=[pl.BlockSpec((SWA_BLOCK, d), lambda b, n: (rb(b, n), 0)),
                  pl.BlockSpec((None, d, SWA_BLOCK), lambda b, n: (rb(b, n) // per, 0, rb(b, n) % per)),
                  pl.BlockSpec((1, d), lambda b, n: (0, 0))],
        out_specs=pl.BlockSpec((SWA_BLOCK, d), lambda b, n: (b * (nb - 1) + n, 0)),
        out_shape=jax.ShapeDtypeStruct((batch * seq, d), F32),
        compiler_params=_cparams(("parallel", "parallel")),
        name="final_norm",
    )(h, yt, gain.reshape(1, d))
    return out.reshape(batch, seq, d)


def _rope_tables(lp, width, rot_heads):
    pos = jnp.maximum(jnp.arange(lp, dtype=F32) - PAD_FRONT, 0.0)
    inv_freq = ROPE_THETA ** (-jnp.arange(0, ROPE_DIMS, 2, dtype=F32) / ROPE_DIMS)
    ang = pos[:, None] * inv_freq[None, :]
    cos, sin = jnp.cos(ang), jnp.sin(ang)
    half = ROPE_DIMS // 2
    rest = HEAD_DIM - ROPE_DIMS
    one = jnp.ones((lp, rest), F32)
    zero = jnp.zeros((lp, rest), F32)
    zh = jnp.zeros((lp, half), F32)
    c_head = jnp.concatenate([cos, cos, one], axis=1)
    a_head = jnp.concatenate([-sin, zh, zero], axis=1)
    b_head = jnp.concatenate([zh, sin, zero], axis=1)
    n_heads = width // HEAD_DIM
    idn = jnp.ones((lp, HEAD_DIM), F32)
    zer = jnp.zeros((lp, HEAD_DIM), F32)
    build = lambda rot, flat: jnp.concatenate([rot if i < rot_heads else flat for i in range(n_heads)], axis=1)
    return build(c_head, idn), build(a_head, zer), build(b_head, zer)


def _layout(d, rc, sw, kvw, hw):
    kvb = _roundup(2 * kvw, LANES)
    hg_off = _roundup(rc, hw)
    q_off = _roundup(hg_off + 4 * hw, sw)
    kv_off = _roundup(q_off + sw, kvb)
    total = _roundup(kv_off + kvb, 512)
    return q_off, kv_off, kvb, hg_off, total


def kernel(x, meta_tokens, norm_mix, norm_ffn, norm_final, w_in, w_out, rwkv_mu, rwkv_w0, rwkv_w2, rwkv_a0,
           rwkv_a2, rwkv_g2, rwkv_k_k, rwkv_k_a, rwkv_r_k, rwkv_lnx_w, rwkv_lnx_b, swa_sinks, hgrn_lb, hgrn_norm,
           peer_wq, peer_keys, peer_u, peer_v):
    batch, seq, d = x.shape
    depth = w_in.shape[0]
    lp = seq + SWA_BLOCK
    rw = rwkv_w0.shape[1]
    rc = rwkv_mu.shape[1]
    n_q = swa_sinks.shape[1]
    sw = n_q * HEAD_DIM
    hw = hgrn_lb.shape[1]
    kvw = (w_in.shape[2] - rc - sw - 4 * hw) // 2
    n_kv = kvw // HEAD_DIM
    q_off, kv_off, kvb, hg_off, n_tot = _layout(d, rc, sw, kvw, hw)

    pad = jnp.zeros((batch, PAD_FRONT, d), x.dtype)
    meta = jnp.broadcast_to(meta_tokens.astype(x.dtype)[None], (batch, N_META, d))
    h = jnp.concatenate([pad, meta, x], axis=1).reshape(batch * lp, d)

    q_tabs = _rope_tables(lp, sw // n_kv, sw // n_kv // HEAD_DIM)
    kv_tabs = _rope_tables(lp, kvb, n_kv)
    lb_all = jnp.cumsum(jax.nn.softmax(hgrn_lb.astype(F32), axis=0), axis=0)
    lb_all = lb_all - lb_all[0:1]

    for l in range(depth):
        w = w_in[l]
        zc = lambda n: jnp.zeros((d, n), w.dtype)
        w_re = jnp.concatenate([
            w[:, :rc], zc(hg_off - rc),
            w[:, rc + sw + 2 * kvw:], zc(q_off - hg_off - 4 * hw),
            w[:, rc:rc + sw], zc(kv_off - q_off - sw),
            w[:, rc + sw:rc + sw + 2 * kvw], zc(n_tot - kv_off - 2 * kvw)], axis=1).astype(BF16)
        proj = norm_matmul(h, norm_mix[l], w_re, lp)

        rp = dict(mu=rwkv_mu[l], w0=rwkv_w0[l], w2=rwkv_w2[l], a0=rwkv_a0[l], a2=rwkv_a2[l], g2=rwkv_g2[l],
                  k_k=rwkv_k_k[l], k_a=rwkv_k_a[l], r_k=rwkv_r_k[l], lnx_w=rwkv_lnx_w[l], lnx_b=rwkv_lnx_b[l])
        y_a = rwkv_scan(*rwkv_prep(proj, lp, rp), batch, lp)

        kv = swa_kv_rope(proj, kv_off, kvb, kv_tabs, batch, lp)
        kvh = jnp.stack([jnp.concatenate([kv[:, i * HEAD_DIM:(i + 1) * HEAD_DIM],
                                          kv[:, kvw + i * HEAD_DIM:kvw + (i + 1) * HEAD_DIM]], axis=1)
                         for i in range(n_kv)])
        y_b = swa_attention(proj, q_off, kvh, swa_sinks[l], q_tabs, batch, lp, n_q, n_kv)

        y_c = hgrn_mix(proj, hg_off, lb_all[l], hgrn_norm[l], batch, lp)

        h = out_proj(h, y_a, y_b, y_c, w_out[l].astype(BF16))

        znt = norm_transposed(h, norm_ffn[l])
        s1, s2, e1, e2, thr = peer_route(znt, peer_wq[l].T.astype(BF16), peer_keys[l])
        yt = peer_experts(znt, peer_u[l].astype(BF16), peer_v[l].astype(BF16), s1, s2, e1, e2, thr)
        if l + 1 < depth:
            h = residual_add_transposed(h, yt)

    return final_norm(h, yt, norm_final, batch, lp)
```

```python
import functools

import jax
import jax.numpy as jnp
from jax import lax
from jax.experimental import pallas as pl
from jax.experimental.pallas import tpu as pltpu

F32 = jnp.float32
BF16 = jnp.bfloat16
HI = lax.Precision.HIGHEST

N_META = 16
HEAD_DIM = 64
NORM_EPS = 1e-5
NEG_INF = -1e30
RWKV_GN_EPS = 64e-5
SWA_BLOCK = 128
PAD_FRONT = SWA_BLOCK - N_META
ROPE_THETA = 500000.0
ROPE_DIMS = HEAD_DIM // 4
HGRN_HEAD_DIM = 128
CHUNK = 64
RWKV_CHUNK = 64
PEER_TOPK = 16
LANES = 128
MXU_WIDTH = 256
VMEM_LIMIT_MB = 56
RWKV_HEADS_PER_STEP = 24
SWA_MODE = "bf16"
PEER_SPLIT = 4
PROJ_DTYPE = BF16
PEER_ROUTE_HEADS_PER_STEP = 4
HGRN_HEADS_PER_STEP = 8
HGRN_MODE = "bf16"
HGRN_SUB = 8
RWKV_SCAN_MODE = "bf16"


def _cparams(sem, vmem_mb=VMEM_LIMIT_MB, **kw):
    return pltpu.CompilerParams(dimension_semantics=sem, vmem_limit_bytes=vmem_mb * 1024 * 1024, **kw)


def _pick(n, cands):
    for c in cands:
        if n % c == 0:
            return c
    raise ValueError(f"no tile in {cands} divides {n}")


def _roundup(n, m):
    return -(-n // m) * m


def _dot(a, b, prec=None):
    return jnp.dot(a, b, preferred_element_type=F32, precision=prec)


def _dot_nt(a, b, prec=None):
    return lax.dot_general(a, b, (((1,), (1,)), ((), ())), preferred_element_type=F32, precision=prec)


def _dot_tn(a, b, prec=None):
    return lax.dot_general(a, b, (((0,), (0,)), ((), ())), preferred_element_type=F32, precision=prec)


def _row_in_batch(i, tm, lp):
    pos0 = lax.rem(i * tm, lp)
    rb = pos0 + lax.broadcasted_iota(jnp.int32, (tm, 1), 0)
    return jnp.where(rb >= lp, rb - lp, rb)


def _sigmoid(x):
    return 1.0 / (1.0 + jnp.exp(-x))


def _norm_mm_body(x_ref, g_ref, w_ref, o_ref, z_ref, *, lp, tm):
    i = pl.program_id(0)

    @pl.when(pl.program_id(1) == 0)
    def _():
        x = x_ref[...]
        ms = jnp.mean(x * x, axis=-1, keepdims=True)
        y = x * lax.rsqrt(ms + NORM_EPS) * g_ref[...]
        z_ref[...] = jnp.where(_row_in_batch(i, tm, lp) >= PAD_FRONT, y, 0.0).astype(BF16)

    o_ref[...] = _dot(z_ref[...], w_ref[...]).astype(o_ref.dtype)


def norm_matmul(h, gain, w_bf16, lp):
    tp, d = h.shape
    n = w_bf16.shape[1]
    tm = _pick(tp, (512, 256, 128))
    tn = _pick(n, (1024, 512, 256, 128))
    return pl.pallas_call(
        functools.partial(_norm_mm_body, lp=lp, tm=tm),
        grid=(tp // tm, n // tn),
        in_specs=[
            pl.BlockSpec((tm, d), lambda i, j: (i, 0)),
            pl.BlockSpec((1, d), lambda i, j: (0, 0)),
            pl.BlockSpec((d, tn), lambda i, j: (0, j)),
        ],
        out_specs=pl.BlockSpec((tm, tn), lambda i, j: (i, j)),
        out_shape=jax.ShapeDtypeStruct((tp, n), PROJ_DTYPE),
        scratch_shapes=[pltpu.VMEM((tm, d), BF16)],
        compiler_params=_cparams(("parallel", "arbitrary")),
        name="norm_in_proj",
    )(h, gain.reshape(1, d), w_bf16)


def _mm(a, b, dims, mode):
    if mode == "bf16":
        return lax.dot_general(a.astype(BF16), b.astype(BF16), (dims, ((), ())), preferred_element_type=F32)
    return lax.dot_general(a, b, (dims, ((), ())), preferred_element_type=F32, precision=HI)


_NN = ((1,), (0,))
_NT = ((1,), (1,))
_TN = ((0,), (0,))


def _split3(x):
    hi = x.astype(BF16)
    r1 = x - hi.astype(F32)
    mid = r1.astype(BF16)
    lo = (r1 - mid.astype(F32)).astype(BF16)
    return hi, mid, lo


def _dot_x3(a, b):
    a_hi = a.astype(BF16)
    a_lo = (a - a_hi.astype(F32)).astype(BF16)
    b_hi = b.astype(BF16)
    b_lo = (b - b_hi.astype(F32)).astype(BF16)
    return (_dot(a_hi, b_lo) + _dot(a_lo, b_hi)) + _dot(a_hi, b_hi)


def _select_sum(sel, x, x_on_left=False):
    sel = sel.astype(BF16)
    terms = [_dot(t, sel) if x_on_left else _dot(sel, t) for t in _split3(x)]
    return (terms[2] + terms[1]) + terms[0]


def _head_sums(x):
    shift = HEAD_DIM.bit_length() - 1
    er = lax.broadcasted_iota(jnp.int32, (LANES, LANES), 0) >> shift
    ec = lax.broadcasted_iota(jnp.int32, (LANES, LANES), 1) >> shift
    ones_bd = jnp.where(er == ec, 1.0, 0.0)
    return jnp.concatenate(
        [_select_sum(ones_bd, x[:, s * LANES:(s + 1) * LANES], x_on_left=True)
         for s in range(x.shape[1] // LANES)], axis=1)


def _rwkv_prep_body(pa_ref, prev_ref, mu_ref, w0_ref, w2_ref, a0_ref, a2_ref, g2_ref, kk_ref, ka_ref, rk_ref,
                    lnw_ref, lnb_ref, out_ref, post_ref, dl_ref, *, lp, tm, rw, rd, ri, rg, c):
    i = pl.program_id(0)
    x = pa_ref[...].astype(F32)
    rowi = lax.broadcasted_iota(jnp.int32, (tm, 1), 0)
    pr = prev_ref.shape[0]
    prev_row = jnp.where(lax.rem(i * tm, lp) == 0, 0.0, prev_ref[...].astype(F32)[pr - 1:pr, :])
    prev = jnp.where(rowi == 0, prev_row, pltpu.roll(x, 1, 0))
    xs = x + (prev - x) * mu_ref[...]
    r = xs[:, 0:rw]
    k = xs[:, rw:2 * rw]
    v = xs[:, 2 * rw:3 * rw]
    wl = xs[:, 3 * rw:3 * rw + rd]
    al = xs[:, 3 * rw + rd:3 * rw + rd + ri]
    gl = xs[:, 3 * rw + rd + ri:3 * rw + rd + ri + rg]
    wv = w0_ref[...] + _dot_x3(jnp.tanh(wl), w2_ref[...])
    nwv = -wv
    softplus = jnp.maximum(nwv, 0.0) + jnp.log(1.0 + jnp.exp(-jnp.abs(nwv)))
    logw = -jnp.exp(-softplus - 0.5)
    alr = _sigmoid(a0_ref[...] + _dot_x3(al, a2_ref[...]))
    g = _dot_x3(_sigmoid(gl), g2_ref[...])
    kkr = k * kk_ref[...]
    kk = kkr / jnp.maximum(jnp.sqrt(_head_sums(kkr * kkr)), 1e-12)
    k2 = k * (1.0 + (alr - 1.0) * ka_ref[...])
    shift = c.bit_length() - 1
    tr = lax.broadcasted_iota(jnp.int32, (tm, tm), 0)
    tc = lax.broadcasted_iota(jnp.int32, (tm, tm), 1)
    ltri = jnp.where(tc <= tr, jnp.where((tr >> shift) == (tc >> shift), 1.0, 0.0), 0.0)
    cum = _select_sum(ltri, logw)
    tot = jnp.concatenate(
        [jnp.broadcast_to(cum[(q + 1) * c - 1:(q + 1) * c, :], (c, rw)) for q in range(tm // c)], axis=0)
    p_inv = jnp.exp(-cum)
    p_rem = jnp.exp(tot - cum)
    kb = kk * alr
    od = out_ref.dtype
    out_ref[0] = (-kk * jnp.exp(cum - logw)).astype(od)
    out_ref[1] = (r * jnp.exp(cum)).astype(od)
    out_ref[2] = (kb * p_inv).astype(od)
    out_ref[3] = (k2 * p_inv).astype(od)
    out_ref[4] = v.astype(od)
    out_ref[5] = (kb * p_rem).astype(od)
    out_ref[6] = (k2 * p_rem).astype(od)
    post_ref[0] = lnw_ref[...] * g
    post_ref[1] = (lnb_ref[...] + _head_sums(r * k2 * rk_ref[...]) * v) * g
    for q in range(tm // c):
        dl_ref[q] = jnp.exp(tot[q * c:q * c + 1, :])


def rwkv_prep(proj, lp, p):
    tp = proj.shape[0]
    rw = p["w0"].shape[0]
    rd, ri, rg = p["w2"].shape[0], p["a2"].shape[0], p["g2"].shape[0]
    rc = 3 * rw + rd + ri + rg
    tm = 128
    c = RWKV_CHUNK
    pr = 8 * 4 // proj.dtype.itemsize
    row = lambda a: a.reshape(1, -1)
    full = lambda a: pl.BlockSpec(a.shape, lambda i: (0, 0))
    args = [row(p["mu"]), row(p["w0"]), p["w2"], row(p["a0"]), p["a2"], p["g2"], row(p["k_k"]), row(p["k_a"]),
            row(p["r_k"]), row(p["lnx_w"]), row(p["lnx_b"])]
    return pl.pallas_call(
        functools.partial(_rwkv_prep_body, lp=lp, tm=tm, rw=rw, rd=rd, ri=ri, rg=rg, c=c),
        grid=(tp // tm,),
        in_specs=[
            pl.BlockSpec((tm, rc), lambda i: (i, 0)),
            pl.BlockSpec((pr, rc), lambda i: (jnp.maximum(i * (tm // pr) - 1, 0), 0)),
        ] + [full(a) for a in args],
        out_specs=[pl.BlockSpec((7, tm, rw), lambda i: (0, i, 0)),
                   pl.BlockSpec((2, tm, rw), lambda i: (0, i, 0)),
                   pl.BlockSpec((tm // c, 1, rw), lambda i: (i, 0, 0))],
        out_shape=[jax.ShapeDtypeStruct((7, tp, rw), BF16 if RWKV_SCAN_MODE == "bf16" else F32),
                   jax.ShapeDtypeStruct((2, tp, rw), F32), jax.ShapeDtypeStruct((tp // c, 1, rw), F32)],
        compiler_params=_cparams(("parallel",)),
        name="rwkv_prep",
    )(proj, proj, *args)


def _rwkv_scan_body(x_ref, post_ref, dl_ref, o_ref, s_ref, *, heads, c, mode):
    @pl.when(pl.program_id(2) == 0)
    def _():
        s_ref[...] = jnp.zeros_like(s_ref)

    ri = lax.broadcasted_iota(jnp.int32, (c, c), 0)
    ci = lax.broadcasted_iota(jnp.int32, (c, c), 1)
    strict = ci < ri
    incl = ci <= ri
    hs = range(heads)
    ld = lambda q, h: x_ref[q, :, h * HEAD_DIM:(h + 1) * HEAD_DIM]
    cat0 = lambda a, b: jnp.concatenate([a, b], axis=0)
    cat1 = lambda a, b: jnp.concatenate([a, b], axis=1)
    ar = [cat0(ld(0, h), ld(1, h)) for h in hs]
    m = [_mm(ar[h], cat0(ld(2, h), ld(3, h)), _NT, mode) for h in hs]
    s0 = [s_ref[h] for h in hs]
    ars = [_mm(ar[h], s0[h], _NT, mode) for h in hs]
    v = [ld(4, h) for h in hs]
    nk = [jnp.where(strict, m[h][:c, :c], 0.0) for h in hs]
    m_ak = [jnp.where(strict, m[h][:c, c:], 0.0) for h in hs]
    m_r = [cat1(jnp.where(incl, m[h][c:, :c], 0.0), jnp.where(incl, m[h][c:, c:], 0.0)) for h in hs]
    u = [ars[h][:c] + _mm(m_ak[h], v[h], _NN, mode) for h in hs]
    steps = c.bit_length() - 1
    for it in range(steps):
        if it + 1 < steps:
            both = [_mm(nk[h], cat1(u[h], nk[h]), _NN, mode) for h in hs]
            u = [u[h] + both[h][:, :HEAD_DIM] for h in hs]
            nk = [both[h][:, HEAD_DIM:] for h in hs]
        else:
            u = [u[h] + _mm(nk[h], u[h], _NN, mode) for h in hs]
    uv = [cat0(u[h].astype(v[h].dtype), v[h]) for h in hs]
    y = [ars[h][c:] + _mm(m_r[h], uv[h], _NN, mode) for h in hs]
    for h in hs:
        dl = dl_ref[0, :, h * HEAD_DIM:(h + 1) * HEAD_DIM]
        s_ref[h] = s0[h] * dl + _mm(uv[h], cat0(ld(5, h), ld(6, h)), _TN, mode)
    outs = []
    for h in hs:
        mean = jnp.mean(y[h], axis=-1, keepdims=True)
        yc = y[h] - mean
        var = jnp.mean(yc * yc, axis=-1, keepdims=True)
        hl = slice(h * HEAD_DIM, (h + 1) * HEAD_DIM)
        outs.append(yc * lax.rsqrt(var + RWKV_GN_EPS) * post_ref[0, :, hl] + post_ref[1, :, hl])
    o_ref[...] = jnp.concatenate(outs, axis=1).astype(o_ref.dtype)


def rwkv_scan(xs, post, dl, batch, lp):
    _, tp, rw = xs.shape
    c = RWKV_CHUNK
    nc = lp // c
    heads = _pick(rw // HEAD_DIM, (RWKV_HEADS_PER_STEP, 2))
    gw = heads * HEAD_DIM
    return pl.pallas_call(
        functools.partial(_rwkv_scan_body, heads=heads, c=c, mode=RWKV_SCAN_MODE),
        grid=(batch, rw // gw, nc),
        in_specs=[pl.BlockSpec((7, c, gw), lambda b, hg, ch: (0, b * nc + ch, hg)),
                  pl.BlockSpec((2, c, gw), lambda b, hg, ch: (0, b * nc + ch, hg)),
                  pl.BlockSpec((1, 1, gw), lambda b, hg, ch: (b * nc + ch, 0, hg))],
        out_specs=pl.BlockSpec((c, gw), lambda b, hg, ch: (b * nc + ch, hg)),
        out_shape=jax.ShapeDtypeStruct((tp, rw), BF16),
        scratch_shapes=[pltpu.VMEM((heads, HEAD_DIM, HEAD_DIM), F32)],
        compiler_params=_cparams(("parallel", "parallel", "arbitrary")),
        name="rwkv_scan",
    )(xs, post, dl)


def _rope(x, cos, sa, sb):
    parts = []
    for s in range(x.shape[1] // LANES):
        sl = slice(s * LANES, (s + 1) * LANES)
        xs = x[:, sl]
        parts.append(xs * cos[:, sl] + pltpu.roll(xs, LANES - ROPE_DIMS // 2, 1) * sa[:, sl]
                     + pltpu.roll(xs, ROPE_DIMS // 2, 1) * sb[:, sl])
    return parts[0] if len(parts) == 1 else jnp.concatenate(parts, axis=1)


def _kv_rope_body(kv_ref, cos_ref, sa_ref, sb_ref, o_ref, *, n_kv):
    r = _rope(kv_ref[...].astype(F32), cos_ref[...], sa_ref[...], sb_ref[...])
    kvw = n_kv * HEAD_DIM
    for i in range(n_kv):
        o_ref[i] = jnp.concatenate([r[:, i * HEAD_DIM:(i + 1) * HEAD_DIM],
                                    r[:, kvw + i * HEAD_DIM:kvw + (i + 1) * HEAD_DIM]], axis=1).astype(o_ref.dtype)


def swa_kv_rope(proj, kv_off, kvb, tabs, batch, lp, n_kv):
    tp = proj.shape[0]
    nb = lp // SWA_BLOCK
    tab = pl.BlockSpec((SWA_BLOCK, kvb), lambda b, n: (n, 0))
    return pl.pallas_call(
        functools.partial(_kv_rope_body, n_kv=n_kv),
        grid=(batch, nb),
        in_specs=[pl.BlockSpec((SWA_BLOCK, kvb), lambda b, n: (b * nb + n, kv_off // kvb)), tab, tab, tab],
        out_specs=pl.BlockSpec((n_kv, SWA_BLOCK, LANES), lambda b, n: (0, b * nb + n, 0)),
        out_shape=jax.ShapeDtypeStruct((n_kv, tp, LANES), BF16 if SWA_MODE == "bf16" else F32),
        compiler_params=_cparams(("parallel", "parallel")),
        name="swa_kv_rope",
    )(proj, *tabs)


def _swa_body(sink_ref, q_ref, kc_ref, kp_ref, km_ref, cos_ref, sa_ref, sb_ref, o_ref, *, group, n_kv):
    n = pl.program_id(1)
    blk = SWA_BLOCK
    mode = SWA_MODE
    qb = group * HEAD_DIM
    rows = group * blk
    hks = range(n_kv)
    qs = []
    for hk in hks:
        q = _rope(q_ref[:, hk * qb:(hk + 1) * qb].astype(F32), cos_ref[...], sa_ref[...], sb_ref[...])
        q = q * (HEAD_DIM ** -0.5)
        qs.append(jnp.concatenate([q[:, gi * HEAD_DIM:(gi + 1) * HEAD_DIM] for gi in range(group)], axis=0))
    qi = lax.broadcasted_iota(jnp.int32, (rows, blk), 0) & (blk - 1)
    kj = lax.broadcasted_iota(jnp.int32, (rows, blk), 1)
    far = 1 << 20
    cur_ok = kj + jnp.where(n >= 1, 0, far) <= qi
    prev_ok = kj > qi + jnp.where(n >= 2, 0, far)
    mi = lax.broadcasted_iota(jnp.int32, (rows, N_META), 0) & (blk - 1)
    mj = lax.broadcasted_iota(jnp.int32, (rows, N_META), 1)
    meta_ok = mj <= mi - PAD_FRONT + jnp.where(n >= 1, far, 0)
    kk, vv = slice(0, HEAD_DIM), slice(HEAD_DIM, 2 * HEAD_DIM)
    sink = [jnp.concatenate([jnp.full((blk, 1), sink_ref[hk * group + gi], F32) for gi in range(group)], axis=0)
            for hk in hks]
    s_c = [jnp.where(cur_ok, _mm(qs[hk], kc_ref[hk, :, kk], _NT, mode), NEG_INF) for hk in hks]
    s_p = [jnp.where(prev_ok, _mm(qs[hk], kp_ref[hk, :, kk], _NT, mode), NEG_INF) for hk in hks]
    s_m = [jnp.where(meta_ok, _mm(qs[hk], km_ref[hk, :, kk], _NT, mode), NEG_INF) for hk in hks]
    rmax = lambda s: jnp.max(s, axis=-1, keepdims=True)
    rsum = lambda s: jnp.sum(s, axis=-1, keepdims=True)
    mx = [jnp.maximum(jnp.maximum(rmax(s_c[hk]), rmax(s_p[hk])), jnp.maximum(rmax(s_m[hk]), sink[hk])) for hk in hks]
    p_c = [jnp.exp(s_c[hk] - mx[hk]) for hk in hks]
    p_p = [jnp.exp(s_p[hk] - mx[hk]) for hk in hks]
    p_m = [jnp.exp(s_m[hk] - mx[hk]) for hk in hks]
    den = [rsum(p_c[hk]) + rsum(p_p[hk]) + rsum(p_m[hk]) + jnp.exp(sink[hk] - mx[hk]) for hk in hks]
    acc = [_mm(p_c[hk], kc_ref[hk, :, vv], _NN, mode) + _mm(p_p[hk], kp_ref[hk, :, vv], _NN, mode)
           + _mm(p_m[hk], km_ref[hk, :, vv], _NN, mode) for hk in hks]
    out = [acc[hk] / den[hk] for hk in hks]
    o_ref[...] = jnp.concatenate([out[hk][gi * blk:(gi + 1) * blk, :] for hk in hks for gi in range(group)],
                                 axis=1).astype(o_ref.dtype)


def swa_attention(proj, q_off, kvh, sinks, qtabs, batch, lp, n_q_heads, n_kv_heads):
    tp = proj.shape[0]
    nb = lp // SWA_BLOCK
    group = n_q_heads // n_kv_heads
    qb = group * HEAD_DIM
    sw = n_q_heads * HEAD_DIM
    assert q_off % sw == 0
    tab = pl.BlockSpec((SWA_BLOCK, qb), lambda b, n: (n, 0))
    kv_spec = lambda f: pl.BlockSpec((n_kv_heads, SWA_BLOCK, LANES), f)
    return pl.pallas_call(
        functools.partial(_swa_body, group=group, n_kv=n_kv_heads),
        grid=(batch, nb),
        in_specs=[
            pl.BlockSpec(memory_space=pltpu.SMEM),
            pl.BlockSpec((SWA_BLOCK, sw), lambda b, n: (b * nb + n, q_off // sw)),
            kv_spec(lambda b, n: (0, b * nb + n, 0)),
            kv_spec(lambda b, n: (0, b * nb + jnp.maximum(n - 1, 0), 0)),
            pl.BlockSpec((n_kv_heads, N_META, LANES),
                         lambda b, n: (0, b * (lp // N_META) + PAD_FRONT // N_META, 0)),
            tab, tab, tab,
        ],
        out_specs=pl.BlockSpec((SWA_BLOCK, sw), lambda b, n: (b * nb + n, 0)),
        out_shape=jax.ShapeDtypeStruct((tp, sw), BF16),
        compiler_params=_cparams(("parallel", "parallel")),
        name="swa_attention",
    )(sinks, proj, kvh, kvh, kvh, *qtabs)


def _hgrn_body(q_ref, f_ref, i_ref, g_ref, lb_ref, gain_ref, o_ref, st_ref, *, c, heads, sub):
    @pl.when(pl.program_id(2) == 0)
    def _():
        st_ref[...] = jnp.zeros_like(st_ref)

    dk = HGRN_HEAD_DIM
    hs = range(heads)
    nb = c // sub
    ri = lax.broadcasted_iota(jnp.int32, (c, c), 0)
    ci = lax.broadcasted_iota(jnp.int32, (c, c), 1)
    incl = ci <= ri
    ltri = jnp.where(incl, 1.0, 0.0)
    cs = lax.broadcasted_iota(jnp.int32, (sub, c), 1)
    col = lambda ref, h: ref[:, h * dk:(h + 1) * dk].astype(F32)
    q = [col(q_ref, h) for h in hs]
    iv = [col(i_ref, h) for h in hs]
    f = [col(lb_ref, h) + (1.0 - col(lb_ref, h)) * _sigmoid(col(f_ref, h)) for h in hs]
    kx = [1.0 - f[h] for h in hs]
    gc = [_select_sum(ltri, jnp.log(f[h])) for h in hs]
    st = [st_ref[h] for h in hs]
    mode = HGRN_MODE
    o_inter = [_mm(q[h] * jnp.exp(gc[h]), st[h], _NT, mode) for h in hs]
    for h in hs:
        g_last = gc[h][c - 1:c, :]
        st_ref[h] = st[h] * jnp.exp(g_last) + _mm(iv[h], kx[h] * jnp.exp(g_last - gc[h]), _TN, mode)
    attn = []
    for h in hs:
        blocks = []
        for b in range(nb):
            rows = slice(b * sub, (b + 1) * sub)
            qb, gb = q[h][rows, :], gc[h][rows, :]
            if b == 0:
                blk = jnp.zeros((sub, c), F32)
            else:
                ref = gc[h][b * sub:b * sub + 1, :]
                k_sc = kx[h] * jnp.exp(jnp.minimum(ref - gc[h], 0.0))
                blk = _mm(qb * jnp.exp(gb - ref), k_sc, _NT, mode)
            for s in range(b * sub, (b + 1) * sub):
                dec = jnp.exp(jnp.minimum(gb - gc[h][s:s + 1, :], 0.0))
                blk = jnp.where(cs == s, jnp.sum(qb * kx[h][s:s + 1, :] * dec, axis=-1, keepdims=True), blk)
            blocks.append(blk)
        attn.append(jnp.where(incl, jnp.concatenate(blocks, axis=0), 0.0))
    outs = []
    for h in hs:
        o = _mm(attn[h], iv[h], _NN, mode) + o_inter[h]
        on = o * lax.rsqrt(jnp.mean(o * o, axis=-1, keepdims=True) + NORM_EPS) * col(gain_ref, h)
        g = col(g_ref, h)
        outs.append(on * (g * _sigmoid(g)))
    o_ref[...] = jnp.concatenate(outs, axis=1).astype(o_ref.dtype)


def hgrn_mix(proj, hg_off, lb, gain, batch, lp):
    tp = proj.shape[0]
    hw = lb.shape[0]
    dk = HGRN_HEAD_DIM
    nh = hw // dk
    c = CHUNK
    nc = lp // c
    heads = _pick(nh, (HGRN_HEADS_PER_STEP, 2, 1))
    gw = heads * dk
    blk = lambda part: pl.BlockSpec((c, gw), lambda b, h, ch: (b * nc + ch, (hg_off + part * hw) // gw + h))
    vec = pl.BlockSpec((1, gw), lambda b, h, ch: (0, h))
    return pl.pallas_call(
        functools.partial(_hgrn_body, c=c, heads=heads, sub=HGRN_SUB),
        grid=(batch, nh // heads, nc),
        in_specs=[blk(0), blk(1), blk(2), blk(3), vec, vec],
        out_specs=pl.BlockSpec((c, gw), lambda b, h, ch: (b * nc + ch, h)),
        out_shape=jax.ShapeDtypeStruct((tp, hw), BF16),
        scratch_shapes=[pltpu.VMEM((heads, dk, dk), F32)],
        compiler_params=_cparams(("parallel", "parallel", "arbitrary")),
        name="hgrn_mix",
    )(proj, proj, proj, proj, lb.reshape(1, hw), gain.reshape(1, hw))


def _out_proj_body(h_ref, ya_ref, yb_ref, yc_ref, wa_ref, wb_ref, wc_ref, o_ref):
    o_ref[...] = (h_ref[...] + _dot(ya_ref[...], wa_ref[...]) + _dot(yb_ref[...], wb_ref[...])
                  + _dot(yc_ref[...], wc_ref[...]))


def out_proj(h, ya, yb, yc, w_bf16):
    tp, d = h.shape
    wa, wb, wc = ya.shape[1], yb.shape[1], yc.shape[1]
    tm = _pick(tp, (512, 256, 128))
    tn = _pick(d, (1024, 512, 256, 128))
    return pl.pallas_call(
        _out_proj_body,
        grid=(tp // tm, d // tn),
        in_specs=[
            pl.BlockSpec((tm, tn), lambda i, j: (i, j)),
            pl.BlockSpec((tm, wa), lambda i, j: (i, 0)),
            pl.BlockSpec((tm, wb), lambda i, j: (i, 0)),
            pl.BlockSpec((tm, wc), lambda i, j: (i, 0)),
            pl.BlockSpec((wa, tn), lambda i, j: (0, j)),
            pl.BlockSpec((wb, tn), lambda i, j: (0, j)),
            pl.BlockSpec((wc, tn), lambda i, j: (0, j)),
        ],
        out_specs=pl.BlockSpec((tm, tn), lambda i, j: (i, j)),
        out_shape=jax.ShapeDtypeStruct((tp, d), F32),
        compiler_params=_cparams(("parallel", "parallel")),
        name="out_proj",
    )(h, ya, yb, yc, w_bf16[:wa], w_bf16[wa:wa + wb], w_bf16[wa + wb:])


def _norm_t_body(h_ref, g_ref, o_ref):
    x = h_ref[...]
    ms = jnp.mean(x * x, axis=-1, keepdims=True)
    y = x * lax.rsqrt(ms + NORM_EPS) * g_ref[...]
    o_ref[...] = y.T.astype(o_ref.dtype)


def norm_transposed(h, gain):
    tp, d = h.shape
    tm = _pick(tp, (512, 256, 128))
    return pl.pallas_call(
        _norm_t_body,
        grid=(tp // tm,),
        in_specs=[pl.BlockSpec((tm, d), lambda i: (i, 0)), pl.BlockSpec((1, d), lambda i: (0, 0))],
        out_specs=pl.BlockSpec((None, d, tm), lambda i: (i, 0, 0)),
        out_shape=jax.ShapeDtypeStruct((tp // tm, d, tm), BF16),
        compiler_params=_cparams(("parallel",)),
        name="peer_norm_t",
    )(h, gain.reshape(1, d))


def _top_values(arrays, k):
    rows = lax.broadcasted_iota(jnp.int32, (k, 1), 0)
    arrays = list(arrays)
    outs = [jnp.zeros((k, s.shape[1]), F32) for s in arrays]
    for r in range(k):
        ms = [jnp.max(s, axis=0, keepdims=True) for s in arrays]
        outs = [jnp.where(rows == r, m, o) for m, o in zip(ms, outs)]
        if r + 1 < k:
            arrays = [jnp.where(s == m, -jnp.inf, s) for s, m in zip(arrays, ms)]
    return outs


def _pair_candidates(v1, v2, topk):
    sub8 = lax.broadcasted_iota(jnp.int32, (8, 1), 0)
    parts = [v1[0:1, :] + v2]
    i = 1
    while topk // (i + 1) > 1:
        n_j = topk // (i + 1)
        parts.append(jnp.where(sub8 < n_j, v1[i:i + 1, :] + v2[0:8, :], -jnp.inf))
        i += 1
    parts.append(v1[i:, :] + v2[0:1, :])
    return jnp.concatenate(parts, axis=0)


def _peer_route_body(wq_ref, zn_ref, keys_ref, s1_ref, s2_ref, e1_ref, e2_ref, thr_ref, *, half, topk, hp):
    qt = _dot(wq_ref[...], zn_ref[...])
    hs = range(hp)
    s1 = [_dot(keys_ref[h, 0], qt[2 * h * half:(2 * h + 1) * half], HI) for h in hs]
    s2 = [_dot(keys_ref[h, 1], qt[(2 * h + 1) * half:(2 * h + 2) * half], HI) for h in hs]
    tops = _top_values(s1 + s2, topk)
    v1, v2 = tops[:hp], tops[hp:]
    best = _top_values([_pair_candidates(v1[h], v2[h], topk) for h in hs], topk)
    for h in hs:
        mx = best[h][0:1, :]
        z = jnp.sum(jnp.exp(best[h] - mx), axis=0, keepdims=True)
        s1_ref[h] = s1[h]
        s2_ref[h] = s2[h]
        e1_ref[h] = jnp.exp(s1[h] - v1[h][0:1, :]) / z
        e2_ref[h] = jnp.exp(s2[h] - v2[h][0:1, :])
        thr_ref[h] = best[h][topk - 1:topk, :]


def peer_route(znt, wqt_bf16, keys):
    nt, d, tm = znt.shape
    nh, _, nk, half = keys.shape
    hp = _pick(nh, (PEER_ROUTE_HEADS_PER_STEP, 2, 1))
    sc = pl.BlockSpec((None, hp, nk, tm), lambda i, h: (i, h, 0, 0))
    big = jax.ShapeDtypeStruct((nt, nh, nk, tm), F32)
    return pl.pallas_call(
        functools.partial(_peer_route_body, half=half, topk=PEER_TOPK, hp=hp),
        grid=(nt, nh // hp),
        in_specs=[
            pl.BlockSpec((hp * 2 * half, d), lambda i, h: (h, 0)),
            pl.BlockSpec((None, d, tm), lambda i, h: (i, 0, 0)),
            pl.BlockSpec((hp, 2, nk, half), lambda i, h: (h, 0, 0, 0)),
        ],
        out_specs=[sc, sc, sc, sc, pl.BlockSpec((None, hp, 1, tm), lambda i, h: (i, h, 0, 0))],
        out_shape=[big, big, big, big, jax.ShapeDtypeStruct((nt, nh, 1, tm), F32)],
        compiler_params=_cparams(("parallel", "arbitrary")),
        name="peer_route",
    )(wqt_bf16, znt, keys)


def _gelu(x):
    return 0.5 * x * (1.0 + lax.erf(x * (2.0 ** -0.5)))


def _peer_expert_body(zn_ref, u_ref, vt_ref, s1_ref, s2_ref, e1_ref, e2_ref, thr_ref, o_ref, a0_ref, a1_ref,
                      hid0_ref, hid1_ref, *, nk, nh, nj, split):
    j = pl.program_id(1)
    te = u_ref.shape[0]
    d = vt_ref.shape[0]
    nq = te // nk
    sub = nk // split
    rows = d // (nq * split)

    def project(q, hid_w):
        hid_w[q * nk:(q + 1) * nk, :] = _dot(u_ref[q * nk:(q + 1) * nk, :], zn_ref[...])

    def activate(q, r, hid_r, a_w):
        a = (j - 1) * nq + q
        rs = slice(r * sub, (r + 1) * sub)
        gate = None
        for h in range(nh):
            cand = s1_ref[h, pl.ds(a, 1), :] + s2_ref[h, rs, :]
            gh = jnp.where(cand >= thr_ref[h], e1_ref[h, pl.ds(a, 1), :] * e2_ref[h, rs, :], 0.0)
            gate = gh if gate is None else gate + gh
        es = slice(q * nk + r * sub, q * nk + (r + 1) * sub)
        a_w[es, :] = (gate * _gelu(hid_r[es, :])).astype(BF16)

    def apply(p, a_r):
        sl = slice(p * rows, (p + 1) * rows)
        o_ref[sl, :] += _dot(vt_ref[sl, :], a_r[...])

    def step(do_project, do_activate, do_apply, par):
        hid_w, hid_r = (hid0_ref, hid1_ref) if par == 0 else (hid1_ref, hid0_ref)
        a_w, a_r = (a1_ref, a0_ref) if par == 0 else (a0_ref, a1_ref)
        for q in range(nq):
            if do_project:
                project(q, hid_w)
            for r in range(split):
                if do_activate:
                    activate(q, r, hid_r, a_w)
                if do_apply:
                    apply(q * split + r, a_r)

    @pl.when(j == 0)
    def _():
        o_ref[...] = jnp.zeros_like(o_ref)
        step(True, False, False, 0)

    @pl.when(j == 1)
    def _():
        step(True, True, False, 1)

    for par in range(2):
        @pl.when((j >= 2) & (j < nj) & (lax.rem(j, 2) == par))
        def _(par=par):
            step(True, True, True, par)

    @pl.when(j == nj)
    def _():
        step(False, True, True, nj % 2)

    @pl.when(j == nj + 1)
    def _():
        step(False, False, True, (nj + 1) % 2)


def peer_experts(znt, u_bf16, v_bf16, s1, s2, e1, e2, thr):
    nt, d, tm = znt.shape
    ne = u_bf16.shape[0]
    _, nh, nk, _ = s1.shape
    te = _pick(ne, (512, 256, 128))
    nj = ne // te
    assert nj >= 2
    vt_bf16 = v_bf16.reshape(nj, te, d).transpose(0, 2, 1)
    once = dict(pipeline_mode=pl.Buffered(1))
    aux = pl.BlockSpec((None, nh, nk, tm), lambda i, j: (i, 0, 0, 0), **once)
    return pl.pallas_call(
        functools.partial(_peer_expert_body, nk=nk, nh=nh, nj=nj, split=PEER_SPLIT),
        grid=(nt, nj + 2),
        in_specs=[
            pl.BlockSpec((None, d, tm), lambda i, j: (i, 0, 0), **once),
            pl.BlockSpec((te, d), lambda i, j: (jnp.minimum(j, nj - 1), 0)),
            pl.BlockSpec((None, d, te), lambda i, j: (jnp.clip(j - 2, 0, nj - 1), 0, 0)),
            aux, aux, aux, aux,
            pl.BlockSpec((None, nh, 1, tm), lambda i, j: (i, 0, 0, 0), **once),
        ],
        out_specs=pl.BlockSpec((None, d, tm), lambda i, j: (i, 0, 0)),
        out_shape=jax.ShapeDtypeStruct((nt, d, tm), F32),
        scratch_shapes=[pltpu.VMEM((te, tm), BF16), pltpu.VMEM((te, tm), BF16),
                        pltpu.VMEM((te, tm), F32), pltpu.VMEM((te, tm), F32)],
        compiler_params=_cparams(("parallel", "arbitrary")),
        name="peer_experts",
    )(znt, u_bf16, vt_bf16, s1, s2, e1, e2, thr)


def _residual_t_body(h_ref, yt_ref, o_ref):
    o_ref[...] = h_ref[...] + yt_ref[...].T


def residual_add_transposed(h, yt):
    tp, d = h.shape
    tm = yt.shape[2]
    return pl.pallas_call(
        _residual_t_body,
        grid=(tp // tm,),
        in_specs=[pl.BlockSpec((tm, d), lambda i: (i, 0)), pl.BlockSpec((None, d, tm), lambda i: (i, 0, 0))],
        out_specs=pl.BlockSpec((tm, d), lambda i: (i, 0)),
        out_shape=jax.ShapeDtypeStruct((tp, d), F32),
        compiler_params=_cparams(("parallel",)),
        name="peer_residual",
    )(h, yt)


def _final_norm_body(h_ref, yt_ref, g_ref, o_ref):
    x = h_ref[...] + yt_ref[...].T
    ms = jnp.mean(x * x, axis=-1, keepdims=True)
    o_ref[...] = x * lax.rsqrt(ms + NORM_EPS) * g_ref[...]


def final_norm(h, yt, gain, batch, lp):
    tp, d = h.shape
    tm = yt.shape[2]
    per = tm // SWA_BLOCK
    nb = lp // SWA_BLOCK
    seq = lp - SWA_BLOCK
    rb = lambda b, n: b * nb + n + 1
    out = pl.pallas_call(
        _final_norm_body,
        grid=(batch, nb - 1),
        in_specs=[pl.BlockSpec((SWA_BLOCK, d), lambda b, n: (rb(b, n), 0)),
                  pl.BlockSpec((None, d, SWA_BLOCK), lambda b, n: (rb(b, n) // per, 0, rb(b, n) % per)),
                  pl.BlockSpec((1, d), lambda b, n: (0, 0))],
        out_specs=pl.BlockSpec((SWA_BLOCK, d), lambda b, n: (b * (nb - 1) + n, 0)),
        out_shape=jax.ShapeDtypeStruct((batch * seq, d), F32),
        compiler_params=_cparams(("parallel", "parallel")),
        name="final_norm",
    )(h, yt, gain.reshape(1, d))
    return out.reshape(batch, seq, d)


def _rope_tables(lp, width, rot_heads):
    pos = jnp.maximum(jnp.arange(lp, dtype=F32) - PAD_FRONT, 0.0)
    inv_freq = ROPE_THETA ** (-jnp.arange(0, ROPE_DIMS, 2, dtype=F32) / ROPE_DIMS)
    ang = pos[:, None] * inv_freq[None, :]
    cos, sin = jnp.cos(ang), jnp.sin(ang)
    half = ROPE_DIMS // 2
    rest = HEAD_DIM - ROPE_DIMS
    one = jnp.ones((lp, rest), F32)
    zero = jnp.zeros((lp, rest), F32)
    zh = jnp.zeros((lp, half), F32)
    c_head = jnp.concatenate([cos, cos, one], axis=1)
    a_head = jnp.concatenate([-sin, zh, zero], axis=1)
    b_head = jnp.concatenate([zh, sin, zero], axis=1)
    n_heads = width // HEAD_DIM
    idn = jnp.ones((lp, HEAD_DIM), F32)
    zer = jnp.zeros((lp, HEAD_DIM), F32)
    build = lambda rot, flat: jnp.concatenate([rot if i < rot_heads else flat for i in range(n_heads)], axis=1)
    return build(c_head, idn), build(a_head, zer), build(b_head, zer)


def _layout(d, rc, sw, kvw, hw):
    kvb = _roundup(2 * kvw, LANES)
    hg_off = _roundup(rc, hw)
    q_off = _roundup(hg_off + 4 * hw, sw)
    kv_off = _roundup(q_off + sw, kvb)
    total = _roundup(kv_off + kvb, 512)
    return q_off, kv_off, kvb, hg_off, total


def kernel(x, meta_tokens, norm_mix, norm_ffn, norm_final, w_in, w_out, rwkv_mu, rwkv_w0, rwkv_w2, rwkv_a0,
           rwkv_a2, rwkv_g2, rwkv_k_k, rwkv_k_a, rwkv_r_k, rwkv_lnx_w, rwkv_lnx_b, swa_sinks, hgrn_lb, hgrn_norm,
           peer_wq, peer_keys, peer_u, peer_v):
    batch, seq, d = x.shape
    depth = w_in.shape[0]
    lp = seq + SWA_BLOCK
    rw = rwkv_w0.shape[1]
    rc = rwkv_mu.shape[1]
    n_q = swa_sinks.shape[1]
    sw = n_q * HEAD_DIM
    hw = hgrn_lb.shape[1]
    kvw = (w_in.shape[2] - rc - sw - 4 * hw) // 2
    n_kv = kvw // HEAD_DIM
    q_off, kv_off, kvb, hg_off, n_tot = _layout(d, rc, sw, kvw, hw)

    pad = jnp.zeros((batch, PAD_FRONT, d), x.dtype)
    meta = jnp.broadcast_to(meta_tokens.astype(x.dtype)[None], (batch, N_META, d))
    h = jnp.concatenate([pad, meta, x], axis=1).reshape(batch * lp, d)

    q_tabs = _rope_tables(lp, sw // n_kv, sw // n_kv // HEAD_DIM)
    kv_tabs = _rope_tables(lp, kvb, n_kv)
    lb_all = jnp.cumsum(jax.nn.softmax(hgrn_lb.astype(F32), axis=0), axis=0)
    lb_all = lb_all - lb_all[0:1]

    for l in range(depth):
        w = w_in[l]
        zc = lambda n: jnp.zeros((d, n), w.dtype)
        w_re = jnp.concatenate([
            w[:, :rc], zc(hg_off - rc),
            w[:, rc + sw + 2 * kvw:], zc(q_off - hg_off - 4 * hw),
            w[:, rc:rc + sw], zc(kv_off - q_off - sw),
            w[:, rc + sw:rc + sw + 2 * kvw], zc(n_tot - kv_off - 2 * kvw)], axis=1).astype(BF16)
        proj = norm_matmul(h, norm_mix[l], w_re, lp)

        rp = dict(mu=rwkv_mu[l], w0=rwkv_w0[l], w2=rwkv_w2[l], a0=rwkv_a0[l], a2=rwkv_a2[l], g2=rwkv_g2[l],
                  k_k=rwkv_k_k[l], k_a=rwkv_k_a[l], r_k=rwkv_r_k[l], lnx_w=rwkv_lnx_w[l], lnx_b=rwkv_lnx_b[l])
        y_a = rwkv_scan(*rwkv_prep(proj, lp, rp), batch, lp)

        kvh = swa_kv_rope(proj, kv_off, kvb, kv_tabs, batch, lp, n_kv)
        y_b = swa_attention(proj, q_off, kvh, swa_sinks[l], q_tabs, batch, lp, n_q, n_kv)

        y_c = hgrn_mix(proj, hg_off, lb_all[l], hgrn_norm[l], batch, lp)

        h = out_proj(h, y_a, y_b, y_c, w_out[l].astype(BF16))

        znt = norm_transposed(h, norm_ffn[l])
        s1, s2, e1, e2, thr = peer_route(znt, peer_wq[l].T.astype(BF16), peer_keys[l])
        yt = peer_experts(znt, peer_u[l].astype(BF16), peer_v[l].astype(BF16), s1, s2, e1, e2, thr)
        if l + 1 < depth:
            h = residual_add_transposed(h, yt)

    return final_norm(h, yt, norm_final, batch, lp)
```
